```python
import jax, jax.numpy as jnp
from jax import lax
import numpy as np

D_MODEL = 2048
BATCH = 2
SEQ = 8192
DEPTH = 4

CONV_W = D_MODEL // 4
CONV_K = 31
FOX_HEAD_DIM = 128
FOX_W = D_MODEL // 2
FOX_HEADS = FOX_W // FOX_HEAD_DIM
Q_BLOCK = 128
LRU_W = D_MODEL // 4
LRU_BLOCKS = 8
LRU_BW = LRU_W // LRU_BLOCKS
LRU_CONV_K = 4
LRU_C = 8.0
D_MIX = CONV_W + FOX_W + LRU_W

OFF_CONV = 0
OFF_Q = OFF_CONV + 2 * CONV_W
OFF_K = OFF_Q + FOX_W
OFF_V = OFF_K + FOX_W
OFF_F = OFF_V + FOX_W
OFF_LRU_X = OFF_F + FOX_HEADS
OFF_LRU_G = OFF_LRU_X + LRU_W
IN_COLS = OFF_LRU_G + LRU_W

N_GROUPS = 4
EXPERTS_PER_GROUP = 8
N_EXPERTS = N_GROUPS * EXPERTS_PER_GROUP
TOP_K = 2
D_FF_EXPERT = D_MODEL // 4
MOE_BLOCK = 128

EPS = 1e-6

kernel_name = 'hymba_style_conv_fox_rglru_hmoe_adaln'


def rms_norm(x, g):
    xf = x.astype(jnp.float32)
    y = xf * lax.rsqrt(jnp.mean(xf * xf, axis=-1, keepdims=True) + EPS)
    return (y * g.astype(jnp.float32)).astype(x.dtype)


def layer_norm(x, g, b):
    xf = x.astype(jnp.float32)
    mu = jnp.mean(xf, axis=-1, keepdims=True)
    var = jnp.mean(jnp.square(xf - mu), axis=-1, keepdims=True)
    y = (xf - mu) * lax.rsqrt(var + EPS) * g.astype(jnp.float32) + b.astype(jnp.float32)
    return y.astype(x.dtype)


def causal_depthwise_conv(x, w, b):
    k = w.shape[0]
    ch = x.shape[-1]
    y = lax.conv_general_dilated(
        x, w[:, None, :].astype(x.dtype), window_strides=(1,), padding=[(k - 1, 0)],
        dimension_numbers=('NWC', 'WIO', 'NWC'), feature_group_count=ch)
    return y + b.astype(x.dtype)


def conformer_conv(u, dw_w, dw_b, ln_g, ln_b):
    a, gate = jnp.split(u, 2, axis=-1)
    y = a * jax.nn.sigmoid(gate)
    y = causal_depthwise_conv(y, dw_w, dw_b)
    y = layer_norm(y, ln_g, ln_b)
    return jax.nn.silu(y)


def forgetting_attention(q, k, v, log_f):
    b, s, h, d = q.shape
    n_blk = s // Q_BLOCK
    scale = d ** -0.5
    cum = jnp.cumsum(log_f, axis=1).transpose(0, 2, 1)
    kh = k.transpose(0, 2, 1, 3)
    vh = v.transpose(0, 2, 1, 3)
    qb = q.reshape(b, n_blk, Q_BLOCK, h, d).transpose(1, 0, 3, 2, 4)
    cqb = cum.reshape(b, h, n_blk, Q_BLOCK).transpose(2, 0, 1, 3)
    kpos = jnp.arange(s)

    def one_block(args):
        qi, ci, blk = args
        logits = jnp.einsum('bhqd,bhkd->bhqk', qi, kh, preferred_element_type=jnp.float32) * scale
        logits = logits + ci[..., :, None] - cum[:, :, None, :]
        qpos = blk * Q_BLOCK + jnp.arange(Q_BLOCK)
        logits = jnp.where(kpos[None, :] <= qpos[:, None], logits, -jnp.inf)
        p = jax.nn.softmax(logits, axis=-1).astype(vh.dtype)
        return jnp.einsum('bhqk,bhkd->bhqd', p, vh)

    o = lax.map(one_block, (qb, cqb, jnp.arange(n_blk)))
    return o.transpose(1, 0, 3, 2, 4).reshape(b, s, h * d)


def rg_lru(u, conv_w, conv_b, w_r, b_r, w_i, b_i, lam):
    b, s, w = u.shape
    xc = causal_depthwise_conv(u, conv_w, conv_b)
    xr = xc.reshape(b, s, LRU_BLOCKS, LRU_BW)
    r = jax.nn.sigmoid((jnp.einsum('bsnc,ncd->bsnd', xr, w_r).reshape(b, s, w) + b_r).astype(jnp.float32))
    i = jax.nn.sigmoid(jnp.einsum('bsnc,ncd->bsnd', xr, w_i).reshape(b, s, w) + b_i)
    log_a = -LRU_C * r * jax.nn.softplus(-lam.astype(jnp.float32))
    a = jnp.exp(log_a)
    bt = jnp.sqrt(-jnp.expm1(2.0 * log_a)) * (i * xc).astype(jnp.float32)

    def combine(left, right):
        a1, b1 = left
        a2, b2 = right
        return a1 * a2, a2 * b1 + b2

    _, h = lax.associative_scan(combine, (a, bt), axis=1)
    return h.astype(u.dtype)


def hierarchical_moe(xn, w_rg, b_rg, w_re, b_re, w_gate, w_up, w_down):
    b, s, d = xn.shape
    t = b * s
    xt = xn.reshape(t, d)
    lg1 = jnp.einsum('td,dg->tg', xt, w_rg, preferred_element_type=jnp.float32) + b_rg.astype(jnp.float32)
    p_grp, grp = lax.top_k(jax.nn.softmax(lg1, axis=-1), 1)
    lg2 = jnp.einsum('td,gde->tge', xt, w_re, preferred_element_type=jnp.float32) + b_re.astype(jnp.float32)
    lg2 = jnp.take_along_axis(lg2, grp[:, :, None], axis=1)[:, 0]
    p_exp, eidx = lax.top_k(jax.nn.softmax(lg2, axis=-1), TOP_K)
    p_exp = p_exp / jnp.sum(p_exp, axis=-1, keepdims=True)
    gates = (p_grp * p_exp).reshape(-1)
    eid = (grp * EXPERTS_PER_GROUP + eidx).reshape(-1)
    tok = jnp.repeat(jnp.arange(t), TOP_K)
    n_assign = t * TOP_K

    order = jnp.argsort(eid)
    eid_s, tok_s, gate_s = eid[order], tok[order], gates[order]
    counts = jnp.bincount(eid, length=N_EXPERTS)
    starts = jnp.cumsum(counts) - counts
    padded = (counts + MOE_BLOCK - 1) // MOE_BLOCK * MOE_BLOCK
    pends = jnp.cumsum(padded)
    pstarts = pends - padded
    dest = pstarts[eid_s] + jnp.arange(n_assign) - starts[eid_s]
    n_blocks = -(-n_assign // MOE_BLOCK) + N_EXPERTS
    n_rows = n_blocks * MOE_BLOCK
    row_tok = jnp.zeros((n_rows,), jnp.int32).at[dest].set(tok_s.astype(jnp.int32))
    row_gate = jnp.zeros((n_rows,), jnp.float32).at[dest].set(gate_s)
    block_exp = jnp.minimum(
        jnp.searchsorted(pends, jnp.arange(n_blocks) * MOE_BLOCK, side='right'), N_EXPERTS - 1)

    def expert_block(args):
        rows, e = args
        xb = xt[rows]
        hb = jax.nn.silu(xb @ w_gate[e]) * (xb @ w_up[e])
        return hb @ w_down[e]

    y = lax.map(expert_block, (row_tok.reshape(n_blocks, MOE_BLOCK), block_exp))
    y = (y.reshape(n_rows, d) * row_gate[:, None]).astype(xt.dtype)
    out = jnp.zeros((t, d), xt.dtype).at[row_tok].add(y)
    return out.reshape(b, s, d)


def setup_inputs(seed: int = 0) -> dict:
    key = jax.random.key(seed)
    ks = jax.random.split(key, 40)
    f32 = jnp.float32

    def nrm(k, shape, scale):
        return jax.random.normal(k, shape, f32) * scale

    u = jax.random.uniform(ks[20], (DEPTH, LRU_W), f32, 0.9, 0.999)
    a0 = u ** (1.0 / LRU_C)
    lru_lambda = jnp.log(a0) - jnp.log1p(-a0)
    return {
        'x': nrm(ks[0], (BATCH, SEQ, D_MODEL), 1.0),
        'c': nrm(ks[1], (BATCH, D_MODEL), 1.0),
        'w_ada': nrm(ks[2], (DEPTH, D_MODEL, 6 * D_MODEL), 0.5 * D_MODEL ** -0.5),
        'b_ada': nrm(ks[3], (DEPTH, 6 * D_MODEL), 0.01),
        'ln1_g': 1.0 + nrm(ks[4], (DEPTH, D_MODEL), 0.02),
        'w_in': nrm(ks[5], (DEPTH, D_MODEL, IN_COLS), D_MODEL ** -0.5),
        'conv_dw_w': nrm(ks[6], (DEPTH, CONV_K, CONV_W), CONV_K ** -0.5),
        'conv_dw_b': nrm(ks[7], (DEPTH, CONV_W), 0.01),
        'conv_ln_g': 1.0 + nrm(ks[8], (DEPTH, CONV_W), 0.02),
        'conv_ln_b': nrm(ks[9], (DEPTH, CONV_W), 0.01),
        'fox_f_bias': jax.random.uniform(ks[10], (DEPTH, FOX_HEADS), f32, 1.0, 4.0),
        'fox_out_g': 1.0 + nrm(ks[11], (DEPTH, FOX_W), 0.02),
        'lru_conv_w': nrm(ks[12], (DEPTH, LRU_CONV_K, LRU_W), LRU_CONV_K ** -0.5),
        'lru_conv_b': nrm(ks[13], (DEPTH, LRU_W), 0.01),
        'lru_w_r': nrm(ks[14], (DEPTH, LRU_BLOCKS, LRU_BW, LRU_BW), LRU_BW ** -0.5),
        'lru_b_r': nrm(ks[15], (DEPTH, LRU_W), 0.01),
        'lru_w_i': nrm(ks[16], (DEPTH, LRU_BLOCKS, LRU_BW, LRU_BW), LRU_BW ** -0.5),
        'lru_b_i': nrm(ks[17], (DEPTH, LRU_W), 0.01),
        'lru_lambda': lru_lambda,
        'lru_out_g': 1.0 + nrm(ks[18], (DEPTH, LRU_W), 0.02),
        'w_out': nrm(ks[19], (DEPTH, D_MIX, D_MODEL), D_MIX ** -0.5),
        'ln2_g': 1.0 + nrm(ks[21], (DEPTH, D_MODEL), 0.02),
        'w_router_group': nrm(ks[22], (DEPTH, D_MODEL, N_GROUPS), D_MODEL ** -0.5),
        'b_router_group': nrm(ks[23], (DEPTH, N_GROUPS), 0.01),
        'w_router_expert': nrm(ks[24], (DEPTH, N_GROUPS, D_MODEL, EXPERTS_PER_GROUP), D_MODEL ** -0.5),
        'b_router_expert': nrm(ks[25], (DEPTH, N_GROUPS, EXPERTS_PER_GROUP), 0.01),
        'w_gate': nrm(ks[26], (DEPTH, N_EXPERTS, D_MODEL, D_FF_EXPERT), D_MODEL ** -0.5),
        'w_up': nrm(ks[27], (DEPTH, N_EXPERTS, D_MODEL, D_FF_EXPERT), D_MODEL ** -0.5),
        'w_down': nrm(ks[28], (DEPTH, N_EXPERTS, D_FF_EXPERT, D_MODEL), D_FF_EXPERT ** -0.5),
        'final_g': 1.0 + nrm(ks[29], (D_MODEL,), 0.02),
    }


def reference(x, c, w_ada, b_ada, ln1_g, w_in, conv_dw_w, conv_dw_b, conv_ln_g, conv_ln_b,
              fox_f_bias, fox_out_g, lru_conv_w, lru_conv_b, lru_w_r, lru_b_r, lru_w_i, lru_b_i,
              lru_lambda, lru_out_g, w_out, ln2_g, w_router_group, b_router_group,
              w_router_expert, b_router_expert, w_gate, w_up, w_down, final_g):
    b, s, _ = x.shape
    cond = jax.nn.silu(c)
    for l in range(DEPTH):
        mod = cond @ w_ada[l] + b_ada[l]
        sh1, sc1, g1, sh2, sc2, g2 = jnp.split(mod[:, None, :], 6, axis=-1)

        h = rms_norm(x, ln1_g[l]) * (1 + sc1) + sh1
        z = jnp.einsum('bsd,dn->bsn', h, w_in[l])

        y_conv = conformer_conv(z[..., OFF_CONV:OFF_Q], conv_dw_w[l], conv_dw_b[l],
                                conv_ln_g[l], conv_ln_b[l])

        q = z[..., OFF_Q:OFF_K].reshape(b, s, FOX_HEADS, FOX_HEAD_DIM)
        k = z[..., OFF_K:OFF_V].reshape(b, s, FOX_HEADS, FOX_HEAD_DIM)
        v = z[..., OFF_V:OFF_F].reshape(b, s, FOX_HEADS, FOX_HEAD_DIM)
        log_f = jax.nn.log_sigmoid(z[..., OFF_F:OFF_LRU_X].astype(jnp.float32)
                                   + fox_f_bias[l].astype(jnp.float32))
        y_fox = rms_norm(forgetting_attention(q, k, v, log_f), fox_out_g[l])

        h_lru = rg_lru(z[..., OFF_LRU_X:OFF_LRU_G], lru_conv_w[l], lru_conv_b[l], lru_w_r[l],
                       lru_b_r[l], lru_w_i[l], lru_b_i[l], lru_lambda[l])
        y_lru = rms_norm(h_lru * jax.nn.gelu(z[..., OFF_LRU_G:IN_COLS]), lru_out_g[l])

        y = jnp.einsum('bsm,md->bsd', jnp.concatenate([y_conv, y_fox, y_lru], axis=-1), w_out[l])
        x = x + g1 * y

        h2 = rms_norm(x, ln2_g[l]) * (1 + sc2) + sh2
        x = x + g2 * hierarchical_moe(h2, w_router_group[l], b_router_group[l], w_router_expert[l],
                                      b_router_expert[l], w_gate[l], w_up[l], w_down[l])
    return rms_norm(x, final_g)
```

```python
import functools

import jax
import jax.numpy as jnp
from jax import lax
from jax.experimental import pallas as pl
from jax.experimental.pallas import tpu as pltpu

EPS = 1e-6
LRU_C = 8.0
TOP_K = 2
LANES = 128
MOE_ROWS = 128
VMEM_LIMIT = 56 * 1024 * 1024

F32 = jnp.float32
BF16 = jnp.bfloat16


def _pick(n, pref):
    if n <= pref:
        return n
    for t in range(pref, 7, -1):
        if n % t == 0 and t % 8 == 0:
            return t
    raise ValueError((n, pref))


def _params(*sem):
    return pltpu.CompilerParams(dimension_semantics=sem, vmem_limit_bytes=VMEM_LIMIT)


def _sigmoid(x):
    return jax.nn.sigmoid(x)


def _ada_kernel(c_ref, w_ref, b_ref, o_ref):
    c = c_ref[...]
    cond = (c * _sigmoid(c)).astype(BF16)
    o_ref[...] = jnp.dot(cond, w_ref[...].astype(BF16), preferred_element_type=F32) + b_ref[...]


def _ada_mod(c_pad, w_ada, b_ada):
    depth, d, n = w_ada.shape
    rows = c_pad.shape[0]
    tn = _pick(n, 1024)
    return pl.pallas_call(
        _ada_kernel,
        grid=(depth, n // tn),
        in_specs=[
            pl.BlockSpec((rows, d), lambda l, j: (0, 0)),
            pl.BlockSpec((None, d, tn), lambda l, j: (l, 0, j)),
            pl.BlockSpec((None, 1, tn), lambda l, j: (l, 0, j)),
        ],
        out_specs=pl.BlockSpec((None, rows, tn), lambda l, j: (l, 0, j)),
        out_shape=jax.ShapeDtypeStruct((depth, rows, n), F32),
        compiler_params=_params("parallel", "parallel"),
        name="ada_mod",
    )(c_pad, w_ada, b_ada.reshape(depth, 1, n))


def _modulated_norm(x, g, sc, sh):
    y = x * lax.rsqrt(jnp.mean(x * x, axis=-1, keepdims=True) + EPS) * g
    return y * (1.0 + sc) + sh


def _inproj_kernel(x_ref, g_ref, sc_ref, sh_ref, w_ref, wf_ref, z_ref, zf_ref, h_scr):
    @pl.when(pl.program_id(2) == 0)
    def _():
        hb = _modulated_norm(x_ref[...], g_ref[...], sc_ref[...], sh_ref[...]).astype(BF16)
        h_scr[...] = hb
        zf_ref[...] = jnp.dot(hb, wf_ref[...], preferred_element_type=F32)

    z_ref[...] = jnp.dot(h_scr[...], w_ref[...], preferred_element_type=F32)


def _inproj(x, g, sc, sh, w_main, w_f):
    b, s, d = x.shape
    n = w_main.shape[1]
    tm = _pick(s, 512)
    tn = _pick(n, 1024)
    return pl.pallas_call(
        _inproj_kernel,
        grid=(b, s // tm, n // tn),
        in_specs=[
            pl.BlockSpec((None, tm, d), lambda bi, i, j: (bi, i, 0)),
            pl.BlockSpec((1, d), lambda bi, i, j: (0, 0)),
            pl.BlockSpec((None, 1, d), lambda bi, i, j: (bi, 0, 0)),
            pl.BlockSpec((None, 1, d), lambda bi, i, j: (bi, 0, 0)),
            pl.BlockSpec((d, tn), lambda bi, i, j: (0, j)),
            pl.BlockSpec((d, LANES), lambda bi, i, j: (0, 0)),
        ],
        out_specs=[
            pl.BlockSpec((None, tm, tn), lambda bi, i, j: (bi, i, j)),
            pl.BlockSpec((None, tm, LANES), lambda bi, i, j: (bi, i, 0)),
        ],
        out_shape=[jax.ShapeDtypeStruct((b, s, n), F32), jax.ShapeDtypeStruct((b, s, LANES), F32)],
        scratch_shapes=[pltpu.VMEM((tm, d), BF16)],
        compiler_params=_params("parallel", "parallel", "arbitrary"),
        name="inproj",
    )(x, g, sc, sh, w_main, w_f)


def _conv_kernel(a_ref, g_ref, ah_ref, gh_ref, w_ref, b_ref, lg_ref, lb_ref, o_ref, buf, ybuf, *, taps, ts, halo, rc):
    cw = a_ref.shape[-1]
    buf[halo:halo + ts, :] = a_ref[...] * _sigmoid(g_ref[...])
    prev = ah_ref[...] * _sigmoid(gh_ref[...])
    buf[0:halo, :] = jnp.where(pl.program_id(1) > 0, prev, 0.0)
    for c in range(cw // LANES):
        cs = slice(c * LANES, (c + 1) * LANES)
        accs = [None] * (ts // rc)
        for k in range(taps):
            wk = w_ref[k:k + 1, cs]
            off = halo - (taps - 1) + k
            for r in range(ts // rc):
                term = wk * buf[off + r * rc:off + (r + 1) * rc, cs]
                accs[r] = term if accs[r] is None else accs[r] + term
        for r in range(ts // rc):
            ybuf[r * rc:(r + 1) * rc, cs] = accs[r] + b_ref[:, cs]
    y = ybuf[...]
    mu = jnp.mean(y, axis=-1, keepdims=True)
    yc = y - mu
    var = jnp.mean(yc * yc, axis=-1, keepdims=True)
    yn = yc * lax.rsqrt(var + EPS) * lg_ref[...] + lb_ref[...]
    o_ref[...] = (yn * _sigmoid(yn)).astype(o_ref.dtype)


def _conv_branch(z, cw, dw_w, dw_b, ln_g, ln_b):
    b, s, _ = z.shape
    taps = dw_w.shape[0]
    halo = 32
    assert taps - 1 <= halo
    ts = _pick(s, 256)
    rc = _pick(ts, 64)
    hb = ts // halo
    kern = functools.partial(_conv_kernel, taps=taps, ts=ts, halo=halo, rc=rc)
    cur = lambda col: pl.BlockSpec((None, ts, cw), lambda bi, i: (bi, i, col))
    prev = lambda col: pl.BlockSpec((None, halo, cw), lambda bi, i: (bi, jnp.maximum(i * hb - 1, 0), col))
    vec = pl.BlockSpec((1, cw), lambda bi, i: (0, 0))
    return pl.pallas_call(
        kern,
        grid=(b, s // ts),
        in_specs=[cur(0), cur(1), prev(0), prev(1),
                  pl.BlockSpec((taps, cw), lambda bi, i: (0, 0)), vec, vec, vec],
        out_specs=pl.BlockSpec((None, ts, cw), lambda bi, i: (bi, i, 0)),
        out_shape=jax.ShapeDtypeStruct((b, s, cw), BF16),
        scratch_shapes=[pltpu.VMEM((halo + ts, cw), F32), pltpu.VMEM((ts, cw), F32)],
        compiler_params=_params("parallel", "parallel"),
        name="conv_branch",
    )(z, z, z, z, dw_w, dw_b.reshape(1, cw), ln_g.reshape(1, cw), ln_b.reshape(1, cw))


def _shift_rows(v, d, fill):
    rows = lax.broadcasted_iota(jnp.int32, v.shape, 0)
    return jnp.where(rows >= d, pltpu.roll(v, d, axis=0), fill)


def _scan_linear(a, bv):
    d = 1
    while d < a.shape[0]:
        bv = a * _shift_rows(bv, d, 0.0) + bv
        a = a * _shift_rows(a, d, 1.0)
        d *= 2
    return a, bv


def _scan_sum(v):
    d = 1
    while d < v.shape[0]:
        v = v + _shift_rows(v, d, 0.0)
        d *= 2
    return v


def _cum_kernel(zf_ref, fb_ref, o_ref, car, *, ts, rs):
    @pl.when(pl.program_id(1) == 0)
    def _():
        car[...] = jnp.zeros_like(car)

    h = car[...]
    for r in range(ts // rs):
        x = zf_ref[r * rs:(r + 1) * rs, :] + fb_ref[...]
        log_f = jnp.minimum(x, 0.0) - jnp.log1p(jnp.exp(-jnp.abs(x)))
        v = _scan_sum(log_f) + h
        o_ref[r * rs:(r + 1) * rs, :] = v
        h = v[rs - 1:rs, :]
    car[...] = h


def _forget_cumsum(zf, f_bias_pad):
    b, s, _ = zf.shape
    ts = _pick(s, 512)
    rs = _pick(ts, 128)
    return pl.pallas_call(
        functools.partial(_cum_kernel, ts=ts, rs=rs),
        grid=(b, s // ts),
        in_specs=[pl.BlockSpec((None, ts, LANES), lambda bi, i: (bi, i, 0)),
                  pl.BlockSpec((1, LANES), lambda bi, i: (0, 0))],
        out_specs=pl.BlockSpec((None, ts, LANES), lambda bi, i: (bi, i, 0)),
        out_shape=jax.ShapeDtypeStruct((b, s, LANES), F32),
        scratch_shapes=[pltpu.VMEM((1, LANES), F32)],
        compiler_params=_params("parallel", "arbitrary"),
        name="forget_cumsum",
    )(zf, f_bias_pad)


def _gelu_tanh(x):
    return 0.5 * x * (1.0 + jnp.tanh(0.7978845608028654 * (x + 0.044715 * (x * x * x))))


def _lru_kernel(u_ref, uh_ref, gate_ref, cw_ref, cb_ref, wr_ref, br_ref, wi_ref, bi_ref, lam_ref, og_ref,
                o_ref, ubuf, abuf, bbuf, car, *, taps, ts, halo, rs):
    lw = u_ref.shape[-1]

    @pl.when(pl.program_id(1) == 0)
    def _():
        car[...] = jnp.zeros_like(car)

    ubuf[halo:halo + ts, :] = u_ref[...]
    ubuf[0:halo, :] = jnp.where(pl.program_id(1) > 0, uh_ref[...], 0.0)
    xc = cb_ref[...]
    for k in range(taps):
        off = halo - (taps - 1) + k
        xc = xc + cw_ref[k:k + 1, :] * ubuf[off:off + ts, :]
    xb = xc.astype(BF16)
    r = _sigmoid(jnp.dot(xb, wr_ref[...], preferred_element_type=F32) + br_ref[...])
    ig = _sigmoid(jnp.dot(xb, wi_ref[...], preferred_element_type=F32) + bi_ref[...])
    nl = -lam_ref[...]
    softplus = jnp.maximum(nl, 0.0) + jnp.log1p(jnp.exp(-jnp.abs(nl)))
    log_a = (-LRU_C) * r * softplus
    th = jnp.tanh(log_a)
    abuf[...] = jnp.exp(log_a)
    bbuf[...] = jnp.sqrt(-2.0 * th / (1.0 - th)) * (ig * xc)
    for c in range(lw // LANES):
        cs = slice(c * LANES, (c + 1) * LANES)
        h = car[:, cs]
        for q in range(ts // rs):
            rows = slice(q * rs, (q + 1) * rs)
            a_cum, b_cum = _scan_linear(abuf[rows, cs], bbuf[rows, cs])
            hv = b_cum + a_cum * h
            bbuf[rows, cs] = hv
            h = hv[rs - 1:rs, :]
        car[:, cs] = h
    y = bbuf[...] * _gelu_tanh(gate_ref[...])
    y = y * lax.rsqrt(jnp.mean(y * y, axis=-1, keepdims=True) + EPS) * og_ref[...]
    o_ref[...] = y.astype(o_ref.dtype)


def _lru_branch(z, lw, col_x, conv_w, conv_b, wr_bd, b_r, wi_bd, b_i, lam, out_g):
    b, s, _ = z.shape
    taps = conv_w.shape[0]
    halo = 8
    assert taps - 1 <= halo
    ts = _pick(s, 256)
    rs = _pick(ts, 128)
    hb = ts // halo
    kern = functools.partial(_lru_kernel, taps=taps, ts=ts, halo=halo, rs=rs)
    vec = pl.BlockSpec((1, lw), lambda bi, i: (0, 0))
    mat = pl.BlockSpec((lw, lw), lambda bi, i: (0, 0))
    return pl.pallas_call(
        kern,
        grid=(b, s // ts),
        in_specs=[
            pl.BlockSpec((None, ts, lw), lambda bi, i: (bi, i, col_x)),
            pl.BlockSpec((None, halo, lw), lambda bi, i: (bi, jnp.maximum(i * hb - 1, 0), col_x)),
            pl.BlockSpec((None, ts, lw), lambda bi, i: (bi, i, col_x + 1)),
            pl.BlockSpec((taps, lw), lambda bi, i: (0, 0)), vec, mat, vec, mat, vec, vec, vec,
        ],
        out_specs=pl.BlockSpec((None, ts, lw), lambda bi, i: (bi, i, 0)),
        out_shape=jax.ShapeDtypeStruct((b, s, lw), BF16),
        scratch_shapes=[pltpu.VMEM((halo + ts, lw), F32), pltpu.VMEM((ts, lw), F32),
                        pltpu.VMEM((ts, lw), F32), pltpu.VMEM((1, lw), F32)],
        compiler_params=_params("parallel", "arbitrary"),
        name="lru_branch",
    )(z, z, z, conv_w, conv_b.reshape(1, lw), wr_bd, b_r.reshape(1, lw), wi_bd, b_i.reshape(1, lw),
      lam.reshape(1, lw), out_g.reshape(1, lw))


def _attn_kernel(q_ref, k_ref, v_ref, cum_ref, cumt_ref, o_ref, *, tq, scale):
    h = pl.program_id(1)
    i = pl.program_id(2)
    q = q_ref[...].astype(BF16)
    lane = lax.broadcasted_iota(jnp.int32, cum_ref.shape, 1)
    cq = jnp.sum(jnp.where(lane == h, cum_ref[...], 0.0), axis=-1, keepdims=True)

    def step(j, carry, masked):
        m, l, acc = carry
        start = pl.multiple_of(j * tq, tq)
        ks = k_ref[pl.ds(start, tq), :].astype(BF16)
        vs = v_ref[pl.ds(start, tq), :].astype(BF16)
        s = lax.dot_general(q, ks, (((1,), (1,)), ((), ())), preferred_element_type=F32) * scale
        s = s + cq - cumt_ref[j]
        if masked:
            row = lax.broadcasted_iota(jnp.int32, s.shape, 0)
            col = lax.broadcasted_iota(jnp.int32, s.shape, 1)
            s = jnp.where(col <= row, s, -jnp.inf)
        m_new = jnp.maximum(m, jnp.max(s, axis=-1, keepdims=True))
        alpha = jnp.exp(m - m_new)
        p = jnp.exp(s - m_new)
        l = alpha * l + jnp.sum(p, axis=-1, keepdims=True)
        acc = alpha * acc + jnp.dot(p.astype(BF16), vs, preferred_element_type=F32)
        return m_new, l, acc

    init = (jnp.full((tq, 1), -jnp.inf, F32), jnp.zeros((tq, 1), F32), jnp.zeros(o_ref.shape, F32))
    carry = lax.fori_loop(0, i, lambda j, c: step(j, c, False), init)
    _, l, acc = step(i, carry, True)
    o_ref[...] = acc / l


def _attention(z, cum, cum_t, heads, hd, col_q, col_k, col_v):
    b, s, _ = z.shape
    tq = cum_t.shape[-1]
    kern = functools.partial(_attn_kernel, tq=tq, scale=hd ** -0.5)
    return pl.pallas_call(
        kern,
        grid=(b, heads, s // tq),
        in_specs=[
            pl.BlockSpec((None, tq, hd), lambda bi, h, i: (bi, i, col_q + h)),
            pl.BlockSpec((None, s, hd), lambda bi, h, i: (bi, 0, col_k + h)),
            pl.BlockSpec((None, s, hd), lambda bi, h, i: (bi, 0, col_v + h)),
            pl.BlockSpec((None, tq, LANES), lambda bi, h, i: (bi, i, 0)),
            pl.BlockSpec((None, None, s // tq, 1, tq), lambda bi, h, i: (bi, h, 0, 0, 0)),
        ],
        out_specs=pl.BlockSpec((None, tq, hd), lambda bi, h, i: (bi, i, h)),
        out_shape=jax.ShapeDtypeStruct((b, s, heads * hd), F32),
        compiler_params=_params("parallel", "parallel", "parallel"),
        name="fox_attention",
    )(z, z, z, cum, cum_t)


def _outproj_kernel(yc_ref, o_ref, yl_ref, fg_ref, w1_ref, w2_ref, w3_ref, x_ref, g1_ref, out_ref, yf_scr):
    @pl.when(pl.program_id(2) == 0)
    def _():
        o = o_ref[...]
        yf = o * lax.rsqrt(jnp.mean(o * o, axis=-1, keepdims=True) + EPS) * fg_ref[...]
        yf_scr[...] = yf.astype(BF16)

    y = jnp.dot(yc_ref[...], w1_ref[...], preferred_element_type=F32)
    y = y + jnp.dot(yf_scr[...], w2_ref[...], preferred_element_type=F32)
    y = y + jnp.dot(yl_ref[...], w3_ref[...], preferred_element_type=F32)
    out_ref[...] = x_ref[...] + g1_ref[...] * y


def _outproj(y_conv, o_fox, y_lru, fox_g, w1, w2, w3, x, g1):
    b, s, d = x.shape
    cw, fw, lw = y_conv.shape[-1], o_fox.shape[-1], y_lru.shape[-1]
    tm = _pick(s, 512)
    tn = _pick(d, 1024)
    act = lambda w: pl.BlockSpec((None, tm, w), lambda bi, i, j: (bi, i, 0))
    wgt = lambda w: pl.BlockSpec((w, tn), lambda bi, i, j: (0, j))
    return pl.pallas_call(
        _outproj_kernel,
        grid=(b, s // tm, d // tn),
        in_specs=[act(cw), act(fw), act(lw), pl.BlockSpec((1, fw), lambda bi, i, j: (0, 0)),
                  wgt(cw), wgt(fw), wgt(lw),
                  pl.BlockSpec((None, tm, tn), lambda bi, i, j: (bi, i, j)),
                  pl.BlockSpec((None, 1, tn), lambda bi, i, j: (bi, 0, j))],
        out_specs=pl.BlockSpec((None, tm, tn), lambda bi, i, j: (bi, i, j)),
        out_shape=jax.ShapeDtypeStruct((b, s, d), F32),
        scratch_shapes=[pltpu.VMEM((tm, fw), BF16)],
        compiler_params=_params("parallel", "parallel", "arbitrary"),
        name="outproj",
    )(y_conv, o_fox, y_lru, fox_g.reshape(1, fw), w1, w2, w3, x, g1)


def _router_kernel(x_ref, g_ref, sc_ref, sh_ref, whi_ref, wlo_ref, rb_ref, h2_ref, ri_ref, rg_ref, *, n_groups, per_group):
    h2 = _modulated_norm(x_ref[...], g_ref[...], sc_ref[...], sh_ref[...])
    h2_ref[...] = h2
    hi = h2.astype(BF16)
    lo = (h2 - hi.astype(F32)).astype(BF16)
    logits = (jnp.dot(hi, whi_ref[...], preferred_element_type=F32)
              + jnp.dot(lo, whi_ref[...], preferred_element_type=F32)
              + jnp.dot(hi, wlo_ref[...], preferred_element_type=F32)) + rb_ref[...]
    lane_i = lax.broadcasted_iota(jnp.int32, logits.shape, 1)
    lane = lane_i.astype(F32)
    big = float(LANES)

    def first_lane(hit):
        return jnp.min(jnp.where(hit, lane, big), axis=-1, keepdims=True)

    gmask = lane_i < n_groups
    lg1 = jnp.where(gmask, logits, -jnp.inf)
    m1 = jnp.max(lg1, axis=-1, keepdims=True)
    e1 = jnp.exp(lg1 - m1)
    p1 = e1 / jnp.sum(e1, axis=-1, keepdims=True)
    p_grp = jnp.max(p1, axis=-1, keepdims=True)
    grp = first_lane((p1 == p_grp) & gmask)
    lo_lane = n_groups + grp * per_group
    emask = (lane >= lo_lane) & (lane < lo_lane + per_group)
    lg2 = jnp.where(emask, logits, -jnp.inf)
    m2 = jnp.max(lg2, axis=-1, keepdims=True)
    e2 = jnp.exp(lg2 - m2)
    p2 = e2 / jnp.sum(e2, axis=-1, keepdims=True)
    v1 = jnp.max(jnp.where(emask, p2, -1.0), axis=-1, keepdims=True)
    i1 = first_lane((p2 == v1) & emask)
    rest = emask & (lane != i1)
    v2 = jnp.max(jnp.where(rest, p2, -1.0), axis=-1, keepdims=True)
    i2 = first_lane((p2 == v2) & rest)
    denom = v1 + v2
    gate1 = p_grp * (v1 / denom)
    gate2 = p_grp * (v2 / denom)
    ri_ref[...] = jnp.where(lane_i == 0, i1 - n_groups, jnp.where(lane_i == 1, i2 - n_groups, 0.0)).astype(jnp.int32)
    rg_ref[...] = jnp.where(lane_i == 0, gate1, jnp.where(lane_i == 1, gate2, 0.0))


def _router(x, g, sc, sh, w_hi, w_lo, r_bias, n_groups, per_group):
    b, s, d = x.shape
    tm = _pick(s, 512)
    kern = functools.partial(_router_kernel, n_groups=n_groups, per_group=per_group)
    row = lambda w: pl.BlockSpec((None, tm, w), lambda bi, i: (bi, i, 0))
    mod = pl.BlockSpec((None, 1, d), lambda bi, i: (bi, 0, 0))
    wsp = pl.BlockSpec((d, LANES), lambda bi, i: (0, 0))
    return pl.pallas_call(
        kern,
        grid=(b, s // tm),
        in_specs=[row(d), pl.BlockSpec((1, d), lambda bi, i: (0, 0)), mod, mod, wsp, wsp,
                  pl.BlockSpec((1, LANES), lambda bi, i: (0, 0))],
        out_specs=[row(d), row(LANES), row(LANES)],
        out_shape=[jax.ShapeDtypeStruct((b, s, d), F32), jax.ShapeDtypeStruct((b, s, LANES), jnp.int32),
                   jax.ShapeDtypeStruct((b, s, LANES), F32)],
        compiler_params=_params("parallel", "parallel"),
        name="router",
    )(x, g, sc, sh, w_hi, w_lo, r_bias)


def _moe_kernel(bexp_ref, rtok_ref, h2_hbm, wg_ref, wu_ref, wd_ref, y_ref, xbuf, sem, wg_bf, wu_bf, wd_bf, *, rows):
    b = pl.program_id(0)
    nb = pl.num_programs(0)
    slot = b % 2

    def gather(blk, slot_):
        base = blk * rows

        def body(r, carry):
            tok = rtok_ref[base + r]
            pltpu.make_async_copy(h2_hbm.at[pl.ds(tok, 1)], xbuf.at[slot_, pl.ds(r, 1)], sem.at[slot_]).start()
            return carry

        lax.fori_loop(0, rows, body, 0)

    @pl.when(b == 0)
    def _():
        gather(0, 0)

    @pl.when(b + 1 < nb)
    def _():
        gather(b + 1, 1 - slot)

    changed = jnp.logical_or(b == 0, bexp_ref[b] != bexp_ref[jnp.maximum(b - 1, 0)])

    @pl.when(changed)
    def _():
        wg_bf[...] = wg_ref[...].astype(BF16)
        wu_bf[...] = wu_ref[...].astype(BF16)
        wd_bf[...] = wd_ref[...].astype(BF16)

    pltpu.make_async_copy(h2_hbm.at[pl.ds(0, rows)], xbuf.at[slot], sem.at[slot]).wait()
    xb = xbuf[slot].astype(BF16)
    gt = jnp.dot(xb, wg_bf[...], preferred_element_type=F32)
    up = jnp.dot(xb, wu_bf[...], preferred_element_type=F32)
    hb = (gt * _sigmoid(gt) * up).astype(BF16)
    y_ref[...] = jnp.dot(hb, wd_bf[...], preferred_element_type=F32)


def _moe_experts(block_exp, row_tok, h2_flat, w_gate, w_up, w_down):
    _, d, f = w_gate.shape
    rows = MOE_ROWS
    n_blocks = block_exp.shape[0]
    grid_spec = pltpu.PrefetchScalarGridSpec(
        num_scalar_prefetch=2,
        grid=(n_blocks,),
        in_specs=[
            pl.BlockSpec(memory_space=pl.ANY),
            pl.BlockSpec((None, d, f), lambda bi, be, rt: (be[bi], 0, 0)),
            pl.BlockSpec((None, d, f), lambda bi, be, rt: (be[bi], 0, 0)),
            pl.BlockSpec((None, f, d), lambda bi, be, rt: (be[bi], 0, 0)),
        ],
        out_specs=pl.BlockSpec((rows, d), lambda bi, be, rt: (bi, 0)),
        scratch_shapes=[pltpu.VMEM((2, rows, d), F32), pltpu.SemaphoreType.DMA((2,)),
                        pltpu.VMEM((d, f), BF16), pltpu.VMEM((d, f), BF16), pltpu.VMEM((f, d), BF16)],
    )
    return pl.pallas_call(
        functools.partial(_moe_kernel, rows=rows),
        grid_spec=grid_spec,
        out_shape=jax.ShapeDtypeStruct((n_blocks * rows, d), F32),
        compiler_params=_params("arbitrary"),
        name="moe_experts",
    )(block_exp, row_tok, h2_flat, w_gate, w_up, w_down)


def _combine_kernel(dest_ref, y_hbm, x_ref, rg_ref, g2_ref, out_ref, ybuf, sem, *, tm):
    i = pl.program_id(0)
    n = pl.num_programs(0)
    slot = i % 2

    def gather(blk, slot_):
        base = blk * tm * TOP_K

        def body(r, carry):
            for k in range(TOP_K):
                row = dest_ref[base + r * TOP_K + k]
                pltpu.make_async_copy(y_hbm.at[pl.ds(row, 1)], ybuf.at[slot_, k, pl.ds(r, 1)], sem.at[slot_]).start()
            return carry

        lax.fori_loop(0, tm, body, 0)

    @pl.when(i == 0)
    def _():
        gather(0, 0)

    @pl.when(i + 1 < n)
    def _():
        gather(i + 1, 1 - slot)

    for k in range(TOP_K):
        pltpu.make_async_copy(y_hbm.at[pl.ds(0, tm)], ybuf.at[slot, k], sem.at[slot]).wait()
    gates = rg_ref[...]
    moe = ybuf[slot, 0] * gates[:, 0:1]
    for k in range(1, TOP_K):
        moe = moe + ybuf[slot, k] * gates[:, k:k + 1]
    out_ref[...] = x_ref[...] + g2_ref[...] * moe


def _combine(dest, y, x_flat, rg_flat, g2, seq):
    t, d = x_flat.shape
    tm = _pick(seq, 128)
    grid_spec = pltpu.PrefetchScalarGridSpec(
        num_scalar_prefetch=1,
        grid=(t // tm,),
        in_specs=[
            pl.BlockSpec(memory_space=pl.ANY),
            pl.BlockSpec((tm, d), lambda i, de: (i, 0)),
            pl.BlockSpec((tm, LANES), lambda i, de: (i, 0)),
            pl.BlockSpec((None, 1, d), lambda i, de: ((i * tm) // seq, 0, 0)),
        ],
        out_specs=pl.BlockSpec((tm, d), lambda i, de: (i, 0)),
        scratch_shapes=[pltpu.VMEM((2, TOP_K, tm, d), F32), pltpu.SemaphoreType.DMA((2,))],
    )
    return pl.pallas_call(
        functools.partial(_combine_kernel, tm=tm),
        grid_spec=grid_spec,
        out_shape=jax.ShapeDtypeStruct((t, d), F32),
        compiler_params=_params("arbitrary"),
        name="moe_combine",
    )(dest, y, x_flat, rg_flat, g2)


def _final_norm_kernel(x_ref, g_ref, o_ref):
    x = x_ref[...]
    o_ref[...] = x * lax.rsqrt(jnp.mean(x * x, axis=-1, keepdims=True) + EPS) * g_ref[...]


def _final_norm(x_flat, g):
    t, d = x_flat.shape
    tm = _pick(t, 512)
    return pl.pallas_call(
        _final_norm_kernel,
        grid=(t // tm,),
        in_specs=[pl.BlockSpec((tm, d), lambda i: (i, 0)), pl.BlockSpec((1, d), lambda i: (0, 0))],
        out_specs=pl.BlockSpec((tm, d), lambda i: (i, 0)),
        out_shape=jax.ShapeDtypeStruct((t, d), F32),
        compiler_params=_params("parallel"),
        name="final_norm",
    )(x_flat, g.reshape(1, d))


def _routing_tables(eid, n_experts, rows):
    n_assign = eid.shape[0]
    onehot = (eid[:, None] == jnp.arange(n_experts, dtype=jnp.int32)[None, :]).astype(jnp.int32)
    csum = jnp.cumsum(onehot, axis=0)
    rank = jnp.sum(csum * onehot, axis=1) - 1
    counts = csum[-1]
    padded = (counts + rows - 1) // rows * rows
    pends = jnp.cumsum(padded)
    pstarts = pends - padded
    dest = (jnp.sum(onehot * pstarts[None, :], axis=1) + rank).astype(jnp.int32)
    n_blocks = -(-n_assign // rows) + n_experts
    row_tok = jnp.zeros((n_blocks * rows,), jnp.int32).at[dest].set(jnp.arange(n_assign, dtype=jnp.int32) // TOP_K)
    block_exp = jnp.minimum(
        jnp.searchsorted(pends, jnp.arange(n_blocks, dtype=jnp.int32) * rows, side='right'), n_experts - 1)
    return block_exp.astype(jnp.int32), row_tok, dest


def _block_diag(w):
    nb, bw, _ = w.shape
    eye = jnp.eye(nb, dtype=w.dtype)
    return (eye[:, None, :, None] * w[:, :, None, :]).reshape(nb * bw, nb * bw)


def kernel(x, c, w_ada, b_ada, ln1_g, w_in, conv_dw_w, conv_dw_b, conv_ln_g, conv_ln_b, fox_f_bias, fox_out_g, lru_conv_w, lru_conv_b, lru_w_r, lru_b_r, lru_w_i, lru_b_i, lru_lambda, lru_out_g, w_out, ln2_g, w_router_group, b_router_group, w_router_expert, b_router_expert, w_gate, w_up, w_down, final_g):
    b, s, d = x.shape
    depth = w_ada.shape[0]
    cw = conv_dw_b.shape[-1]
    fw = fox_out_g.shape[-1]
    lw = lru_lambda.shape[-1]
    heads = fox_f_bias.shape[-1]
    hd = fw // heads
    n_groups, per_group = b_router_expert.shape[1:]
    n_experts = n_groups * per_group
    off_f = 2 * cw + 3 * fw
    assert cw % LANES == 0 and lw == cw and fw % cw == 0 and hd == LANES and heads <= LANES
    assert n_groups + n_experts <= LANES

    c_pad = jnp.zeros((8, d), F32).at[:b].set(c)
    mod_all = _ada_mod(c_pad, w_ada, b_ada)

    tq = _pick(s, 512)
    for l in range(depth):
        mod = mod_all[l, :b].reshape(b, 1, 6 * d)
        sh1, sc1, g1, sh2, sc2, g2 = [mod[..., k * d:(k + 1) * d] for k in range(6)]

        w_l = w_in[l]
        w_main = jnp.concatenate([w_l[:, :off_f], w_l[:, off_f + heads:]], axis=1).astype(BF16)
        w_f = jnp.zeros((d, LANES), BF16).at[:, :heads].set(w_l[:, off_f:off_f + heads].astype(BF16))
        z, zf = _inproj(x, ln1_g[l].reshape(1, d), sc1, sh1, w_main, w_f)

        y_conv = _conv_branch(z, cw, conv_dw_w[l], conv_dw_b[l], conv_ln_g[l], conv_ln_b[l])

        f_bias = jnp.zeros((1, LANES), F32).at[0, :heads].set(fox_f_bias[l])
        cum = _forget_cumsum(zf, f_bias)
        cum_t = cum[:, :, :heads].transpose(0, 2, 1).reshape(b, heads, s // tq, 1, tq)
        o_fox = _attention(z, cum, cum_t, heads, hd, 2 * cw // hd, (2 * cw + fw) // hd, (2 * cw + 2 * fw) // hd)

        y_lru = _lru_branch(z, lw, off_f // lw, lru_conv_w[l], lru_conv_b[l],
                            _block_diag(lru_w_r[l]).astype(BF16), lru_b_r[l],
                            _block_diag(lru_w_i[l]).astype(BF16), lru_b_i[l], lru_lambda[l], lru_out_g[l])

        wo = w_out[l].astype(BF16)
        x = _outproj(y_conv, o_fox, y_lru, fox_out_g[l], wo[:cw], wo[cw:cw + fw], wo[cw + fw:], x, g1)

        w_r = jnp.concatenate(
            [w_router_group[l], w_router_expert[l].transpose(1, 0, 2).reshape(d, n_experts)], axis=1)
        w_r = jnp.zeros((d, LANES), F32).at[:, :n_groups + n_experts].set(w_r)
        w_hi = w_r.astype(BF16)
        w_lo = (w_r - w_hi.astype(F32)).astype(BF16)
        r_bias = jnp.zeros((1, LANES), F32).at[0, :n_groups + n_experts].set(
            jnp.concatenate([b_router_group[l], b_router_expert[l].reshape(-1)]))
        h2, ri, rg = _router(x, ln2_g[l].reshape(1, d), sc2, sh2, w_hi, w_lo, r_bias, n_groups, per_group)

        eid = ri[:, :, :TOP_K].reshape(-1)
        block_exp, row_tok, dest = _routing_tables(eid, n_experts, MOE_ROWS)
        y = _moe_experts(block_exp, row_tok, h2.reshape(b * s, d), w_gate[l], w_up[l], w_down[l])
        x = _combine(dest, y, x.reshape(b * s, d), rg.reshape(b * s, LANES), g2, s).reshape(b, s, d)

    return _final_norm(x.reshape(b * s, d), final_g).reshape(b, s, d)
```

```python
import functools

import jax
import jax.numpy as jnp
from jax import lax
from jax.experimental import pallas as pl
from jax.experimental.pallas import tpu as pltpu

EPS = 1e-6
LRU_C = 8.0
TOP_K = 2
LANES = 128
MOE_ROWS = 128
VMEM_LIMIT = 56 * 1024 * 1024

F32 = jnp.float32
BF16 = jnp.bfloat16


def _pick(n, pref):
    if n <= pref:
        return n
    for t in range(pref, 7, -1):
        if n % t == 0 and t % 8 == 0:
            return t
    raise ValueError((n, pref))


def _params(*sem):
    return pltpu.CompilerParams(dimension_semantics=sem, vmem_limit_bytes=VMEM_LIMIT)


def _sigmoid(x):
    return jax.nn.sigmoid(x)


def _ada_kernel(c_ref, w_ref, b_ref, o_ref):
    c = c_ref[...]
    cond = (c * _sigmoid(c)).astype(BF16)
    o_ref[...] = jnp.dot(cond, w_ref[...].astype(BF16), preferred_element_type=F32) + b_ref[...]


def _ada_mod(c_pad, w_ada, b_ada):
    depth, d, n = w_ada.shape
    rows = c_pad.shape[0]
    tn = _pick(n, 1024)
    return pl.pallas_call(
        _ada_kernel,
        grid=(depth, n // tn),
        in_specs=[
            pl.BlockSpec((rows, d), lambda l, j: (0, 0)),
            pl.BlockSpec((None, d, tn), lambda l, j: (l, 0, j)),
            pl.BlockSpec((None, 1, tn), lambda l, j: (l, 0, j)),
        ],
        out_specs=pl.BlockSpec((None, rows, tn), lambda l, j: (l, 0, j)),
        out_shape=jax.ShapeDtypeStruct((depth, rows, n), F32),
        compiler_params=_params("parallel", "parallel"),
        name="ada_mod",
    )(c_pad, w_ada, b_ada.reshape(depth, 1, n))


def _modulated_norm(x, g, sc, sh):
    y = x * lax.rsqrt(jnp.mean(x * x, axis=-1, keepdims=True) + EPS) * g
    return y * (1.0 + sc) + sh


def _inproj_kernel(x_ref, g_ref, sc_ref, sh_ref, w_ref, wf_ref, z_ref, zf_ref, h_scr):
    @pl.when(pl.program_id(2) == 0)
    def _():
        hb = _modulated_norm(x_ref[...], g_ref[...], sc_ref[...], sh_ref[...]).astype(BF16)
        h_scr[...] = hb
        zf_ref[...] = jnp.dot(hb, wf_ref[...], preferred_element_type=F32)

    z_ref[...] = jnp.dot(h_scr[...], w_ref[...], preferred_element_type=F32)


def _inproj(x, g, sc, sh, w_main, w_f):
    b, s, d = x.shape
    n = w_main.shape[1]
    tm = _pick(s, 1024)
    tn = _pick(n, 1024)
    return pl.pallas_call(
        _inproj_kernel,
        grid=(b, s // tm, n // tn),
        in_specs=[
            pl.BlockSpec((None, tm, d), lambda bi, i, j: (bi, i, 0)),
            pl.BlockSpec((1, d), lambda bi, i, j: (0, 0)),
            pl.BlockSpec((None, 1, d), lambda bi, i, j: (bi, 0, 0)),
            pl.BlockSpec((None, 1, d), lambda bi, i, j: (bi, 0, 0)),
            pl.BlockSpec((d, tn), lambda bi, i, j: (0, j)),
            pl.BlockSpec((d, LANES), lambda bi, i, j: (0, 0)),
        ],
        out_specs=[
            pl.BlockSpec((None, tm, tn), lambda bi, i, j: (bi, i, j)),
            pl.BlockSpec((None, tm, LANES), lambda bi, i, j: (bi, i, 0)),
        ],
        out_shape=[jax.ShapeDtypeStruct((b, s, n), F32), jax.ShapeDtypeStruct((b, s, LANES), F32)],
        scratch_shapes=[pltpu.VMEM((tm, d), BF16)],
        compiler_params=_params("parallel", "parallel", "arbitrary"),
        name="inproj",
    )(x, g, sc, sh, w_main, w_f)


def _conv_kernel(a_ref, g_ref, ah_ref, gh_ref, w_ref, b_ref, lg_ref, lb_ref, o_ref, buf, ybuf, *, taps, ts, halo, rc):
    cw = a_ref.shape[-1]
    buf[halo:halo + ts, :] = a_ref[...] * _sigmoid(g_ref[...])
    prev = ah_ref[...] * _sigmoid(gh_ref[...])
    buf[0:halo, :] = jnp.where(pl.program_id(1) > 0, prev, 0.0)
    for c in range(cw // LANES):
        cs = slice(c * LANES, (c + 1) * LANES)
        accs = [None] * (ts // rc)
        for k in range(taps):
            wk = w_ref[k:k + 1, cs]
            off = halo - (taps - 1) + k
            for r in range(ts // rc):
                term = wk * buf[off + r * rc:off + (r + 1) * rc, cs]
                accs[r] = term if accs[r] is None else accs[r] + term
        for r in range(ts // rc):
            ybuf[r * rc:(r + 1) * rc, cs] = accs[r] + b_ref[:, cs]
    y = ybuf[...]
    mu = jnp.mean(y, axis=-1, keepdims=True)
    yc = y - mu
    var = jnp.mean(yc * yc, axis=-1, keepdims=True)
    yn = yc * lax.rsqrt(var + EPS) * lg_ref[...] + lb_ref[...]
    o_ref[...] = (yn * _sigmoid(yn)).astype(o_ref.dtype)


def _conv_branch(z, cw, dw_w, dw_b, ln_g, ln_b):
    b, s, _ = z.shape
    taps = dw_w.shape[0]
    halo = 32
    assert taps - 1 <= halo
    ts = _pick(s, 256)
    rc = _pick(ts, 64)
    hb = ts // halo
    kern = functools.partial(_conv_kernel, taps=taps, ts=ts, halo=halo, rc=rc)
    cur = lambda col: pl.BlockSpec((None, ts, cw), lambda bi, i: (bi, i, col))
    prev = lambda col: pl.BlockSpec((None, halo, cw), lambda bi, i: (bi, jnp.maximum(i * hb - 1, 0), col))
    vec = pl.BlockSpec((1, cw), lambda bi, i: (0, 0))
    return pl.pallas_call(
        kern,
        grid=(b, s // ts),
        in_specs=[cur(0), cur(1), prev(0), prev(1),
                  pl.BlockSpec((taps, cw), lambda bi, i: (0, 0)), vec, vec, vec],
        out_specs=pl.BlockSpec((None, ts, cw), lambda bi, i: (bi, i, 0)),
        out_shape=jax.ShapeDtypeStruct((b, s, cw), BF16),
        scratch_shapes=[pltpu.VMEM((halo + ts, cw), F32), pltpu.VMEM((ts, cw), F32)],
        compiler_params=_params("parallel", "parallel"),
        name="conv_branch",
    )(z, z, z, z, dw_w, dw_b.reshape(1, cw), ln_g.reshape(1, cw), ln_b.reshape(1, cw))


def _shift_rows(v, d, fill):
    rows = lax.broadcasted_iota(jnp.int32, v.shape, 0)
    return jnp.where(rows >= d, pltpu.roll(v, d, axis=0), fill)


def _scan_linear(a, bv):
    d = 1
    while d < a.shape[0]:
        bv = a * _shift_rows(bv, d, 0.0) + bv
        a = a * _shift_rows(a, d, 1.0)
        d *= 2
    return a, bv


def _scan_sum(v):
    d = 1
    while d < v.shape[0]:
        v = v + _shift_rows(v, d, 0.0)
        d *= 2
    return v


def _cum_kernel(zf_ref, fb_ref, o_ref, car, *, ts, rs):
    @pl.when(pl.program_id(1) == 0)
    def _():
        car[...] = jnp.zeros_like(car)

    h = car[...]
    for r in range(ts // rs):
        x = zf_ref[r * rs:(r + 1) * rs, :] + fb_ref[...]
        log_f = jnp.minimum(x, 0.0) - jnp.log1p(jnp.exp(-jnp.abs(x)))
        v = _scan_sum(log_f) + h
        o_ref[r * rs:(r + 1) * rs, :] = v
        h = v[rs - 1:rs, :]
    car[...] = h


def _forget_cumsum(zf, f_bias_pad):
    b, s, _ = zf.shape
    ts = _pick(s, 512)
    rs = _pick(ts, 128)
    return pl.pallas_call(
        functools.partial(_cum_kernel, ts=ts, rs=rs),
        grid=(b, s // ts),
        in_specs=[pl.BlockSpec((None, ts, LANES), lambda bi, i: (bi, i, 0)),
                  pl.BlockSpec((1, LANES), lambda bi, i: (0, 0))],
        out_specs=pl.BlockSpec((None, ts, LANES), lambda bi, i: (bi, i, 0)),
        out_shape=jax.ShapeDtypeStruct((b, s, LANES), F32),
        scratch_shapes=[pltpu.VMEM((1, LANES), F32)],
        compiler_params=_params("parallel", "arbitrary"),
        name="forget_cumsum",
    )(zf, f_bias_pad)


def _gelu_tanh(x):
    return 0.5 * x * (1.0 + jnp.tanh(0.7978845608028654 * (x + 0.044715 * (x * x * x))))


def _lru_kernel(u_ref, uh_ref, gate_ref, cw_ref, cb_ref, wr_ref, br_ref, wi_ref, bi_ref, lam_ref, og_ref,
                o_ref, ubuf, abuf, bbuf, car, *, taps, ts, halo, rs):
    lw = u_ref.shape[-1]

    @pl.when(pl.program_id(1) == 0)
    def _():
        car[...] = jnp.zeros_like(car)

    ubuf[halo:halo + ts, :] = u_ref[...]
    ubuf[0:halo, :] = jnp.where(pl.program_id(1) > 0, uh_ref[...], 0.0)
    xc = cb_ref[...]
    for k in range(taps):
        off = halo - (taps - 1) + k
        xc = xc + cw_ref[k:k + 1, :] * ubuf[off:off + ts, :]
    xb = xc.astype(BF16)
    r = _sigmoid(jnp.dot(xb, wr_ref[...], preferred_element_type=F32) + br_ref[...])
    ig = _sigmoid(jnp.dot(xb, wi_ref[...], preferred_element_type=F32) + bi_ref[...])
    nl = -lam_ref[...]
    softplus = jnp.maximum(nl, 0.0) + jnp.log1p(jnp.exp(-jnp.abs(nl)))
    log_a = (-LRU_C) * r * softplus
    th = jnp.tanh(log_a)
    abuf[...] = jnp.exp(log_a)
    bbuf[...] = jnp.sqrt(-2.0 * th / (1.0 - th)) * (ig * xc)
    for c in range(lw // LANES):
        cs = slice(c * LANES, (c + 1) * LANES)
        h = car[:, cs]
        for q in range(ts // rs):
            rows = slice(q * rs, (q + 1) * rs)
            a_cum, b_cum = _scan_linear(abuf[rows, cs], bbuf[rows, cs])
            hv = b_cum + a_cum * h
            bbuf[rows, cs] = hv
            h = hv[rs - 1:rs, :]
        car[:, cs] = h
    y = bbuf[...] * _gelu_tanh(gate_ref[...])
    y = y * lax.rsqrt(jnp.mean(y * y, axis=-1, keepdims=True) + EPS) * og_ref[...]
    o_ref[...] = y.astype(o_ref.dtype)


def _lru_branch(z, lw, col_x, conv_w, conv_b, wr_bd, b_r, wi_bd, b_i, lam, out_g):
    b, s, _ = z.shape
    taps = conv_w.shape[0]
    halo = 8
    assert taps - 1 <= halo
    ts = _pick(s, 256)
    rs = _pick(ts, 128)
    hb = ts // halo
    kern = functools.partial(_lru_kernel, taps=taps, ts=ts, halo=halo, rs=rs)
    vec = pl.BlockSpec((1, lw), lambda bi, i: (0, 0))
    mat = pl.BlockSpec((lw, lw), lambda bi, i: (0, 0))
    return pl.pallas_call(
        kern,
        grid=(b, s // ts),
        in_specs=[
            pl.BlockSpec((None, ts, lw), lambda bi, i: (bi, i, col_x)),
            pl.BlockSpec((None, halo, lw), lambda bi, i: (bi, jnp.maximum(i * hb - 1, 0), col_x)),
            pl.BlockSpec((None, ts, lw), lambda bi, i: (bi, i, col_x + 1)),
            pl.BlockSpec((taps, lw), lambda bi, i: (0, 0)), vec, mat, vec, mat, vec, vec, vec,
        ],
        out_specs=pl.BlockSpec((None, ts, lw), lambda bi, i: (bi, i, 0)),
        out_shape=jax.ShapeDtypeStruct((b, s, lw), BF16),
        scratch_shapes=[pltpu.VMEM((halo + ts, lw), F32), pltpu.VMEM((ts, lw), F32),
                        pltpu.VMEM((ts, lw), F32), pltpu.VMEM((1, lw), F32)],
        compiler_params=_params("parallel", "arbitrary"),
        name="lru_branch",
    )(z, z, z, conv_w, conv_b.reshape(1, lw), wr_bd, b_r.reshape(1, lw), wi_bd, b_i.reshape(1, lw),
      lam.reshape(1, lw), out_g.reshape(1, lw))


def _attn_kernel(q_ref, k_ref, v_ref, cum_ref, cumt_ref, o_ref, *, tq, scale):
    h = pl.program_id(1)
    i = pl.program_id(2)
    q = q_ref[...].astype(BF16)
    lane = lax.broadcasted_iota(jnp.int32, cum_ref.shape, 1)
    cq = jnp.sum(jnp.where(lane == h, cum_ref[...], 0.0), axis=-1, keepdims=True)

    def step(j, carry, masked):
        m, l, acc = carry
        start = pl.multiple_of(j * tq, tq)
        ks = k_ref[pl.ds(start, tq), :].astype(BF16)
        vs = v_ref[pl.ds(start, tq), :].astype(BF16)
        s = lax.dot_general(q, ks, (((1,), (1,)), ((), ())), preferred_element_type=F32) * scale
        s = s + cq - cumt_ref[j]
        if masked:
            row = lax.broadcasted_iota(jnp.int32, s.shape, 0)
            col = lax.broadcasted_iota(jnp.int32, s.shape, 1)
            s = jnp.where(col <= row, s, -jnp.inf)
        m_new = jnp.maximum(m, jnp.max(s, axis=-1, keepdims=True))
        alpha = jnp.exp(m - m_new)
        p = jnp.exp(s - m_new)
        l = alpha * l + jnp.sum(p, axis=-1, keepdims=True)
        acc = alpha * acc + jnp.dot(p.astype(BF16), vs, preferred_element_type=F32)
        return m_new, l, acc

    init = (jnp.full((tq, 1), -jnp.inf, F32), jnp.zeros((tq, 1), F32), jnp.zeros(o_ref.shape, F32))
    carry = lax.fori_loop(0, i, lambda j, c: step(j, c, False), init)
    _, l, acc = step(i, carry, True)
    o_ref[...] = acc / l


def _attention(z, cum, cum_t, heads, hd, col_q, col_k, col_v):
    b, s, _ = z.shape
    tq = cum_t.shape[-1]
    kern = functools.partial(_attn_kernel, tq=tq, scale=hd ** -0.5)
    return pl.pallas_call(
        kern,
        grid=(b, heads, s // tq),
        in_specs=[
            pl.BlockSpec((None, tq, hd), lambda bi, h, i: (bi, i, col_q + h)),
            pl.BlockSpec((None, s, hd), lambda bi, h, i: (bi, 0, col_k + h)),
            pl.BlockSpec((None, s, hd), lambda bi, h, i: (bi, 0, col_v + h)),
            pl.BlockSpec((None, tq, LANES), lambda bi, h, i: (bi, i, 0)),
            pl.BlockSpec((None, None, s // tq, 1, tq), lambda bi, h, i: (bi, h, 0, 0, 0)),
        ],
        out_specs=pl.BlockSpec((None, tq, hd), lambda bi, h, i: (bi, i, h)),
        out_shape=jax.ShapeDtypeStruct((b, s, heads * hd), F32),
        compiler_params=_params("parallel", "parallel", "parallel"),
        name="fox_attention",
    )(z, z, z, cum, cum_t)


def _outproj_kernel(yc_ref, o_ref, yl_ref, fg_ref, w1_ref, w2_ref, w3_ref, x_ref, g1_ref, out_ref, yf_scr):
    @pl.when(pl.program_id(2) == 0)
    def _():
        o = o_ref[...]
        yf = o * lax.rsqrt(jnp.mean(o * o, axis=-1, keepdims=True) + EPS) * fg_ref[...]
        yf_scr[...] = yf.astype(BF16)

    y = jnp.dot(yc_ref[...], w1_ref[...], preferred_element_type=F32)
    y = y + jnp.dot(yf_scr[...], w2_ref[...], preferred_element_type=F32)
    y = y + jnp.dot(yl_ref[...], w3_ref[...], preferred_element_type=F32)
    out_ref[...] = x_ref[...] + g1_ref[...] * y


def _outproj(y_conv, o_fox, y_lru, fox_g, w1, w2, w3, x, g1):
    b, s, d = x.shape
    cw, fw, lw = y_conv.shape[-1], o_fox.shape[-1], y_lru.shape[-1]
    tm = _pick(s, 512)
    tn = _pick(d, 1024)
    act = lambda w: pl.BlockSpec((None, tm, w), lambda bi, i, j: (bi, i, 0))
    wgt = lambda w: pl.BlockSpec((w, tn), lambda bi, i, j: (0, j))
    return pl.pallas_call(
        _outproj_kernel,
        grid=(b, s // tm, d // tn),
        in_specs=[act(cw), act(fw), act(lw), pl.BlockSpec((1, fw), lambda bi, i, j: (0, 0)),
                  wgt(cw), wgt(fw), wgt(lw),
                  pl.BlockSpec((None, tm, tn), lambda bi, i, j: (bi, i, j)),
                  pl.BlockSpec((None, 1, tn), lambda bi, i, j: (bi, 0, j))],
        out_specs=pl.BlockSpec((None, tm, tn), lambda bi, i, j: (bi, i, j)),
        out_shape=jax.ShapeDtypeStruct((b, s, d), F32),
        scratch_shapes=[pltpu.VMEM((tm, fw), BF16)],
        compiler_params=_params("parallel", "parallel", "arbitrary"),
        name="outproj",
    )(y_conv, o_fox, y_lru, fox_g.reshape(1, fw), w1, w2, w3, x, g1)


def _router_kernel(x_ref, g_ref, sc_ref, sh_ref, whi_ref, wlo_ref, rb_ref, h2_ref, ri_ref, rg_ref, cnt_ref, car,
                   *, n_groups, per_group):
    @pl.when(jnp.logical_and(pl.program_id(0) == 0, pl.program_id(1) == 0))
    def _():
        car[...] = jnp.zeros_like(car)

    h2 = _modulated_norm(x_ref[...], g_ref[...], sc_ref[...], sh_ref[...])
    h2_ref[...] = h2
    hi = h2.astype(BF16)
    lo = (h2 - hi.astype(F32)).astype(BF16)
    logits = (jnp.dot(hi, whi_ref[...], preferred_element_type=F32)
              + jnp.dot(lo, whi_ref[...], preferred_element_type=F32)
              + jnp.dot(hi, wlo_ref[...], preferred_element_type=F32)) + rb_ref[...]
    lane_i = lax.broadcasted_iota(jnp.int32, logits.shape, 1)
    lane = lane_i.astype(F32)
    big = float(LANES)

    def first_lane(hit):
        return jnp.min(jnp.where(hit, lane, big), axis=-1, keepdims=True)

    gmask = lane_i < n_groups
    lg1 = jnp.where(gmask, logits, -jnp.inf)
    m1 = jnp.max(lg1, axis=-1, keepdims=True)
    e1 = jnp.exp(lg1 - m1)
    p1 = e1 / jnp.sum(e1, axis=-1, keepdims=True)
    p_grp = jnp.max(p1, axis=-1, keepdims=True)
    grp = first_lane((p1 == p_grp) & gmask)
    lo_lane = n_groups + grp * per_group
    emask = (lane >= lo_lane) & (lane < lo_lane + per_group)
    lg2 = jnp.where(emask, logits, -jnp.inf)
    m2 = jnp.max(lg2, axis=-1, keepdims=True)
    e2 = jnp.exp(lg2 - m2)
    p2 = e2 / jnp.sum(e2, axis=-1, keepdims=True)
    v1 = jnp.max(jnp.where(emask, p2, -1.0), axis=-1, keepdims=True)
    i1 = first_lane((p2 == v1) & emask)
    rest = emask & (lane != i1)
    v2 = jnp.max(jnp.where(rest, p2, -1.0), axis=-1, keepdims=True)
    i2 = first_lane((p2 == v2) & rest)
    denom = v1 + v2
    gate1 = p_grp * (v1 / denom)
    gate2 = p_grp * (v2 / denom)
    tm = logits.shape[0]
    picked = jnp.where((lane == i1) | (lane == i2), 1.0, 0.0)
    tri = jnp.where(lax.broadcasted_iota(jnp.int32, (tm, tm), 0) > lax.broadcasted_iota(jnp.int32, (tm, tm), 1),
                    1.0, 0.0).astype(BF16)
    before = jnp.dot(tri, picked.astype(BF16), preferred_element_type=F32) + car[...]
    rank1 = jnp.sum(jnp.where(lane == i1, before, 0.0), axis=-1, keepdims=True)
    rank2 = jnp.sum(jnp.where(lane == i2, before, 0.0), axis=-1, keepdims=True)
    total = car[...] + jnp.sum(picked, axis=0, keepdims=True)
    car[...] = total
    cnt_ref[...] = total
    info = jnp.where(lane_i == 0, i1 - n_groups, jnp.where(lane_i == 1, i2 - n_groups,
                     jnp.where(lane_i == 2, rank1, jnp.where(lane_i == 3, rank2, 0.0))))
    ri_ref[...] = info.astype(jnp.int32)
    rg_ref[...] = jnp.where(lane_i == 0, gate1, jnp.where(lane_i == 1, gate2, 0.0))


def _router(x, g, sc, sh, w_hi, w_lo, r_bias, n_groups, per_group):
    b, s, d = x.shape
    tm = _pick(s, 512)
    kern = functools.partial(_router_kernel, n_groups=n_groups, per_group=per_group)
    row = lambda w: pl.BlockSpec((None, tm, w), lambda bi, i: (bi, i, 0))
    mod = pl.BlockSpec((None, 1, d), lambda bi, i: (bi, 0, 0))
    wsp = pl.BlockSpec((d, LANES), lambda bi, i: (0, 0))
    return pl.pallas_call(
        kern,
        grid=(b, s // tm),
        in_specs=[row(d), pl.BlockSpec((1, d), lambda bi, i: (0, 0)), mod, mod, wsp, wsp,
                  pl.BlockSpec((1, LANES), lambda bi, i: (0, 0))],
        out_specs=[row(d), row(LANES), row(LANES), pl.BlockSpec((1, LANES), lambda bi, i: (0, 0))],
        out_shape=[jax.ShapeDtypeStruct((b, s, d), F32), jax.ShapeDtypeStruct((b, s, LANES), jnp.int32),
                   jax.ShapeDtypeStruct((b, s, LANES), F32), jax.ShapeDtypeStruct((1, LANES), F32)],
        scratch_shapes=[pltpu.VMEM((1, LANES), F32)],
        compiler_params=_params("arbitrary", "arbitrary"),
        name="router",
    )(x, g, sc, sh, w_hi, w_lo, r_bias)


def _moe_kernel(bexp_ref, rtok_ref, h2_hbm, wg_ref, wu_ref, wd_ref, y_ref, xbuf, sem, wg_bf, wu_bf, wd_bf, *, rows):
    b = pl.program_id(0)
    nb = pl.num_programs(0)
    slot = b % 2

    def row_copy(tok, r, slot_):
        return pltpu.make_async_copy(h2_hbm.at[pl.ds(tok, 1)], xbuf.at[slot_, pl.ds(r, 1)], sem.at[slot_])

    def wait_block(slot_):
        pltpu.make_async_copy(h2_hbm.at[pl.ds(0, rows)], xbuf.at[slot_], sem.at[slot_]).wait()

    @pl.when(b == 0)
    def _():
        def body(r, carry):
            row_copy(rtok_ref[r], r, 0).start()
            return carry

        lax.fori_loop(0, rows, body, 0)

    changed = jnp.logical_or(b == 0, bexp_ref[b] != bexp_ref[jnp.maximum(b - 1, 0)])

    @pl.when(changed)
    def _():
        wg_bf[...] = wg_ref[...].astype(BF16)
        wu_bf[...] = wu_ref[...].astype(BF16)
        wd_bf[...] = wd_ref[...].astype(BF16)

    wait_block(slot)
    base = jnp.minimum(b + 1, nb - 1) * rows
    for r in range(rows):
        row_copy(rtok_ref[base + r], r, 1 - slot).start()
    xb = xbuf[slot].astype(BF16)
    gt = jnp.dot(xb, wg_bf[...], preferred_element_type=F32)
    up = jnp.dot(xb, wu_bf[...], preferred_element_type=F32)
    hb = (gt * _sigmoid(gt) * up).astype(BF16)
    y_ref[...] = jnp.dot(hb, wd_bf[...], preferred_element_type=F32)

    @pl.when(b == nb - 1)
    def _():
        wait_block(1 - slot)


def _moe_experts(block_exp, row_tok, h2_flat, w_gate, w_up, w_down, layer):
    _, _, d, f = w_gate.shape
    rows = MOE_ROWS
    n_blocks = block_exp.shape[0]
    grid_spec = pltpu.PrefetchScalarGridSpec(
        num_scalar_prefetch=2,
        grid=(n_blocks,),
        in_specs=[
            pl.BlockSpec(memory_space=pl.ANY),
            pl.BlockSpec((None, None, d, f), lambda bi, be, rt: (layer, be[bi], 0, 0)),
            pl.BlockSpec((None, None, d, f), lambda bi, be, rt: (layer, be[bi], 0, 0)),
            pl.BlockSpec((None, None, f, d), lambda bi, be, rt: (layer, be[bi], 0, 0)),
        ],
        out_specs=pl.BlockSpec((rows, d), lambda bi, be, rt: (bi, 0)),
        scratch_shapes=[pltpu.VMEM((2, rows, d), F32), pltpu.SemaphoreType.DMA((2,)),
                        pltpu.VMEM((d, f), BF16), pltpu.VMEM((d, f), BF16), pltpu.VMEM((f, d), BF16)],
    )
    return pl.pallas_call(
        functools.partial(_moe_kernel, rows=rows),
        grid_spec=grid_spec,
        out_shape=jax.ShapeDtypeStruct((n_blocks * rows, d), F32),
        compiler_params=_params("arbitrary"),
        name="moe_experts",
    )(block_exp, row_tok, h2_flat, w_gate, w_up, w_down)


def _combine_kernel(dest_ref, y_hbm, x_ref, rg_ref, g2_ref, out_ref, ybuf, sem, *, tm):
    i = pl.program_id(0)
    n = pl.num_programs(0)
    slot = i % 2

    def start(row, r, k, slot_):
        pltpu.make_async_copy(y_hbm.at[pl.ds(row, 1)], ybuf.at[slot_, k, pl.ds(r, 1)], sem.at[slot_]).start()

    def wait_tile(slot_):
        for k in range(TOP_K):
            pltpu.make_async_copy(y_hbm.at[pl.ds(0, tm)], ybuf.at[slot_, k], sem.at[slot_]).wait()

    @pl.when(i == 0)
    def _():
        def body(r, carry):
            for k in range(TOP_K):
                start(dest_ref[r * TOP_K + k], r, k, 0)
            return carry

        lax.fori_loop(0, tm, body, 0)

    wait_tile(slot)
    base = jnp.minimum(i + 1, n - 1) * (tm * TOP_K)
    for r in range(tm):
        for k in range(TOP_K):
            start(dest_ref[base + r * TOP_K + k], r, k, 1 - slot)
    gates = rg_ref[...]
    moe = ybuf[slot, 0] * gates[:, 0:1]
    for k in range(1, TOP_K):
        moe = moe + ybuf[slot, k] * gates[:, k:k + 1]
    out_ref[...] = x_ref[...] + g2_ref[...] * moe

    @pl.when(i == n - 1)
    def _():
        wait_tile(1 - slot)


def _combine(dest, y, x_flat, rg_flat, g2, seq):
    t, d = x_flat.shape
    tm = _pick(seq, 128)
    grid_spec = pltpu.PrefetchScalarGridSpec(
        num_scalar_prefetch=1,
        grid=(t // tm,),
        in_specs=[
            pl.BlockSpec(memory_space=pl.ANY),
            pl.BlockSpec((tm, d), lambda i, de: (i, 0)),
            pl.BlockSpec((tm, LANES), lambda i, de: (i, 0)),
            pl.BlockSpec((None, 1, d), lambda i, de: ((i * tm) // seq, 0, 0)),
        ],
        out_specs=pl.BlockSpec((tm, d), lambda i, de: (i, 0)),
        scratch_shapes=[pltpu.VMEM((2, TOP_K, tm, d), F32), pltpu.SemaphoreType.DMA((2,))],
    )
    return pl.pallas_call(
        functools.partial(_combine_kernel, tm=tm),
        grid_spec=grid_spec,
        out_shape=jax.ShapeDtypeStruct((t, d), F32),
        compiler_params=_params("arbitrary"),
        name="moe_combine",
    )(dest, y, x_flat, rg_flat, g2)


def _final_norm_kernel(x_ref, g_ref, o_ref):
    x = x_ref[...]
    o_ref[...] = x * lax.rsqrt(jnp.mean(x * x, axis=-1, keepdims=True) + EPS) * g_ref[...]


def _final_norm(x_flat, g):
    t, d = x_flat.shape
    tm = _pick(t, 512)
    return pl.pallas_call(
        _final_norm_kernel,
        grid=(t // tm,),
        in_specs=[pl.BlockSpec((tm, d), lambda i: (i, 0)), pl.BlockSpec((1, d), lambda i: (0, 0))],
        out_specs=pl.BlockSpec((tm, d), lambda i: (i, 0)),
        out_shape=jax.ShapeDtypeStruct((t, d), F32),
        compiler_params=_params("parallel"),
        name="final_norm",
    )(x_flat, g.reshape(1, d))


def _routing_tables(eid, rank, counts, rows):
    n_assign = eid.shape[0]
    n_experts = counts.shape[0]
    padded = (counts + rows - 1) // rows * rows
    pends = jnp.cumsum(padded)
    pstarts = pends - padded
    onehot = eid[:, None] == jnp.arange(n_experts, dtype=jnp.int32)[None, :]
    dest = (jnp.sum(jnp.where(onehot, pstarts[None, :], 0), axis=1) + rank).astype(jnp.int32)
    n_blocks = -(-n_assign // rows) + n_experts
    row_tok = jnp.zeros((n_blocks * rows,), jnp.int32).at[dest].set(jnp.arange(n_assign, dtype=jnp.int32) // TOP_K)
    first_row = jnp.arange(n_blocks, dtype=jnp.int32) * rows
    block_exp = jnp.minimum(jnp.sum((pends[None, :] <= first_row[:, None]).astype(jnp.int32), axis=1), n_experts - 1)
    return block_exp.astype(jnp.int32), row_tok, dest


def _block_diag(w):
    nb, bw, _ = w.shape
    eye = jnp.eye(nb, dtype=w.dtype)
    return (eye[:, None, :, None] * w[:, :, None, :]).reshape(nb * bw, nb * bw)


def kernel(x, c, w_ada, b_ada, ln1_g, w_in, conv_dw_w, conv_dw_b, conv_ln_g, conv_ln_b, fox_f_bias, fox_out_g, lru_conv_w, lru_conv_b, lru_w_r, lru_b_r, lru_w_i, lru_b_i, lru_lambda, lru_out_g, w_out, ln2_g, w_router_group, b_router_group, w_router_expert, b_router_expert, w_gate, w_up, w_down, final_g):
    b, s, d = x.shape
    depth = w_ada.shape[0]
    cw = conv_dw_b.shape[-1]
    fw = fox_out_g.shape[-1]
    lw = lru_lambda.shape[-1]
    heads = fox_f_bias.shape[-1]
    hd = fw // heads
    n_groups, per_group = b_router_expert.shape[1:]
    n_experts = n_groups * per_group
    off_f = 2 * cw + 3 * fw
    assert cw % LANES == 0 and lw == cw and fw % cw == 0 and hd == LANES and heads <= LANES
    assert n_groups + n_experts <= LANES

    c_pad = jnp.zeros((8, d), F32).at[:b].set(c)
    mod_all = _ada_mod(c_pad, w_ada, b_ada)

    tq = _pick(s, 512)
    for l in range(depth):
        mod = mod_all[l, :b].reshape(b, 1, 6 * d)
        sh1, sc1, g1, sh2, sc2, g2 = [mod[..., k * d:(k + 1) * d] for k in range(6)]

        w_l = w_in[l]
        w_main = jnp.concatenate([w_l[:, :off_f], w_l[:, off_f + heads:]], axis=1).astype(BF16)
        w_f = jnp.zeros((d, LANES), BF16).at[:, :heads].set(w_l[:, off_f:off_f + heads].astype(BF16))
        z, zf = _inproj(x, ln1_g[l].reshape(1, d), sc1, sh1, w_main, w_f)

        y_conv = _conv_branch(z, cw, conv_dw_w[l], conv_dw_b[l], conv_ln_g[l], conv_ln_b[l])

        f_bias = jnp.zeros((1, LANES), F32).at[0, :heads].set(fox_f_bias[l])
        cum = _forget_cumsum(zf, f_bias)
        cum_t = cum[:, :, :heads].transpose(0, 2, 1).reshape(b, heads, s // tq, 1, tq)
        o_fox = _attention(z, cum, cum_t, heads, hd, 2 * cw // hd, (2 * cw + fw) // hd, (2 * cw + 2 * fw) // hd)

        y_lru = _lru_branch(z, lw, off_f // lw, lru_conv_w[l], lru_conv_b[l],
                            _block_diag(lru_w_r[l]).astype(BF16), lru_b_r[l],
                            _block_diag(lru_w_i[l]).astype(BF16), lru_b_i[l], lru_lambda[l], lru_out_g[l])

        wo = w_out[l].astype(BF16)
        x = _outproj(y_conv, o_fox, y_lru, fox_out_g[l], wo[:cw], wo[cw:cw + fw], wo[cw + fw:], x, g1)

        w_r = jnp.concatenate(
            [w_router_group[l], w_router_expert[l].transpose(1, 0, 2).reshape(d, n_experts)], axis=1)
        w_r = jnp.zeros((d, LANES), F32).at[:, :n_groups + n_experts].set(w_r)
        w_hi = w_r.astype(BF16)
        w_lo = (w_r - w_hi.astype(F32)).astype(BF16)
        r_bias = jnp.zeros((1, LANES), F32).at[0, :n_groups + n_experts].set(
            jnp.concatenate([b_router_group[l], b_router_expert[l].reshape(-1)]))
        h2, ri, rg, cnt = _router(x, ln2_g[l].reshape(1, d), sc2, sh2, w_hi, w_lo, r_bias, n_groups, per_group)

        eid = ri[:, :, :TOP_K].reshape(-1)
        rank = ri[:, :, TOP_K:2 * TOP_K].reshape(-1)
        counts = cnt[0, n_groups:n_groups + n_experts].astype(jnp.int32)
        block_exp, row_tok, dest = _routing_tables(eid, rank, counts, MOE_ROWS)
        y = _moe_experts(block_exp, row_tok, h2.reshape(b * s, d), w_gate, w_up, w_down, l)
        x = _combine(dest, y, x.reshape(b * s, d), rg.reshape(b * s, LANES), g2, s).reshape(b, s, d)

    return _final_norm(x.reshape(b * s, d), final_g).reshape(b, s, d)
```

```python
import functools

import jax
import jax.numpy as jnp
from jax import lax
from jax.experimental import pallas as pl
from jax.experimental.pallas import tpu as pltpu

EPS = 1e-6
LRU_C = 8.0
TOP_K = 2
LANES = 128
MOE_ROWS = 128
VMEM_LIMIT = 56 * 1024 * 1024

F32 = jnp.float32
BF16 = jnp.bfloat16


def _pick(n, pref):
    if n <= pref:
        return n
    for t in range(pref, 7, -1):
        if n % t == 0 and t % 8 == 0:
            return t
    raise ValueError((n, pref))


def _params(*sem):
    return pltpu.CompilerParams(dimension_semantics=sem, vmem_limit_bytes=VMEM_LIMIT)


def _sigmoid(x):
    return jax.nn.sigmoid(x)


def _ada_kernel(c_ref, w_ref, b_ref, o_ref):
    c = c_ref[...]
    cond = (c * _sigmoid(c)).astype(BF16)
    o_ref[...] = jnp.dot(cond, w_ref[...].astype(BF16), preferred_element_type=F32) + b_ref[...]


def _ada_mod(c_pad, w_ada, b_ada):
    depth, d, n = w_ada.shape
    rows = c_pad.shape[0]
    tn = _pick(n, 1024)
    return pl.pallas_call(
        _ada_kernel,
        grid=(depth, n // tn),
        in_specs=[
            pl.BlockSpec((rows, d), lambda l, j: (0, 0)),
            pl.BlockSpec((None, d, tn), lambda l, j: (l, 0, j)),
            pl.BlockSpec((None, 1, tn), lambda l, j: (l, 0, j)),
        ],
        out_specs=pl.BlockSpec((None, rows, tn), lambda l, j: (l, 0, j)),
        out_shape=jax.ShapeDtypeStruct((depth, rows, n), F32),
        compiler_params=_params("parallel", "parallel"),
        name="ada_mod",
    )(c_pad, w_ada, b_ada.reshape(depth, 1, n))


def _modulated_norm(x, g, sc, sh):
    y = x * lax.rsqrt(jnp.mean(x * x, axis=-1, keepdims=True) + EPS) * g
    return y * (1.0 + sc) + sh


def _inproj_kernel(x_ref, g_ref, sc_ref, sh_ref, w_ref, wf_ref, z_ref, zf_ref, h_scr):
    @pl.when(pl.program_id(2) == 0)
    def _():
        hb = _modulated_norm(x_ref[...], g_ref[...], sc_ref[...], sh_ref[...]).astype(BF16)
        h_scr[...] = hb
        zf_ref[...] = jnp.dot(hb, wf_ref[...], preferred_element_type=F32)

    z_ref[...] = jnp.dot(h_scr[...], w_ref[...], preferred_element_type=F32)


def _inproj(x, g, sc, sh, w_main, w_f):
    b, s, d = x.shape
    n = w_main.shape[1]
    tm = _pick(s, 1024)
    tn = _pick(n, 1024)
    return pl.pallas_call(
        _inproj_kernel,
        grid=(b, s // tm, n // tn),
        in_specs=[
            pl.BlockSpec((None, tm, d), lambda bi, i, j: (bi, i, 0)),
            pl.BlockSpec((1, d), lambda bi, i, j: (0, 0)),
            pl.BlockSpec((None, 1, d), lambda bi, i, j: (bi, 0, 0)),
            pl.BlockSpec((None, 1, d), lambda bi, i, j: (bi, 0, 0)),
            pl.BlockSpec((d, tn), lambda bi, i, j: (0, j)),
            pl.BlockSpec((d, LANES), lambda bi, i, j: (0, 0)),
        ],
        out_specs=[
            pl.BlockSpec((None, tm, tn), lambda bi, i, j: (bi, i, j)),
            pl.BlockSpec((None, tm, LANES), lambda bi, i, j: (bi, i, 0)),
        ],
        out_shape=[jax.ShapeDtypeStruct((b, s, n), F32), jax.ShapeDtypeStruct((b, s, LANES), F32)],
        scratch_shapes=[pltpu.VMEM((tm, d), BF16)],
        compiler_params=_params("parallel", "parallel", "arbitrary"),
        name="inproj",
    )(x, g, sc, sh, w_main, w_f)


def _conv_kernel(a_ref, g_ref, ah_ref, gh_ref, w_ref, b_ref, lg_ref, lb_ref, o_ref, buf, ybuf, *, taps, ts, halo, rc):
    cw = a_ref.shape[-1]
    buf[halo:halo + ts, :] = a_ref[...] * _sigmoid(g_ref[...])
    prev = ah_ref[...] * _sigmoid(gh_ref[...])
    buf[0:halo, :] = jnp.where(pl.program_id(1) > 0, prev, 0.0)
    for c in range(cw // LANES):
        cs = slice(c * LANES, (c + 1) * LANES)
        accs = [None] * (ts // rc)
        for k in range(taps):
            wk = w_ref[k:k + 1, cs]
            off = halo - (taps - 1) + k
            for r in range(ts // rc):
                term = wk * buf[off + r * rc:off + (r + 1) * rc, cs]
                accs[r] = term if accs[r] is None else accs[r] + term
        for r in range(ts // rc):
            ybuf[r * rc:(r + 1) * rc, cs] = accs[r] + b_ref[:, cs]
    y = ybuf[...]
    mu = jnp.mean(y, axis=-1, keepdims=True)
    yc = y - mu
    var = jnp.mean(yc * yc, axis=-1, keepdims=True)
    yn = yc * lax.rsqrt(var + EPS) * lg_ref[...] + lb_ref[...]
    o_ref[...] = (yn * _sigmoid(yn)).astype(o_ref.dtype)


def _conv_branch(z, cw, dw_w, dw_b, ln_g, ln_b):
    b, s, _ = z.shape
    taps = dw_w.shape[0]
    halo = 32
    assert taps - 1 <= halo
    ts = _pick(s, 256)
    rc = _pick(ts, 64)
    hb = ts // halo
    kern = functools.partial(_conv_kernel, taps=taps, ts=ts, halo=halo, rc=rc)
    cur = lambda col: pl.BlockSpec((None, ts, cw), lambda bi, i: (bi, i, col))
    prev = lambda col: pl.BlockSpec((None, halo, cw), lambda bi, i: (bi, jnp.maximum(i * hb - 1, 0), col))
    vec = pl.BlockSpec((1, cw), lambda bi, i: (0, 0))
    return pl.pallas_call(
        kern,
        grid=(b, s // ts),
        in_specs=[cur(0), cur(1), prev(0), prev(1),
                  pl.BlockSpec((taps, cw), lambda bi, i: (0, 0)), vec, vec, vec],
        out_specs=pl.BlockSpec((None, ts, cw), lambda bi, i: (bi, i, 0)),
        out_shape=jax.ShapeDtypeStruct((b, s, cw), BF16),
        scratch_shapes=[pltpu.VMEM((halo + ts, cw), F32), pltpu.VMEM((ts, cw), F32)],
        compiler_params=_params("parallel", "parallel"),
        name="conv_branch",
    )(z, z, z, z, dw_w, dw_b.reshape(1, cw), ln_g.reshape(1, cw), ln_b.reshape(1, cw))


def _shift_rows(v, d, fill):
    rows = lax.broadcasted_iota(jnp.int32, v.shape, 0)
    return jnp.where(rows >= d, pltpu.roll(v, d, axis=0), fill)


def _scan_linear(a, bv):
    d = 1
    while d < a.shape[0]:
        bv = a * _shift_rows(bv, d, 0.0) + bv
        a = a * _shift_rows(a, d, 1.0)
        d *= 2
    return a, bv


def _scan_sum(v):
    d = 1
    while d < v.shape[0]:
        v = v + _shift_rows(v, d, 0.0)
        d *= 2
    return v


def _cum_kernel(zf_ref, fb_ref, o_ref, car, *, ts, rs):
    @pl.when(pl.program_id(1) == 0)
    def _():
        car[...] = jnp.zeros_like(car)

    h = car[...]
    for r in range(ts // rs):
        x = zf_ref[r * rs:(r + 1) * rs, :] + fb_ref[...]
        log_f = jnp.minimum(x, 0.0) - jnp.log1p(jnp.exp(-jnp.abs(x)))
        v = _scan_sum(log_f) + h
        o_ref[r * rs:(r + 1) * rs, :] = v
        h = v[rs - 1:rs, :]
    car[...] = h


def _forget_cumsum(zf, f_bias_pad):
    b, s, _ = zf.shape
    ts = _pick(s, 512)
    rs = _pick(ts, 128)
    return pl.pallas_call(
        functools.partial(_cum_kernel, ts=ts, rs=rs),
        grid=(b, s // ts),
        in_specs=[pl.BlockSpec((None, ts, LANES), lambda bi, i: (bi, i, 0)),
                  pl.BlockSpec((1, LANES), lambda bi, i: (0, 0))],
        out_specs=pl.BlockSpec((None, ts, LANES), lambda bi, i: (bi, i, 0)),
        out_shape=jax.ShapeDtypeStruct((b, s, LANES), F32),
        scratch_shapes=[pltpu.VMEM((1, LANES), F32)],
        compiler_params=_params("parallel", "arbitrary"),
        name="forget_cumsum",
    )(zf, f_bias_pad)


def _gelu_tanh(x):
    return 0.5 * x * (1.0 + jnp.tanh(0.7978845608028654 * (x + 0.044715 * (x * x * x))))


def _lru_kernel(u_ref, uh_ref, gate_ref, cw_ref, cb_ref, wr_ref, br_ref, wi_ref, bi_ref, lam_ref, og_ref,
                o_ref, ubuf, abuf, bbuf, car, *, taps, ts, halo, rs):
    lw = u_ref.shape[-1]

    @pl.when(pl.program_id(1) == 0)
    def _():
        car[...] = jnp.zeros_like(car)

    ubuf[halo:halo + ts, :] = u_ref[...]
    ubuf[0:halo, :] = jnp.where(pl.program_id(1) > 0, uh_ref[...], 0.0)
    xc = cb_ref[...]
    for k in range(taps):
        off = halo - (taps - 1) + k
        xc = xc + cw_ref[k:k + 1, :] * ubuf[off:off + ts, :]
    xb = xc.astype(BF16)
    r = _sigmoid(jnp.dot(xb, wr_ref[...], preferred_element_type=F32) + br_ref[...])
    ig = _sigmoid(jnp.dot(xb, wi_ref[...], preferred_element_type=F32) + bi_ref[...])
    nl = -lam_ref[...]
    softplus = jnp.maximum(nl, 0.0) + jnp.log1p(jnp.exp(-jnp.abs(nl)))
    log_a = (-LRU_C) * r * softplus
    th = jnp.tanh(log_a)
    abuf[...] = jnp.exp(log_a)
    bbuf[...] = jnp.sqrt(-2.0 * th / (1.0 - th)) * (ig * xc)
    for c in range(lw // LANES):
        cs = slice(c * LANES, (c + 1) * LANES)
        h = car[:, cs]
        for q in range(ts // rs):
            rows = slice(q * rs, (q + 1) * rs)
            a_cum, b_cum = _scan_linear(abuf[rows, cs], bbuf[rows, cs])
            hv = b_cum + a_cum * h
            bbuf[rows, cs] = hv
            h = hv[rs - 1:rs, :]
        car[:, cs] = h
    y = bbuf[...] * _gelu_tanh(gate_ref[...])
    y = y * lax.rsqrt(jnp.mean(y * y, axis=-1, keepdims=True) + EPS) * og_ref[...]
    o_ref[...] = y.astype(o_ref.dtype)


def _lru_branch(z, lw, col_x, conv_w, conv_b, wr_bd, b_r, wi_bd, b_i, lam, out_g):
    b, s, _ = z.shape
    taps = conv_w.shape[0]
    halo = 8
    assert taps - 1 <= halo
    ts = _pick(s, 256)
    rs = _pick(ts, 128)
    hb = ts // halo
    kern = functools.partial(_lru_kernel, taps=taps, ts=ts, halo=halo, rs=rs)
    vec = pl.BlockSpec((1, lw), lambda bi, i: (0, 0))
    mat = pl.BlockSpec((lw, lw), lambda bi, i: (0, 0))
    return pl.pallas_call(
        kern,
        grid=(b, s // ts),
        in_specs=[
            pl.BlockSpec((None, ts, lw), lambda bi, i: (bi, i, col_x)),
            pl.BlockSpec((None, halo, lw), lambda bi, i: (bi, jnp.maximum(i * hb - 1, 0), col_x)),
            pl.BlockSpec((None, ts, lw), lambda bi, i: (bi, i, col_x + 1)),
            pl.BlockSpec((taps, lw), lambda bi, i: (0, 0)), vec, mat, vec, mat, vec, vec, vec,
        ],
        out_specs=pl.BlockSpec((None, ts, lw), lambda bi, i: (bi, i, 0)),
        out_shape=jax.ShapeDtypeStruct((b, s, lw), BF16),
        scratch_shapes=[pltpu.VMEM((halo + ts, lw), F32), pltpu.VMEM((ts, lw), F32),
                        pltpu.VMEM((ts, lw), F32), pltpu.VMEM((1, lw), F32)],
        compiler_params=_params("parallel", "arbitrary"),
        name="lru_branch",
    )(z, z, z, conv_w, conv_b.reshape(1, lw), wr_bd, b_r.reshape(1, lw), wi_bd, b_i.reshape(1, lw),
      lam.reshape(1, lw), out_g.reshape(1, lw))


def _split3(t):
    hi = t.astype(BF16).astype(F32)
    r = t - hi
    mid = r.astype(BF16).astype(F32)
    return hi, mid, r - mid


def _attn_kernel(q_ref, k_ref, v_ref, cum_ref, o_ref, kaug, vaug, qaug, s_a, s_b, p_a, p_b, al_a, al_b, m_ref, acc_ref,
                 *, tq, tk, scale):
    h = pl.program_id(1)
    i = pl.program_id(2)
    s_len, hd = k_ref.shape
    inv_scale = 1.0 / scale
    c_exp = scale * 1.4426950408889634
    s_buf, p_buf, al_buf = (s_a, s_b), (p_a, p_b), (al_a, al_b)

    def head_cum(start, size):
        lane = lax.broadcasted_iota(jnp.int32, (size, LANES), 1)
        rows = pl.ds(pl.multiple_of(start, size), size)
        return jnp.sum(jnp.where(lane == h, cum_ref[rows, :], 0.0), axis=-1, keepdims=True) * inv_scale

    def key_rows(j):
        return pl.ds(pl.multiple_of(j * tk, tk), tk)

    @pl.when(i == 0)
    def _():
        lane = lax.broadcasted_iota(jnp.int32, (tk, LANES), 1)

        def body(t, carry):
            rows = key_rows(t)
            hi, mid, lo = _split3(-head_cum(t * tk, tk))
            aug = jnp.where(lane < 3, 1.0, jnp.where(lane == 3, hi, jnp.where(lane == 4, mid,
                            jnp.where(lane == 5, lo, 0.0))))
            kaug[t, 0:hd, :] = k_ref[rows, :].T.astype(BF16)
            kaug[t, hd:2 * hd, :] = aug.T.astype(BF16)
            vaug[rows, 0:hd] = v_ref[rows, :].astype(BF16)
            vaug[rows, hd:2 * hd] = jnp.where(lane == 0, 1.0, 0.0).astype(BF16)
            return carry

        lax.fori_loop(0, s_len // tk, body, 0)

    lane_q = lax.broadcasted_iota(jnp.int32, (tq, LANES), 1)
    hi, mid, lo = _split3(head_cum(i * tq, tq))
    qa = jnp.where(lane_q == 0, hi, jnp.where(lane_q == 1, mid, jnp.where(lane_q == 2, lo,
                   jnp.where(lane_q < 6, 1.0, 0.0))))
    qaug[:, 0:hd] = q_ref[...].astype(BF16)
    qaug[:, hd:2 * hd] = qa.astype(BF16)

    def scores_into(dst, j, masked=False):
        s = jnp.dot(qaug[...], kaug[j], preferred_element_type=F32)
        if masked:
            row = lax.broadcasted_iota(jnp.int32, s.shape, 0)
            col = lax.broadcasted_iota(jnp.int32, s.shape, 1)
            s = jnp.where(col <= row, s, -jnp.inf)
        dst[...] = s

    def softmax_into(src, dst_p, dst_alpha):
        s = src[...]
        m_old = m_ref[...]
        m_new = jnp.maximum(m_old, jnp.max(s, axis=-1, keepdims=True))
        dst_alpha[...] = jnp.exp2((m_old - m_new) * c_exp)
        m_ref[...] = m_new
        dst_p[...] = jnp.exp2((s - jnp.concatenate([m_new] * (tk // LANES), axis=1)) * c_exp).astype(BF16)

    def accumulate(src_alpha, src_p, j):
        alpha = jnp.concatenate([src_alpha[...]] * (acc_ref.shape[1] // LANES), axis=1)
        acc_ref[...] = alpha * acc_ref[...] + jnp.dot(src_p[...], vaug[key_rows(j), :], preferred_element_type=F32)

    def blk(t):
        return jnp.where(t == 0, i, jnp.maximum(t - 1, 0))

    def tick(t, par):
        scores_into(s_buf[1 - par], t)
        softmax_into(s_buf[par], p_buf[par], al_buf[par])
        accumulate(al_buf[1 - par], p_buf[1 - par], blk(t - 1))

    scores_into(s_buf[0], i, masked=True)
    m_ref[...] = jnp.full(m_ref.shape, -jnp.inf, F32)
    acc_ref[...] = jnp.zeros(acc_ref.shape, F32)
    p_buf[1][...] = jnp.zeros(p_buf[1].shape, BF16)
    al_buf[1][...] = jnp.ones(al_buf[1].shape, F32)

    def body(u, carry):
        tick(2 * u, 0)
        tick(2 * u + 1, 1)
        return carry

    lax.fori_loop(0, i // 2, body, 0)
    odd = i % 2 == 1

    @pl.when(odd)
    def _():
        tick(i - 1, 0)

    def drain(par):
        softmax_into(s_buf[par], p_buf[par], al_buf[par])
        accumulate(al_buf[1 - par], p_buf[1 - par], blk(i - 1))
        accumulate(al_buf[par], p_buf[par], blk(i))

    @pl.when(odd)
    def _():
        drain(1)

    @pl.when(jnp.logical_not(odd))
    def _():
        drain(0)

    o_ref[...] = acc_ref[:, 0:hd] / acc_ref[:, hd:hd + 1]


def _attention(z, cum, heads, hd, col_q, col_k, col_v):
    b, s, _ = z.shape
    tq = _pick(s, 512)
    tk = tq
    kern = functools.partial(_attn_kernel, tq=tq, tk=tk, scale=hd ** -0.5)
    return pl.pallas_call(
        kern,
        grid=(b, heads, s // tq),
        in_specs=[
            pl.BlockSpec((None, tq, hd), lambda bi, h, i: (bi, i, col_q + h)),
            pl.BlockSpec((None, s, hd), lambda bi, h, i: (bi, 0, col_k + h)),
            pl.BlockSpec((None, s, hd), lambda bi, h, i: (bi, 0, col_v + h)),
            pl.BlockSpec((None, s, LANES), lambda bi, h, i: (bi, 0, 0)),
        ],
        out_specs=pl.BlockSpec((None, tq, hd), lambda bi, h, i: (bi, i, h)),
        out_shape=jax.ShapeDtypeStruct((b, s, heads * hd), F32),
        scratch_shapes=[pltpu.VMEM((s // tk, 2 * hd, tk), BF16), pltpu.VMEM((s, 2 * hd), BF16),
                        pltpu.VMEM((tq, 2 * hd), BF16),
                        pltpu.VMEM((tq, tk), F32), pltpu.VMEM((tq, tk), F32),
                        pltpu.VMEM((tq, tk), BF16), pltpu.VMEM((tq, tk), BF16),
                        pltpu.VMEM((tq, LANES), F32), pltpu.VMEM((tq, LANES), F32), pltpu.VMEM((tq, LANES), F32),
                        pltpu.VMEM((tq, 2 * hd), F32)],
        compiler_params=_params("arbitrary", "arbitrary", "arbitrary"),
        name="fox_attention",
    )(z, z, z, cum)


def _outproj_kernel(yc_ref, o_ref, yl_ref, fg_ref, w1_ref, w2_ref, w3_ref, x_ref, g1_ref, out_ref, yf_scr):
    @pl.when(pl.program_id(2) == 0)
    def _():
        o = o_ref[...]
        yf = o * lax.rsqrt(jnp.mean(o * o, axis=-1, keepdims=True) + EPS) * fg_ref[...]
        yf_scr[...] = yf.astype(BF16)

    y = jnp.dot(yc_ref[...], w1_ref[...], preferred_element_type=F32)
    y = y + jnp.dot(yf_scr[...], w2_ref[...], preferred_element_type=F32)
    y = y + jnp.dot(yl_ref[...], w3_ref[...], preferred_element_type=F32)
    out_ref[...] = x_ref[...] + g1_ref[...] * y


def _outproj(y_conv, o_fox, y_lru, fox_g, w1, w2, w3, x, g1):
    b, s, d = x.shape
    cw, fw, lw = y_conv.shape[-1], o_fox.shape[-1], y_lru.shape[-1]
    tm = _pick(s, 512)
    tn = _pick(d, 1024)
    act = lambda w: pl.BlockSpec((None, tm, w), lambda bi, i, j: (bi, i, 0))
    wgt = lambda w: pl.BlockSpec((w, tn), lambda bi, i, j: (0, j))
    return pl.pallas_call(
        _outproj_kernel,
        grid=(b, s // tm, d // tn),
        in_specs=[act(cw), act(fw), act(lw), pl.BlockSpec((1, fw), lambda bi, i, j: (0, 0)),
                  wgt(cw), wgt(fw), wgt(lw),
                  pl.BlockSpec((None, tm, tn), lambda bi, i, j: (bi, i, j)),
                  pl.BlockSpec((None, 1, tn), lambda bi, i, j: (bi, 0, j))],
        out_specs=pl.BlockSpec((None, tm, tn), lambda bi, i, j: (bi, i, j)),
        out_shape=jax.ShapeDtypeStruct((b, s, d), F32),
        scratch_shapes=[pltpu.VMEM((tm, fw), BF16)],
        compiler_params=_params("parallel", "parallel", "arbitrary"),
        name="outproj",
    )(y_conv, o_fox, y_lru, fox_g.reshape(1, fw), w1, w2, w3, x, g1)


def _router_kernel(x_ref, g_ref, sc_ref, sh_ref, whi_ref, wlo_ref, rb_ref, h2_ref, ri_ref, rg_ref, cnt_ref, car,
                   *, n_groups, per_group):
    @pl.when(jnp.logical_and(pl.program_id(0) == 0, pl.program_id(1) == 0))
    def _():
        car[...] = jnp.zeros_like(car)

    h2 = _modulated_norm(x_ref[...], g_ref[...], sc_ref[...], sh_ref[...])
    h2_ref[...] = h2
    hi = h2.astype(BF16)
    lo = (h2 - hi.astype(F32)).astype(BF16)
    logits = (jnp.dot(hi, whi_ref[...], preferred_element_type=F32)
              + jnp.dot(lo, whi_ref[...], preferred_element_type=F32)
              + jnp.dot(hi, wlo_ref[...], preferred_element_type=F32)) + rb_ref[...]
    lane_i = lax.broadcasted_iota(jnp.int32, logits.shape, 1)
    lane = lane_i.astype(F32)
    big = float(LANES)

    def first_lane(hit):
        return jnp.min(jnp.where(hit, lane, big), axis=-1, keepdims=True)

    gmask = lane_i < n_groups
    lg1 = jnp.where(gmask, logits, -jnp.inf)
    m1 = jnp.max(lg1, axis=-1, keepdims=True)
    e1 = jnp.exp(lg1 - m1)
    p1 = e1 / jnp.sum(e1, axis=-1, keepdims=True)
    p_grp = jnp.max(p1, axis=-1, keepdims=True)
    grp = first_lane((p1 == p_grp) & gmask)
    lo_lane = n_groups + grp * per_group
    emask = (lane >= lo_lane) & (lane < lo_lane + per_group)
    lg2 = jnp.where(emask, logits, -jnp.inf)
    m2 = jnp.max(lg2, axis=-1, keepdims=True)
    e2 = jnp.exp(lg2 - m2)
    p2 = e2 / jnp.sum(e2, axis=-1, keepdims=True)
    v1 = jnp.max(jnp.where(emask, p2, -1.0), axis=-1, keepdims=True)
    i1 = first_lane((p2 == v1) & emask)
    rest = emask & (lane != i1)
    v2 = jnp.max(jnp.where(rest, p2, -1.0), axis=-1, keepdims=True)
    i2 = first_lane((p2 == v2) & rest)
    denom = v1 + v2
    gate1 = p_grp * (v1 / denom)
    gate2 = p_grp * (v2 / denom)
    tm = logits.shape[0]
    picked = jnp.where((lane == i1) | (lane == i2), 1.0, 0.0)
    tri = jnp.where(lax.broadcasted_iota(jnp.int32, (tm, tm), 0) > lax.broadcasted_iota(jnp.int32, (tm, tm), 1),
                    1.0, 0.0).astype(BF16)
    before = jnp.dot(tri, picked.astype(BF16), preferred_element_type=F32) + car[...]
    rank1 = jnp.sum(jnp.where(lane == i1, before, 0.0), axis=-1, keepdims=True)
    rank2 = jnp.sum(jnp.where(lane == i2, before, 0.0), axis=-1, keepdims=True)
    total = car[...] + jnp.sum(picked, axis=0, keepdims=True)
    car[...] = total
    cnt_ref[...] = total
    info = jnp.where(lane_i == 0, i1 - n_groups, jnp.where(lane_i == 1, i2 - n_groups,
                     jnp.where(lane_i == 2, rank1, jnp.where(lane_i == 3, rank2, 0.0))))
    ri_ref[...] = info.astype(jnp.int32)
    rg_ref[...] = jnp.where(lane_i == 0, gate1, jnp.where(lane_i == 1, gate2, 0.0))


def _router(x, g, sc, sh, w_hi, w_lo, r_bias, n_groups, per_group):
    b, s, d = x.shape
    tm = _pick(s, 512)
    kern = functools.partial(_router_kernel, n_groups=n_groups, per_group=per_group)
    row = lambda w: pl.BlockSpec((None, tm, w), lambda bi, i: (bi, i, 0))
    mod = pl.BlockSpec((None, 1, d), lambda bi, i: (bi, 0, 0))
    wsp = pl.BlockSpec((d, LANES), lambda bi, i: (0, 0))
    return pl.pallas_call(
        kern,
        grid=(b, s // tm),
        in_specs=[row(d), pl.BlockSpec((1, d), lambda bi, i: (0, 0)), mod, mod, wsp, wsp,
                  pl.BlockSpec((1, LANES), lambda bi, i: (0, 0))],
        out_specs=[row(d), row(LANES), row(LANES), pl.BlockSpec((1, LANES), lambda bi, i: (0, 0))],
        out_shape=[jax.ShapeDtypeStruct((b, s, d), F32), jax.ShapeDtypeStruct((b, s, LANES), jnp.int32),
                   jax.ShapeDtypeStruct((b, s, LANES), F32), jax.ShapeDtypeStruct((1, LANES), F32)],
        scratch_shapes=[pltpu.VMEM((1, LANES), F32)],
        compiler_params=_params("arbitrary", "arbitrary"),
        name="router",
    )(x, g, sc, sh, w_hi, w_lo, r_bias)


def _moe_kernel(bexp_ref, rtok_ref, h2_hbm, wg_ref, wu_ref, wd_ref, y_ref, xbuf, sem, wg_bf, wu_bf, wd_bf, *, rows):
    b = pl.program_id(0)
    nb = pl.num_programs(0)
    slot = b % 2

    def row_copy(tok, r, slot_):
        return pltpu.make_async_copy(h2_hbm.at[pl.ds(tok, 1)], xbuf.at[slot_, pl.ds(r, 1)], sem.at[slot_])

    def wait_block(slot_):
        pltpu.make_async_copy(h2_hbm.at[pl.ds(0, rows)], xbuf.at[slot_], sem.at[slot_]).wait()

    @pl.when(b == 0)
    def _():
        def body(r, carry):
            row_copy(rtok_ref[r], r, 0).start()
            return carry

        lax.fori_loop(0, rows, body, 0)

    changed = jnp.logical_or(b == 0, bexp_ref[b] != bexp_ref[jnp.maximum(b - 1, 0)])

    @pl.when(changed)
    def _():
        wg_bf[...] = wg_ref[...].astype(BF16)
        wu_bf[...] = wu_ref[...].astype(BF16)
        wd_bf[...] = wd_ref[...].astype(BF16)

    wait_block(slot)
    base = jnp.minimum(b + 1, nb - 1) * rows
    for r in range(rows):
        row_copy(rtok_ref[base + r], r, 1 - slot).start()
    xb = xbuf[slot].astype(BF16)
    gt = jnp.dot(xb, wg_bf[...], preferred_element_type=F32)
    up = jnp.dot(xb, wu_bf[...], preferred_element_type=F32)
    hb = (gt * _sigmoid(gt) * up).astype(BF16)
    y_ref[...] = jnp.dot(hb, wd_bf[...], preferred_element_type=F32)

    @pl.when(b == nb - 1)
    def _():
        wait_block(1 - slot)


def _moe_experts(block_exp, row_tok, h2_flat, w_gate, w_up, w_down, layer):
    _, _, d, f = w_gate.shape
    rows = MOE_ROWS
    n_blocks = block_exp.shape[0]
    grid_spec = pltpu.PrefetchScalarGridSpec(
        num_scalar_prefetch=2,
        grid=(n_blocks,),
        in_specs=[
            pl.BlockSpec(memory_space=pl.ANY),
            pl.BlockSpec((None, None, d, f), lambda bi, be, rt: (layer, be[bi], 0, 0)),
            pl.BlockSpec((None, None, d, f), lambda bi, be, rt: (layer, be[bi], 0, 0)),
            pl.BlockSpec((None, None, f, d), lambda bi, be, rt: (layer, be[bi], 0, 0)),
        ],
        out_specs=pl.BlockSpec((rows, d), lambda bi, be, rt: (bi, 0)),
        scratch_shapes=[pltpu.VMEM((2, rows, d), F32), pltpu.SemaphoreType.DMA((2,)),
                        pltpu.VMEM((d, f), BF16), pltpu.VMEM((d, f), BF16), pltpu.VMEM((f, d), BF16)],
    )
    return pl.pallas_call(
        functools.partial(_moe_kernel, rows=rows),
        grid_spec=grid_spec,
        out_shape=jax.ShapeDtypeStruct((n_blocks * rows, d), F32),
        compiler_params=_params("arbitrary"),
        name="moe_experts",
    )(block_exp, row_tok, h2_flat, w_gate, w_up, w_down)


def _combine_kernel(dest_ref, y_hbm, x_ref, rg_ref, g2_ref, out_ref, ybuf, sem, *, tm):
    i = pl.program_id(0)
    n = pl.num_programs(0)
    slot = i % 2

    def start(row, r, k, slot_):
        pltpu.make_async_copy(y_hbm.at[pl.ds(row, 1)], ybuf.at[slot_, k, pl.ds(r, 1)], sem.at[slot_]).start()

    def wait_tile(slot_):
        for k in range(TOP_K):
            pltpu.make_async_copy(y_hbm.at[pl.ds(0, tm)], ybuf.at[slot_, k], sem.at[slot_]).wait()

    @pl.when(i == 0)
    def _():
        def body(r, carry):
            for k in range(TOP_K):
                start(dest_ref[r * TOP_K + k], r, k, 0)
            return carry

        lax.fori_loop(0, tm, body, 0)

    wait_tile(slot)
    base = jnp.minimum(i + 1, n - 1) * (tm * TOP_K)
    for r in range(tm):
        for k in range(TOP_K):
            start(dest_ref[base + r * TOP_K + k], r, k, 1 - slot)
    gates = rg_ref[...]
    moe = ybuf[slot, 0] * gates[:, 0:1]
    for k in range(1, TOP_K):
        moe = moe + ybuf[slot, k] * gates[:, k:k + 1]
    out_ref[...] = x_ref[...] + g2_ref[...] * moe

    @pl.when(i == n - 1)
    def _():
        wait_tile(1 - slot)


def _combine(dest, y, x_flat, rg_flat, g2, seq):
    t, d = x_flat.shape
    tm = _pick(seq, 128)
    grid_spec = pltpu.PrefetchScalarGridSpec(
        num_scalar_prefetch=1,
        grid=(t // tm,),
        in_specs=[
            pl.BlockSpec(memory_space=pl.ANY),
            pl.BlockSpec((tm, d), lambda i, de: (i, 0)),
            pl.BlockSpec((tm, LANES), lambda i, de: (i, 0)),
            pl.BlockSpec((None, 1, d), lambda i, de: ((i * tm) // seq, 0, 0)),
        ],
        out_specs=pl.BlockSpec((tm, d), lambda i, de: (i, 0)),
        scratch_shapes=[pltpu.VMEM((2, TOP_K, tm, d), F32), pltpu.SemaphoreType.DMA((2,))],
    )
    return pl.pallas_call(
        functools.partial(_combine_kernel, tm=tm),
        grid_spec=grid_spec,
        out_shape=jax.ShapeDtypeStruct((t, d), F32),
        compiler_params=_params("arbitrary"),
        name="moe_combine",
    )(dest, y, x_flat, rg_flat, g2)


def _final_norm_kernel(x_ref, g_ref, o_ref):
    x = x_ref[...]
    o_ref[...] = x * lax.rsqrt(jnp.mean(x * x, axis=-1, keepdims=True) + EPS) * g_ref[...]


def _final_norm(x_flat, g):
    t, d = x_flat.shape
    tm = _pick(t, 512)
    return pl.pallas_call(
        _final_norm_kernel,
        grid=(t // tm,),
        in_specs=[pl.BlockSpec((tm, d), lambda i: (i, 0)), pl.BlockSpec((1, d), lambda i: (0, 0))],
        out_specs=pl.BlockSpec((tm, d), lambda i: (i, 0)),
        out_shape=jax.ShapeDtypeStruct((t, d), F32),
        compiler_params=_params("parallel"),
        name="final_norm",
    )(x_flat, g.reshape(1, d))


def _routing_tables(eid, rank, counts, rows):
    n_assign = eid.shape[0]
    n_experts = counts.shape[0]
    padded = (counts + rows - 1) // rows * rows
    pends = jnp.cumsum(padded)
    pstarts = pends - padded
    onehot = eid[:, None] == jnp.arange(n_experts, dtype=jnp.int32)[None, :]
    dest = (jnp.sum(jnp.where(onehot, pstarts[None, :], 0), axis=1) + rank).astype(jnp.int32)
    n_blocks = -(-n_assign // rows) + n_experts
    row_tok = jnp.zeros((n_blocks * rows,), jnp.int32).at[dest].set(jnp.arange(n_assign, dtype=jnp.int32) // TOP_K)
    first_row = jnp.arange(n_blocks, dtype=jnp.int32) * rows
    block_exp = jnp.minimum(jnp.sum((pends[None, :] <= first_row[:, None]).astype(jnp.int32), axis=1), n_experts - 1)
    return block_exp.astype(jnp.int32), row_tok, dest


def _block_diag(w):
    nb, bw, _ = w.shape
    eye = jnp.eye(nb, dtype=w.dtype)
    return (eye[:, None, :, None] * w[:, :, None, :]).reshape(nb * bw, nb * bw)


def kernel(x, c, w_ada, b_ada, ln1_g, w_in, conv_dw_w, conv_dw_b, conv_ln_g, conv_ln_b, fox_f_bias, fox_out_g, lru_conv_w, lru_conv_b, lru_w_r, lru_b_r, lru_w_i, lru_b_i, lru_lambda, lru_out_g, w_out, ln2_g, w_router_group, b_router_group, w_router_expert, b_router_expert, w_gate, w_up, w_down, final_g):
    b, s, d = x.shape
    depth = w_ada.shape[0]
    cw = conv_dw_b.shape[-1]
    fw = fox_out_g.shape[-1]
    lw = lru_lambda.shape[-1]
    heads = fox_f_bias.shape[-1]
    hd = fw // heads
    n_groups, per_group = b_router_expert.shape[1:]
    n_experts = n_groups * per_group
    off_f = 2 * cw + 3 * fw
    assert cw % LANES == 0 and lw == cw and fw % cw == 0 and hd == LANES and heads <= LANES
    assert n_groups + n_experts <= LANES

    c_pad = jnp.zeros((8, d), F32).at[:b].set(c)
    mod_all = _ada_mod(c_pad, w_ada, b_ada)

    for l in range(depth):
        mod = mod_all[l, :b].reshape(b, 1, 6 * d)
        sh1, sc1, g1, sh2, sc2, g2 = [mod[..., k * d:(k + 1) * d] for k in range(6)]

        w_l = w_in[l]
        w_main = jnp.concatenate([w_l[:, :off_f], w_l[:, off_f + heads:]], axis=1).astype(BF16)
        w_f = jnp.zeros((d, LANES), BF16).at[:, :heads].set(w_l[:, off_f:off_f + heads].astype(BF16))
        z, zf = _inproj(x, ln1_g[l].reshape(1, d), sc1, sh1, w_main, w_f)

        y_conv = _conv_branch(z, cw, conv_dw_w[l], conv_dw_b[l], conv_ln_g[l], conv_ln_b[l])

        f_bias = jnp.zeros((1, LANES), F32).at[0, :heads].set(fox_f_bias[l])
        cum = _forget_cumsum(zf, f_bias)
        o_fox = _attention(z, cum, heads, hd, 2 * cw // hd, (2 * cw + fw) // hd, (2 * cw + 2 * fw) // hd)

        y_lru = _lru_branch(z, lw, off_f // lw, lru_conv_w[l], lru_conv_b[l],
                            _block_diag(lru_w_r[l]).astype(BF16), lru_b_r[l],
                            _block_diag(lru_w_i[l]).astype(BF16), lru_b_i[l], lru_lambda[l], lru_out_g[l])

        wo = w_out[l].astype(BF16)
        x = _outproj(y_conv, o_fox, y_lru, fox_out_g[l], wo[:cw], wo[cw:cw + fw], wo[cw + fw:], x, g1)

        w_r = jnp.concatenate(
            [w_router_group[l], w_router_expert[l].transpose(1, 0, 2).reshape(d, n_experts)], axis=1)
        w_r = jnp.zeros((d, LANES), F32).at[:, :n_groups + n_experts].set(w_r)
        w_hi = w_r.astype(BF16)
        w_lo = (w_r - w_hi.astype(F32)).astype(BF16)
        r_bias = jnp.zeros((1, LANES), F32).at[0, :n_groups + n_experts].set(
            jnp.concatenate([b_router_group[l], b_router_expert[l].reshape(-1)]))
        h2, ri, rg, cnt = _router(x, ln2_g[l].reshape(1, d), sc2, sh2, w_hi, w_lo, r_bias, n_groups, per_group)

        eid = ri[:, :, :TOP_K].reshape(-1)
        rank = ri[:, :, TOP_K:2 * TOP_K].reshape(-1)
        counts = cnt[0, n_groups:n_groups + n_experts].astype(jnp.int32)
        block_exp, row_tok, dest = _routing_tables(eid, rank, counts, MOE_ROWS)
        y = _moe_experts(block_exp, row_tok, h2.reshape(b * s, d), w_gate, w_up, w_down, l)
        x = _combine(dest, y, x.reshape(b * s, d), rg.reshape(b * s, LANES), g2, s).reshape(b, s, d)

    return _final_norm(x.reshape(b * s, d), final_g).reshape(b, s, d)
```

```python
import functools

import jax
import jax.numpy as jnp
from jax import lax
from jax.experimental import pallas as pl
from jax.experimental.pallas import tpu as pltpu

EPS = 1e-6
LRU_C = 8.0
TOP_K = 2
LANES = 128
MOE_ROWS = 128
VMEM_LIMIT = 56 * 1024 * 1024

F32 = jnp.float32
BF16 = jnp.bfloat16


def _pick(n, pref):
    if n <= pref:
        return n
    for t in range(pref, 7, -1):
        if n % t == 0 and t % 8 == 0:
            return t
    raise ValueError((n, pref))


def _params(*sem):
    return pltpu.CompilerParams(dimension_semantics=sem, vmem_limit_bytes=VMEM_LIMIT)


def _sigmoid(x):
    return jax.nn.sigmoid(x)


def _ada_kernel(c_ref, w_ref, b_ref, o_ref):
    c = c_ref[...]
    cond = (c * _sigmoid(c)).astype(BF16)
    o_ref[...] = jnp.dot(cond, w_ref[...].astype(BF16), preferred_element_type=F32) + b_ref[...]


def _ada_mod(c_pad, w_ada, b_ada):
    depth, d, n = w_ada.shape
    rows = c_pad.shape[0]
    tn = _pick(n, 1024)
    return pl.pallas_call(
        _ada_kernel,
        grid=(depth, n // tn),
        in_specs=[
            pl.BlockSpec((rows, d), lambda l, j: (0, 0)),
            pl.BlockSpec((None, d, tn), lambda l, j: (l, 0, j)),
            pl.BlockSpec((None, 1, tn), lambda l, j: (l, 0, j)),
        ],
        out_specs=pl.BlockSpec((None, rows, tn), lambda l, j: (l, 0, j)),
        out_shape=jax.ShapeDtypeStruct((depth, rows, n), F32),
        compiler_params=_params("parallel", "parallel"),
        name="ada_mod",
    )(c_pad, w_ada, b_ada.reshape(depth, 1, n))


def _modulated_norm(x, g, sc, sh):
    y = x * lax.rsqrt(jnp.mean(x * x, axis=-1, keepdims=True) + EPS) * g
    return y * (1.0 + sc) + sh


def _inproj_kernel(x_ref, g_ref, sc_ref, sh_ref, w_ref, wf_ref, z_ref, zf_ref, h_scr):
    @pl.when(pl.program_id(2) == 0)
    def _():
        hb = _modulated_norm(x_ref[...], g_ref[...], sc_ref[...], sh_ref[...]).astype(BF16)
        h_scr[...] = hb
        zf_ref[...] = jnp.dot(hb, wf_ref[...], preferred_element_type=F32)

    z_ref[...] = jnp.dot(h_scr[...], w_ref[...], preferred_element_type=F32)


def _inproj(x, g, sc, sh, w_main, w_f):
    b, s, d = x.shape
    n = w_main.shape[1]
    tm = _pick(s, 1024)
    tn = _pick(n, 1024)
    return pl.pallas_call(
        _inproj_kernel,
        grid=(b, s // tm, n // tn),
        in_specs=[
            pl.BlockSpec((None, tm, d), lambda bi, i, j: (bi, i, 0)),
            pl.BlockSpec((1, d), lambda bi, i, j: (0, 0)),
            pl.BlockSpec((None, 1, d), lambda bi, i, j: (bi, 0, 0)),
            pl.BlockSpec((None, 1, d), lambda bi, i, j: (bi, 0, 0)),
            pl.BlockSpec((d, tn), lambda bi, i, j: (0, j)),
            pl.BlockSpec((d, LANES), lambda bi, i, j: (0, 0)),
        ],
        out_specs=[
            pl.BlockSpec((None, tm, tn), lambda bi, i, j: (bi, i, j)),
            pl.BlockSpec((None, tm, LANES), lambda bi, i, j: (bi, i, 0)),
        ],
        out_shape=[jax.ShapeDtypeStruct((b, s, n), F32), jax.ShapeDtypeStruct((b, s, LANES), F32)],
        scratch_shapes=[pltpu.VMEM((tm, d), BF16)],
        compiler_params=_params("parallel", "parallel", "arbitrary"),
        name="inproj",
    )(x, g, sc, sh, w_main, w_f)


def _conv_kernel(a_ref, g_ref, ah_ref, gh_ref, w_ref, b_ref, lg_ref, lb_ref, o_ref, buf, ybuf, shifted,
                 *, taps, ts, halo, rc):
    cw = a_ref.shape[-1]
    buf[halo:halo + ts, :] = a_ref[...] * _sigmoid(g_ref[...])
    prev = ah_ref[...] * _sigmoid(gh_ref[...])
    buf[0:halo, :] = jnp.where(pl.program_id(1) > 0, prev, 0.0)
    first = halo - (taps - 1)
    for c in range(cw // LANES):
        cs = slice(c * LANES, (c + 1) * LANES)
        for r in range(ts // rc):
            acc = b_ref[:, cs]
            for rem in range(8):
                group = [(k, first + k - rem) for k in range(taps) if (first + k) % 8 == rem]
                if not group:
                    continue
                lo = min(a for _, a in group)
                hi = max(a for _, a in group)
                shifted[0:hi - lo + rc, :] = buf[r * rc + rem + lo:r * rc + rem + hi + rc, cs]
                for k, a in group:
                    acc = acc + w_ref[k:k + 1, cs] * shifted[a - lo:a - lo + rc, :]
            ybuf[r * rc:(r + 1) * rc, cs] = acc
    y = ybuf[...]
    mu = jnp.mean(y, axis=-1, keepdims=True)
    yc = y - mu
    var = jnp.mean(yc * yc, axis=-1, keepdims=True)
    yn = yc * lax.rsqrt(var + EPS) * lg_ref[...] + lb_ref[...]
    o_ref[...] = (yn * _sigmoid(yn)).astype(o_ref.dtype)


def _conv_branch(z, cw, dw_w, dw_b, ln_g, ln_b):
    b, s, _ = z.shape
    taps = dw_w.shape[0]
    halo = 32
    assert taps - 1 <= halo
    ts = _pick(s, 256)
    rc = _pick(ts, 128)
    hb = ts // halo
    kern = functools.partial(_conv_kernel, taps=taps, ts=ts, halo=halo, rc=rc)
    cur = lambda col: pl.BlockSpec((None, ts, cw), lambda bi, i: (bi, i, col))
    prev = lambda col: pl.BlockSpec((None, halo, cw), lambda bi, i: (bi, jnp.maximum(i * hb - 1, 0), col))
    vec = pl.BlockSpec((1, cw), lambda bi, i: (0, 0))
    return pl.pallas_call(
        kern,
        grid=(b, s // ts),
        in_specs=[cur(0), cur(1), prev(0), prev(1),
                  pl.BlockSpec((taps, cw), lambda bi, i: (0, 0)), vec, vec, vec],
        out_specs=pl.BlockSpec((None, ts, cw), lambda bi, i: (bi, i, 0)),
        out_shape=jax.ShapeDtypeStruct((b, s, cw), BF16),
        scratch_shapes=[pltpu.VMEM((halo + ts, cw), F32), pltpu.VMEM((ts, cw), F32),
                        pltpu.VMEM((rc + halo, LANES), F32)],
        compiler_params=_params("parallel", "parallel"),
        name="conv_branch",
    )(z, z, z, z, dw_w, dw_b.reshape(1, cw), ln_g.reshape(1, cw), ln_b.reshape(1, cw))


def _shift_rows(v, d, fill):
    rows = lax.broadcasted_iota(jnp.int32, v.shape, 0)
    return jnp.where(rows >= d, pltpu.roll(v, d, axis=0), fill)


def _scan_linear(a, bv):
    d = 1
    while d < a.shape[0]:
        bv = a * _shift_rows(bv, d, 0.0) + bv
        a = a * _shift_rows(a, d, 1.0)
        d *= 2
    return a, bv


def _scan_sum(v):
    d = 1
    while d < v.shape[0]:
        v = v + _shift_rows(v, d, 0.0)
        d *= 2
    return v


def _cum_kernel(zf_ref, fb_ref, o_ref, car, *, ts, rs):
    @pl.when(pl.program_id(1) == 0)
    def _():
        car[...] = jnp.zeros_like(car)

    h = car[...]
    for r in range(ts // rs):
        x = zf_ref[r * rs:(r + 1) * rs, :] + fb_ref[...]
        log_f = jnp.minimum(x, 0.0) - jnp.log1p(jnp.exp(-jnp.abs(x)))
        v = _scan_sum(log_f) + h
        o_ref[r * rs:(r + 1) * rs, :] = v
        h = v[rs - 1:rs, :]
    car[...] = h


def _forget_cumsum(zf, f_bias_pad):
    b, s, _ = zf.shape
    ts = _pick(s, 512)
    rs = _pick(ts, 128)
    return pl.pallas_call(
        functools.partial(_cum_kernel, ts=ts, rs=rs),
        grid=(b, s // ts),
        in_specs=[pl.BlockSpec((None, ts, LANES), lambda bi, i: (bi, i, 0)),
                  pl.BlockSpec((1, LANES), lambda bi, i: (0, 0))],
        out_specs=pl.BlockSpec((None, ts, LANES), lambda bi, i: (bi, i, 0)),
        out_shape=jax.ShapeDtypeStruct((b, s, LANES), F32),
        scratch_shapes=[pltpu.VMEM((1, LANES), F32)],
        compiler_params=_params("parallel", "arbitrary"),
        name="forget_cumsum",
    )(zf, f_bias_pad)


def _gelu_tanh(x):
    return 0.5 * x * (1.0 + jnp.tanh(0.7978845608028654 * (x + 0.044715 * (x * x * x))))


def _lru_kernel(u_ref, uh_ref, gate_ref, cw_ref, cb_ref, wr_ref, br_ref, wi_ref, bi_ref, lam_ref, og_ref,
                o_ref, ubuf, abuf, bbuf, car, *, taps, ts, halo, rs):
    lw = u_ref.shape[-1]

    @pl.when(pl.program_id(1) == 0)
    def _():
        car[...] = jnp.zeros_like(car)

    ubuf[halo:halo + ts, :] = u_ref[...]
    ubuf[0:halo, :] = jnp.where(pl.program_id(1) > 0, uh_ref[...], 0.0)
    xc = cb_ref[...]
    for k in range(taps):
        off = halo - (taps - 1) + k
        xc = xc + cw_ref[k:k + 1, :] * ubuf[off:off + ts, :]
    xb = xc.astype(BF16)
    r = _sigmoid(jnp.dot(xb, wr_ref[...], preferred_element_type=F32) + br_ref[...])
    ig = _sigmoid(jnp.dot(xb, wi_ref[...], preferred_element_type=F32) + bi_ref[...])
    nl = -lam_ref[...]
    softplus = jnp.maximum(nl, 0.0) + jnp.log1p(jnp.exp(-jnp.abs(nl)))
    log_a = (-LRU_C) * r * softplus
    th = jnp.tanh(log_a)
    abuf[...] = jnp.exp(log_a)
    bbuf[...] = jnp.sqrt(-2.0 * th / (1.0 - th)) * (ig * xc)
    for c in range(lw // LANES):
        cs = slice(c * LANES, (c + 1) * LANES)
        h = car[:, cs]
        for q in range(ts // rs):
            rows = slice(q * rs, (q + 1) * rs)
            a_cum, b_cum = _scan_linear(abuf[rows, cs], bbuf[rows, cs])
            hv = b_cum + a_cum * h
            bbuf[rows, cs] = hv
            h = hv[rs - 1:rs, :]
        car[:, cs] = h
    y = bbuf[...] * _gelu_tanh(gate_ref[...])
    y = y * lax.rsqrt(jnp.mean(y * y, axis=-1, keepdims=True) + EPS) * og_ref[...]
    o_ref[...] = y.astype(o_ref.dtype)


def _lru_branch(z, lw, col_x, conv_w, conv_b, wr_bd, b_r, wi_bd, b_i, lam, out_g):
    b, s, _ = z.shape
    taps = conv_w.shape[0]
    halo = 8
    assert taps - 1 <= halo
    ts = _pick(s, 256)
    rs = _pick(ts, 128)
    hb = ts // halo
    kern = functools.partial(_lru_kernel, taps=taps, ts=ts, halo=halo, rs=rs)
    vec = pl.BlockSpec((1, lw), lambda bi, i: (0, 0))
    mat = pl.BlockSpec((lw, lw), lambda bi, i: (0, 0))
    return pl.pallas_call(
        kern,
        grid=(b, s // ts),
        in_specs=[
            pl.BlockSpec((None, ts, lw), lambda bi, i: (bi, i, col_x)),
            pl.BlockSpec((None, halo, lw), lambda bi, i: (bi, jnp.maximum(i * hb - 1, 0), col_x)),
            pl.BlockSpec((None, ts, lw), lambda bi, i: (bi, i, col_x + 1)),
            pl.BlockSpec((taps, lw), lambda bi, i: (0, 0)), vec, mat, vec, mat, vec, vec, vec,
        ],
        out_specs=pl.BlockSpec((None, ts, lw), lambda bi, i: (bi, i, 0)),
        out_shape=jax.ShapeDtypeStruct((b, s, lw), BF16),
        scratch_shapes=[pltpu.VMEM((halo + ts, lw), F32), pltpu.VMEM((ts, lw), F32),
                        pltpu.VMEM((ts, lw), F32), pltpu.VMEM((1, lw), F32)],
        compiler_params=_params("parallel", "arbitrary"),
        name="lru_branch",
    )(z, z, z, conv_w, conv_b.reshape(1, lw), wr_bd, b_r.reshape(1, lw), wi_bd, b_i.reshape(1, lw),
      lam.reshape(1, lw), out_g.reshape(1, lw))


def _split3(t):
    hi = t.astype(BF16).astype(F32)
    r = t - hi
    mid = r.astype(BF16).astype(F32)
    return hi, mid, r - mid


def _attn_kernel(q_ref, k_ref, v_ref, cum_ref, o_ref, kaug, vaug, qaug, s_a, s_b, p_a, p_b, al_a, al_b, m_ref, acc_ref,
                 *, tq, tk, scale):
    h = pl.program_id(1)
    i = pl.program_id(2)
    s_len, hd = k_ref.shape
    inv_scale = 1.0 / scale
    c_exp = scale * 1.4426950408889634
    s_buf, p_buf, al_buf = (s_a, s_b), (p_a, p_b), (al_a, al_b)

    def head_cum(start, size):
        lane = lax.broadcasted_iota(jnp.int32, (size, LANES), 1)
        rows = pl.ds(pl.multiple_of(start, size), size)
        return jnp.sum(jnp.where(lane == h, cum_ref[rows, :], 0.0), axis=-1, keepdims=True) * inv_scale

    def key_rows(j):
        return pl.ds(pl.multiple_of(j * tk, tk), tk)

    @pl.when(i == 0)
    def _():
        lane = lax.broadcasted_iota(jnp.int32, (tk, LANES), 1)

        def body(t, carry):
            rows = key_rows(t)
            hi, mid, lo = _split3(-head_cum(t * tk, tk))
            aug = jnp.where(lane < 3, 1.0, jnp.where(lane == 3, hi, jnp.where(lane == 4, mid,
                            jnp.where(lane == 5, lo, 0.0))))
            kaug[t, 0:hd, :] = k_ref[rows, :].T.astype(BF16)
            kaug[t, hd:2 * hd, :] = aug.T.astype(BF16)
            vaug[rows, 0:hd] = v_ref[rows, :].astype(BF16)
            vaug[rows, hd:2 * hd] = jnp.where(lane == 0, 1.0, 0.0).astype(BF16)
            return carry

        lax.fori_loop(0, s_len // tk, body, 0)

    lane_q = lax.broadcasted_iota(jnp.int32, (tq, LANES), 1)
    hi, mid, lo = _split3(head_cum(i * tq, tq))
    qa = jnp.where(lane_q == 0, hi, jnp.where(lane_q == 1, mid, jnp.where(lane_q == 2, lo,
                   jnp.where(lane_q < 6, 1.0, 0.0))))
    qaug[:, 0:hd] = q_ref[...].astype(BF16)
    qaug[:, hd:2 * hd] = qa.astype(BF16)

    def scores_into(dst, j, masked=False):
        s = jnp.dot(qaug[...], kaug[j], preferred_element_type=F32)
        if masked:
            row = lax.broadcasted_iota(jnp.int32, s.shape, 0)
            col = lax.broadcasted_iota(jnp.int32, s.shape, 1)
            s = jnp.where(col <= row, s, -jnp.inf)
        dst[...] = s

    def softmax_into(src, dst_p, dst_alpha):
        s = src[...]
        m_old = m_ref[...]
        m_new = jnp.maximum(m_old, jnp.max(s, axis=-1, keepdims=True))
        dst_alpha[...] = jnp.exp2((m_old - m_new) * c_exp)
        m_ref[...] = m_new
        dst_p[...] = jnp.exp2((s - jnp.concatenate([m_new] * (tk // LANES), axis=1)) * c_exp).astype(BF16)

    def accumulate(src_alpha, src_p, j):
        alpha = jnp.concatenate([src_alpha[...]] * (acc_ref.shape[1] // LANES), axis=1)
        acc_ref[...] = alpha * acc_ref[...] + jnp.dot(src_p[...], vaug[key_rows(j), :], preferred_element_type=F32)

    def blk(t):
        return jnp.where(t == 0, i, jnp.maximum(t - 1, 0))

    def tick(t, par):
        scores_into(s_buf[1 - par], t)
        softmax_into(s_buf[par], p_buf[par], al_buf[par])
        accumulate(al_buf[1 - par], p_buf[1 - par], blk(t - 1))

    scores_into(s_buf[0], i, masked=True)
    m_ref[...] = jnp.full(m_ref.shape, -jnp.inf, F32)
    acc_ref[...] = jnp.zeros(acc_ref.shape, F32)
    p_buf[1][...] = jnp.zeros(p_buf[1].shape, BF16)
    al_buf[1][...] = jnp.ones(al_buf[1].shape, F32)

    def body(u, carry):
        tick(2 * u, 0)
        tick(2 * u + 1, 1)
        return carry

    lax.fori_loop(0, i // 2, body, 0)
    odd = i % 2 == 1

    @pl.when(odd)
    def _():
        tick(i - 1, 0)

    def drain(par):
        softmax_into(s_buf[par], p_buf[par], al_buf[par])
        accumulate(al_buf[1 - par], p_buf[1 - par], blk(i - 1))
        accumulate(al_buf[par], p_buf[par], blk(i))

    @pl.when(odd)
    def _():
        drain(1)

    @pl.when(jnp.logical_not(odd))
    def _():
        drain(0)

    o_ref[...] = acc_ref[:, 0:hd] / acc_ref[:, hd:hd + 1]


def _attention(z, cum, heads, hd, col_q, col_k, col_v):
    b, s, _ = z.shape
    tq = _pick(s, 512)
    tk = tq
    kern = functools.partial(_attn_kernel, tq=tq, tk=tk, scale=hd ** -0.5)
    return pl.pallas_call(
        kern,
        grid=(b, heads, s // tq),
        in_specs=[
            pl.BlockSpec((None, tq, hd), lambda bi, h, i: (bi, i, col_q + h)),
            pl.BlockSpec((None, s, hd), lambda bi, h, i: (bi, 0, col_k + h)),
            pl.BlockSpec((None, s, hd), lambda bi, h, i: (bi, 0, col_v + h)),
            pl.BlockSpec((None, s, LANES), lambda bi, h, i: (bi, 0, 0)),
        ],
        out_specs=pl.BlockSpec((None, tq, hd), lambda bi, h, i: (bi, i, h)),
        out_shape=jax.ShapeDtypeStruct((b, s, heads * hd), F32),
        scratch_shapes=[pltpu.VMEM((s // tk, 2 * hd, tk), BF16), pltpu.VMEM((s, 2 * hd), BF16),
                        pltpu.VMEM((tq, 2 * hd), BF16),
                        pltpu.VMEM((tq, tk), F32), pltpu.VMEM((tq, tk), F32),
                        pltpu.VMEM((tq, tk), BF16), pltpu.VMEM((tq, tk), BF16),
                        pltpu.VMEM((tq, LANES), F32), pltpu.VMEM((tq, LANES), F32), pltpu.VMEM((tq, LANES), F32),
                        pltpu.VMEM((tq, 2 * hd), F32)],
        compiler_params=_params("arbitrary", "arbitrary", "arbitrary"),
        name="fox_attention",
    )(z, z, z, cum)


def _outproj_kernel(yc_ref, o_ref, yl_ref, fg_ref, w1_ref, w2_ref, w3_ref, x_ref, g1_ref, out_ref, yf_scr):
    @pl.when(pl.program_id(2) == 0)
    def _():
        o = o_ref[...]
        yf = o * lax.rsqrt(jnp.mean(o * o, axis=-1, keepdims=True) + EPS) * fg_ref[...]
        yf_scr[...] = yf.astype(BF16)

    y = jnp.dot(yc_ref[...], w1_ref[...], preferred_element_type=F32)
    y = y + jnp.dot(yf_scr[...], w2_ref[...], preferred_element_type=F32)
    y = y + jnp.dot(yl_ref[...], w3_ref[...], preferred_element_type=F32)
    out_ref[...] = x_ref[...] + g1_ref[...] * y


def _outproj(y_conv, o_fox, y_lru, fox_g, w1, w2, w3, x, g1):
    b, s, d = x.shape
    cw, fw, lw = y_conv.shape[-1], o_fox.shape[-1], y_lru.shape[-1]
    tm = _pick(s, 512)
    tn = _pick(d, 1024)
    act = lambda w: pl.BlockSpec((None, tm, w), lambda bi, i, j: (bi, i, 0))
    wgt = lambda w: pl.BlockSpec((w, tn), lambda bi, i, j: (0, j))
    return pl.pallas_call(
        _outproj_kernel,
        grid=(b, s // tm, d // tn),
        in_specs=[act(cw), act(fw), act(lw), pl.BlockSpec((1, fw), lambda bi, i, j: (0, 0)),
                  wgt(cw), wgt(fw), wgt(lw),
                  pl.BlockSpec((None, tm, tn), lambda bi, i, j: (bi, i, j)),
                  pl.BlockSpec((None, 1, tn), lambda bi, i, j: (bi, 0, j))],
        out_specs=pl.BlockSpec((None, tm, tn), lambda bi, i, j: (bi, i, j)),
        out_shape=jax.ShapeDtypeStruct((b, s, d), F32),
        scratch_shapes=[pltpu.VMEM((tm, fw), BF16)],
        compiler_params=_params("parallel", "parallel", "arbitrary"),
        name="outproj",
    )(y_conv, o_fox, y_lru, fox_g.reshape(1, fw), w1, w2, w3, x, g1)


def _store_token_major(ref, val):
    n, d = val.shape
    g = d // LANES
    for j in range(g):
        ref[pl.ds(j, n, stride=g), :] = val[:, j * LANES:(j + 1) * LANES]


def _load_token_major(ref, n, g, lead=()):
    return jnp.concatenate([ref[lead + (pl.ds(j, n, stride=g), slice(None))] for j in range(g)], axis=1)


def _router_kernel(x_ref, g_ref, sc_ref, sh_ref, whi_ref, wlo_ref, rb_ref, h2_ref, ri_ref, rg_ref, cnt_ref, car,
                   *, n_groups, per_group):
    @pl.when(jnp.logical_and(pl.program_id(0) == 0, pl.program_id(1) == 0))
    def _():
        car[...] = jnp.zeros_like(car)

    h2 = _modulated_norm(x_ref[...], g_ref[...], sc_ref[...], sh_ref[...])
    _store_token_major(h2_ref, h2)
    hi = h2.astype(BF16)
    lo = (h2 - hi.astype(F32)).astype(BF16)
    logits = (jnp.dot(hi, whi_ref[...], preferred_element_type=F32)
              + jnp.dot(lo, whi_ref[...], preferred_element_type=F32)
              + jnp.dot(hi, wlo_ref[...], preferred_element_type=F32)) + rb_ref[...]
    lane_i = lax.broadcasted_iota(jnp.int32, logits.shape, 1)
    lane = lane_i.astype(F32)
    big = float(LANES)

    def first_lane(hit):
        return jnp.min(jnp.where(hit, lane, big), axis=-1, keepdims=True)

    gmask = lane_i < n_groups
    lg1 = jnp.where(gmask, logits, -jnp.inf)
    m1 = jnp.max(lg1, axis=-1, keepdims=True)
    e1 = jnp.exp(lg1 - m1)
    p1 = e1 / jnp.sum(e1, axis=-1, keepdims=True)
    p_grp = jnp.max(p1, axis=-1, keepdims=True)
    grp = first_lane((p1 == p_grp) & gmask)
    lo_lane = n_groups + grp * per_group
    emask = (lane >= lo_lane) & (lane < lo_lane + per_group)
    lg2 = jnp.where(emask, logits, -jnp.inf)
    m2 = jnp.max(lg2, axis=-1, keepdims=True)
    e2 = jnp.exp(lg2 - m2)
    p2 = e2 / jnp.sum(e2, axis=-1, keepdims=True)
    v1 = jnp.max(jnp.where(emask, p2, -1.0), axis=-1, keepdims=True)
    i1 = first_lane((p2 == v1) & emask)
    rest = emask & (lane != i1)
    v2 = jnp.max(jnp.where(rest, p2, -1.0), axis=-1, keepdims=True)
    i2 = first_lane((p2 == v2) & rest)
    denom = v1 + v2
    gate1 = p_grp * (v1 / denom)
    gate2 = p_grp * (v2 / denom)
    tm = logits.shape[0]
    picked = jnp.where((lane == i1) | (lane == i2), 1.0, 0.0)
    tri = jnp.where(lax.broadcasted_iota(jnp.int32, (tm, tm), 0) > lax.broadcasted_iota(jnp.int32, (tm, tm), 1),
                    1.0, 0.0).astype(BF16)
    before = jnp.dot(tri, picked.astype(BF16), preferred_element_type=F32) + car[...]
    rank1 = jnp.sum(jnp.where(lane == i1, before, 0.0), axis=-1, keepdims=True)
    rank2 = jnp.sum(jnp.where(lane == i2, before, 0.0), axis=-1, keepdims=True)
    total = car[...] + jnp.sum(picked, axis=0, keepdims=True)
    car[...] = total
    cnt_ref[...] = total
    info = jnp.where(lane_i == 0, i1 - n_groups, jnp.where(lane_i == 1, i2 - n_groups,
                     jnp.where(lane_i == 2, rank1, jnp.where(lane_i == 3, rank2, 0.0))))
    ri_ref[...] = info.astype(jnp.int32)
    rg_ref[...] = jnp.where(lane_i == 0, gate1, jnp.where(lane_i == 1, gate2, 0.0))


def _router(x, g, sc, sh, w_hi, w_lo, r_bias, n_groups, per_group):
    b, s, d = x.shape
    tm = _pick(s, 512)
    kern = functools.partial(_router_kernel, n_groups=n_groups, per_group=per_group)
    row = lambda w: pl.BlockSpec((None, tm, w), lambda bi, i: (bi, i, 0))
    mod = pl.BlockSpec((None, 1, d), lambda bi, i: (bi, 0, 0))
    wsp = pl.BlockSpec((d, LANES), lambda bi, i: (0, 0))
    return pl.pallas_call(
        kern,
        grid=(b, s // tm),
        in_specs=[row(d), pl.BlockSpec((1, d), lambda bi, i: (0, 0)), mod, mod, wsp, wsp,
                  pl.BlockSpec((1, LANES), lambda bi, i: (0, 0))],
        out_specs=[pl.BlockSpec((None, tm * (d // LANES), LANES), lambda bi, i: (bi, i, 0)), row(LANES), row(LANES),
                   pl.BlockSpec((1, LANES), lambda bi, i: (0, 0))],
        out_shape=[jax.ShapeDtypeStruct((b, s * (d // LANES), LANES), F32),
                   jax.ShapeDtypeStruct((b, s, LANES), jnp.int32),
                   jax.ShapeDtypeStruct((b, s, LANES), F32), jax.ShapeDtypeStruct((1, LANES), F32)],
        scratch_shapes=[pltpu.VMEM((1, LANES), F32)],
        compiler_params=_params("arbitrary", "arbitrary"),
        name="router",
    )(x, g, sc, sh, w_hi, w_lo, r_bias)


def _moe_kernel(bexp_ref, rtok_ref, h2_hbm, wg_ref, wu_ref, wd_ref, y_ref, xbuf, sem, wg_bf, wu_bf, wd_bf, *, rows, g):
    b = pl.program_id(0)
    nb = pl.num_programs(0)
    slot = b % 2

    def row_copy(tok, r, slot_):
        src = h2_hbm.at[pl.ds(pl.multiple_of(tok * g, g), g)]
        dst = xbuf.at[slot_, pl.ds(pl.multiple_of(r * g, g), g)]
        return pltpu.make_async_copy(src, dst, sem.at[slot_])

    def wait_block(slot_):
        pltpu.make_async_copy(h2_hbm.at[pl.ds(0, rows * g)], xbuf.at[slot_], sem.at[slot_]).wait()

    @pl.when(b == 0)
    def _():
        def body(r, carry):
            row_copy(rtok_ref[r], r, 0).start()
            return carry

        lax.fori_loop(0, rows, body, 0)

    changed = jnp.logical_or(b == 0, bexp_ref[b] != bexp_ref[jnp.maximum(b - 1, 0)])

    @pl.when(changed)
    def _():
        wg_bf[...] = wg_ref[...].astype(BF16)
        wu_bf[...] = wu_ref[...].astype(BF16)
        wd_bf[...] = wd_ref[...].astype(BF16)

    wait_block(slot)
    base = jnp.minimum(b + 1, nb - 1) * rows
    for r in range(rows):
        row_copy(rtok_ref[base + r], r, 1 - slot).start()
    xb = _load_token_major(xbuf, rows, g, lead=(slot,)).astype(BF16)
    gt = jnp.dot(xb, wg_bf[...], preferred_element_type=F32)
    up = jnp.dot(xb, wu_bf[...], preferred_element_type=F32)
    hb = (gt * _sigmoid(gt) * up).astype(BF16)
    _store_token_major(y_ref, jnp.dot(hb, wd_bf[...], preferred_element_type=F32))

    @pl.when(b == nb - 1)
    def _():
        wait_block(1 - slot)


def _moe_experts(block_exp, row_tok, h2_rows, w_gate, w_up, w_down, layer):
    _, _, d, f = w_gate.shape
    rows = MOE_ROWS
    g = d // LANES
    n_blocks = block_exp.shape[0]
    grid_spec = pltpu.PrefetchScalarGridSpec(
        num_scalar_prefetch=2,
        grid=(n_blocks,),
        in_specs=[
            pl.BlockSpec(memory_space=pl.ANY),
            pl.BlockSpec((None, None, d, f), lambda bi, be, rt: (layer, be[bi], 0, 0)),
            pl.BlockSpec((None, None, d, f), lambda bi, be, rt: (layer, be[bi], 0, 0)),
            pl.BlockSpec((None, None, f, d), lambda bi, be, rt: (layer, be[bi], 0, 0)),
        ],
        out_specs=pl.BlockSpec((rows * g, LANES), lambda bi, be, rt: (bi, 0)),
        scratch_shapes=[pltpu.VMEM((2, rows * g, LANES), F32), pltpu.SemaphoreType.DMA((2,)),
                        pltpu.VMEM((d, f), BF16), pltpu.VMEM((d, f), BF16), pltpu.VMEM((f, d), BF16)],
    )
    return pl.pallas_call(
        functools.partial(_moe_kernel, rows=rows, g=g),
        grid_spec=grid_spec,
        out_shape=jax.ShapeDtypeStruct((n_blocks * rows * g, LANES), F32),
        compiler_params=_params("arbitrary"),
        name="moe_experts",
    )(block_exp, row_tok, h2_rows, w_gate, w_up, w_down)


def _combine_kernel(dest_ref, y_hbm, x_ref, rg_ref, g2_ref, out_ref, ybuf, sem, *, tm, g):
    i = pl.program_id(0)
    n = pl.num_programs(0)
    slot = i % 2

    def start(row, r, k, slot_):
        src = y_hbm.at[pl.ds(pl.multiple_of(row * g, g), g)]
        dst = ybuf.at[slot_, k, pl.ds(pl.multiple_of(r * g, g), g)]
        pltpu.make_async_copy(src, dst, sem.at[slot_]).start()

    def wait_tile(slot_):
        for k in range(TOP_K):
            pltpu.make_async_copy(y_hbm.at[pl.ds(0, tm * g)], ybuf.at[slot_, k], sem.at[slot_]).wait()

    @pl.when(i == 0)
    def _():
        def body(r, carry):
            for k in range(TOP_K):
                start(dest_ref[r * TOP_K + k], r, k, 0)
            return carry

        lax.fori_loop(0, tm, body, 0)

    wait_tile(slot)
    base = jnp.minimum(i + 1, n - 1) * (tm * TOP_K)
    for r in range(tm):
        for k in range(TOP_K):
            start(dest_ref[base + r * TOP_K + k], r, k, 1 - slot)
    gates = rg_ref[...]
    gate = [jnp.broadcast_to(gates[:, k:k + 1], (tm, LANES)) for k in range(TOP_K)]
    for j in range(g):
        cols = slice(j * LANES, (j + 1) * LANES)
        moe = ybuf[slot, 0, pl.ds(j, tm, stride=g), :] * gate[0]
        for k in range(1, TOP_K):
            moe = moe + ybuf[slot, k, pl.ds(j, tm, stride=g), :] * gate[k]
        out_ref[:, cols] = x_ref[:, cols] + g2_ref[:, cols] * moe

    @pl.when(i == n - 1)
    def _():
        wait_tile(1 - slot)


def _combine(dest, y, x_flat, rg_flat, g2, seq):
    t, d = x_flat.shape
    tm = _pick(seq, 128)
    grid_spec = pltpu.PrefetchScalarGridSpec(
        num_scalar_prefetch=1,
        grid=(t // tm,),
        in_specs=[
            pl.BlockSpec(memory_space=pl.ANY),
            pl.BlockSpec((tm, d), lambda i, de: (i, 0)),
            pl.BlockSpec((tm, LANES), lambda i, de: (i, 0)),
            pl.BlockSpec((None, 1, d), lambda i, de: ((i * tm) // seq, 0, 0)),
        ],
        out_specs=pl.BlockSpec((tm, d), lambda i, de: (i, 0)),
        scratch_shapes=[pltpu.VMEM((2, TOP_K, tm * (d // LANES), LANES), F32), pltpu.SemaphoreType.DMA((2,))],
    )
    return pl.pallas_call(
        functools.partial(_combine_kernel, tm=tm, g=d // LANES),
        grid_spec=grid_spec,
        out_shape=jax.ShapeDtypeStruct((t, d), F32),
        compiler_params=_params("arbitrary"),
        name="moe_combine",
    )(dest, y, x_flat, rg_flat, g2)


def _final_norm_kernel(x_ref, g_ref, o_ref):
    x = x_ref[...]
    o_ref[...] = x * lax.rsqrt(jnp.mean(x * x, axis=-1, keepdims=True) + EPS) * g_ref[...]


def _final_norm(x_flat, g):
    t, d = x_flat.shape
    tm = _pick(t, 512)
    return pl.pallas_call(
        _final_norm_kernel,
        grid=(t // tm,),
        in_specs=[pl.BlockSpec((tm, d), lambda i: (i, 0)), pl.BlockSpec((1, d), lambda i: (0, 0))],
        out_specs=pl.BlockSpec((tm, d), lambda i: (i, 0)),
        out_shape=jax.ShapeDtypeStruct((t, d), F32),
        compiler_params=_params("parallel"),
        name="final_norm",
    )(x_flat, g.reshape(1, d))


def _routing_tables(eid, rank, counts, rows):
    n_assign = eid.shape[0]
    n_experts = counts.shape[0]
    padded = (counts + rows - 1) // rows * rows
    pends = jnp.cumsum(padded)
    pstarts = pends - padded
    onehot = eid[:, None] == jnp.arange(n_experts, dtype=jnp.int32)[None, :]
    dest = (jnp.sum(jnp.where(onehot, pstarts[None, :], 0), axis=1) + rank).astype(jnp.int32)
    n_blocks = -(-n_assign // rows) + n_experts
    row_tok = jnp.zeros((n_blocks * rows,), jnp.int32).at[dest].set(jnp.arange(n_assign, dtype=jnp.int32) // TOP_K)
    first_row = jnp.arange(n_blocks, dtype=jnp.int32) * rows
    block_exp = jnp.minimum(jnp.sum((pends[None, :] <= first_row[:, None]).astype(jnp.int32), axis=1), n_experts - 1)
    return block_exp.astype(jnp.int32), row_tok, dest


def _block_diag(w):
    nb, bw, _ = w.shape
    eye = jnp.eye(nb, dtype=w.dtype)
    return (eye[:, None, :, None] * w[:, :, None, :]).reshape(nb * bw, nb * bw)


def kernel(x, c, w_ada, b_ada, ln1_g, w_in, conv_dw_w, conv_dw_b, conv_ln_g, conv_ln_b, fox_f_bias, fox_out_g, lru_conv_w, lru_conv_b, lru_w_r, lru_b_r, lru_w_i, lru_b_i, lru_lambda, lru_out_g, w_out, ln2_g, w_router_group, b_router_group, w_router_expert, b_router_expert, w_gate, w_up, w_down, final_g):
    b, s, d = x.shape
    depth = w_ada.shape[0]
    cw = conv_dw_b.shape[-1]
    fw = fox_out_g.shape[-1]
    lw = lru_lambda.shape[-1]
    heads = fox_f_bias.shape[-1]
    hd = fw // heads
    n_groups, per_group = b_router_expert.shape[1:]
    n_experts = n_groups * per_group
    off_f = 2 * cw + 3 * fw
    assert cw % LANES == 0 and lw == cw and fw % cw == 0 and hd == LANES and heads <= LANES
    assert n_groups + n_experts <= LANES

    c_pad = jnp.zeros((8, d), F32).at[:b].set(c)
    mod_all = _ada_mod(c_pad, w_ada, b_ada)

    for l in range(depth):
        mod = mod_all[l, :b].reshape(b, 1, 6 * d)
        sh1, sc1, g1, sh2, sc2, g2 = [mod[..., k * d:(k + 1) * d] for k in range(6)]

        w_l = w_in[l]
        w_main = jnp.concatenate([w_l[:, :off_f], w_l[:, off_f + heads:]], axis=1).astype(BF16)
        w_f = jnp.zeros((d, LANES), BF16).at[:, :heads].set(w_l[:, off_f:off_f + heads].astype(BF16))
        z, zf = _inproj(x, ln1_g[l].reshape(1, d), sc1, sh1, w_main, w_f)

        y_conv = _conv_branch(z, cw, conv_dw_w[l], conv_dw_b[l], conv_ln_g[l], conv_ln_b[l])

        f_bias = jnp.zeros((1, LANES), F32).at[0, :heads].set(fox_f_bias[l])
        cum = _forget_cumsum(zf, f_bias)
        o_fox = _attention(z, cum, heads, hd, 2 * cw // hd, (2 * cw + fw) // hd, (2 * cw + 2 * fw) // hd)

        y_lru = _lru_branch(z, lw, off_f // lw, lru_conv_w[l], lru_conv_b[l],
                            _block_diag(lru_w_r[l]).astype(BF16), lru_b_r[l],
                            _block_diag(lru_w_i[l]).astype(BF16), lru_b_i[l], lru_lambda[l], lru_out_g[l])

        wo = w_out[l].astype(BF16)
        x = _outproj(y_conv, o_fox, y_lru, fox_out_g[l], wo[:cw], wo[cw:cw + fw], wo[cw + fw:], x, g1)

        w_r = jnp.concatenate(
            [w_router_group[l], w_router_expert[l].transpose(1, 0, 2).reshape(d, n_experts)], axis=1)
        w_r = jnp.zeros((d, LANES), F32).at[:, :n_groups + n_experts].set(w_r)
        w_hi = w_r.astype(BF16)
        w_lo = (w_r - w_hi.astype(F32)).astype(BF16)
        r_bias = jnp.zeros((1, LANES), F32).at[0, :n_groups + n_experts].set(
            jnp.concatenate([b_router_group[l], b_router_expert[l].reshape(-1)]))
        h2, ri, rg, cnt = _router(x, ln2_g[l].reshape(1, d), sc2, sh2, w_hi, w_lo, r_bias, n_groups, per_group)

        eid = ri[:, :, :TOP_K].reshape(-1)
        rank = ri[:, :, TOP_K:2 * TOP_K].reshape(-1)
        counts = cnt[0, n_groups:n_groups + n_experts].astype(jnp.int32)
        block_exp, row_tok, dest = _routing_tables(eid, rank, counts, MOE_ROWS)
        y = _moe_experts(block_exp, row_tok, h2.reshape(b * s * (d // LANES), LANES), w_gate, w_up, w_down, l)
        x = _combine(dest, y, x.reshape(b * s, d), rg.reshape(b * s, LANES), g2, s).reshape(b, s, d)

    return _final_norm(x.reshape(b * s, d), final_g).reshape(b, s, d)
```

```python
import functools

import jax
import jax.numpy as jnp
from jax import lax
from jax.experimental import pallas as pl
from jax.experimental.pallas import tpu as pltpu

EPS = 1e-6
LRU_C = 8.0
TOP_K = 2
LANES = 128
MOE_ROWS = 128
VMEM_LIMIT = 56 * 1024 * 1024

F32 = jnp.float32
BF16 = jnp.bfloat16


def _pick(n, pref):
    if n <= pref:
        return n
    for t in range(pref, 7, -1):
        if n % t == 0 and t % 8 == 0:
            return t
    raise ValueError((n, pref))


def _params(*sem):
    return pltpu.CompilerParams(dimension_semantics=sem, vmem_limit_bytes=VMEM_LIMIT)


def _sigmoid(x):
    return jax.nn.sigmoid(x)


def _ada_kernel(c_ref, w_ref, b_ref, o_ref):
    c = c_ref[...]
    cond = (c * _sigmoid(c)).astype(BF16)
    o_ref[...] = jnp.dot(cond, w_ref[...].astype(BF16), preferred_element_type=F32) + b_ref[...]


def _ada_mod(c_pad, w_ada, b_ada):
    depth, d, n = w_ada.shape
    rows = c_pad.shape[0]
    tn = _pick(n, 1024)
    return pl.pallas_call(
        _ada_kernel,
        grid=(depth, n // tn),
        in_specs=[
            pl.BlockSpec((rows, d), lambda l, j: (0, 0)),
            pl.BlockSpec((None, d, tn), lambda l, j: (l, 0, j)),
            pl.BlockSpec((None, 1, tn), lambda l, j: (l, 0, j)),
        ],
        out_specs=pl.BlockSpec((None, rows, tn), lambda l, j: (l, 0, j)),
        out_shape=jax.ShapeDtypeStruct((depth, rows, n), F32),
        compiler_params=_params("parallel", "parallel"),
        name="ada_mod",
    )(c_pad, w_ada, b_ada.reshape(depth, 1, n))


def _modulated_norm(x, g, sc, sh):
    y = x * lax.rsqrt(jnp.mean(x * x, axis=-1, keepdims=True) + EPS) * g
    return y * (1.0 + sc) + sh


def _inproj_kernel(x_ref, g_ref, sc_ref, sh_ref, w_ref, wf_ref, z_ref, zf_ref, h_scr):
    @pl.when(pl.program_id(2) == 0)
    def _():
        hb = _modulated_norm(x_ref[...], g_ref[...], sc_ref[...], sh_ref[...]).astype(BF16)
        h_scr[...] = hb
        zf_ref[...] = jnp.dot(hb, wf_ref[...], preferred_element_type=F32)

    z_ref[...] = jnp.dot(h_scr[...], w_ref[...], preferred_element_type=F32)


def _inproj(x, g, sc, sh, w_main, w_f):
    b, s, d = x.shape
    n = w_main.shape[1]
    tm = _pick(s, 1024)
    tn = _pick(n, 1024)
    return pl.pallas_call(
        _inproj_kernel,
        grid=(b, s // tm, n // tn),
        in_specs=[
            pl.BlockSpec((None, tm, d), lambda bi, i, j: (bi, i, 0)),
            pl.BlockSpec((1, d), lambda bi, i, j: (0, 0)),
            pl.BlockSpec((None, 1, d), lambda bi, i, j: (bi, 0, 0)),
            pl.BlockSpec((None, 1, d), lambda bi, i, j: (bi, 0, 0)),
            pl.BlockSpec((d, tn), lambda bi, i, j: (0, j)),
            pl.BlockSpec((d, LANES), lambda bi, i, j: (0, 0)),
        ],
        out_specs=[
            pl.BlockSpec((None, tm, tn), lambda bi, i, j: (bi, i, j)),
            pl.BlockSpec((None, tm, LANES), lambda bi, i, j: (bi, i, 0)),
        ],
        out_shape=[jax.ShapeDtypeStruct((b, s, n), F32), jax.ShapeDtypeStruct((b, s, LANES), F32)],
        scratch_shapes=[pltpu.VMEM((tm, d), BF16)],
        compiler_params=_params("parallel", "parallel", "arbitrary"),
        name="inproj",
    )(x, g, sc, sh, w_main, w_f)


def _conv_kernel(a_ref, g_ref, ah_ref, gh_ref, w_ref, b_ref, lg_ref, lb_ref, o_ref, buf, ybuf, shifted,
                 *, taps, ts, halo, rc):
    cw = a_ref.shape[-1]
    buf[halo:halo + ts, :] = a_ref[...] * _sigmoid(g_ref[...])
    prev = ah_ref[...] * _sigmoid(gh_ref[...])
    buf[0:halo, :] = jnp.where(pl.program_id(1) > 0, prev, 0.0)
    first = halo - (taps - 1)
    for c in range(cw // LANES):
        cs = slice(c * LANES, (c + 1) * LANES)
        for r in range(ts // rc):
            acc = b_ref[:, cs]
            for rem in range(8):
                group = [(k, first + k - rem) for k in range(taps) if (first + k) % 8 == rem]
                if not group:
                    continue
                lo = min(a for _, a in group)
                hi = max(a for _, a in group)
                shifted[0:hi - lo + rc, :] = buf[r * rc + rem + lo:r * rc + rem + hi + rc, cs]
                for k, a in group:
                    acc = acc + w_ref[k:k + 1, cs] * shifted[a - lo:a - lo + rc, :]
            ybuf[r * rc:(r + 1) * rc, cs] = acc
    y = ybuf[...]
    mu = jnp.mean(y, axis=-1, keepdims=True)
    yc = y - mu
    var = jnp.mean(yc * yc, axis=-1, keepdims=True)
    yn = yc * lax.rsqrt(var + EPS) * lg_ref[...] + lb_ref[...]
    o_ref[...] = (yn * _sigmoid(yn)).astype(o_ref.dtype)


def _conv_branch(z, cw, dw_w, dw_b, ln_g, ln_b):
    b, s, _ = z.shape
    taps = dw_w.shape[0]
    halo = 32
    assert taps - 1 <= halo
    ts = _pick(s, 256)
    rc = _pick(ts, 128)
    hb = ts // halo
    kern = functools.partial(_conv_kernel, taps=taps, ts=ts, halo=halo, rc=rc)
    cur = lambda col: pl.BlockSpec((None, ts, cw), lambda bi, i: (bi, i, col))
    prev = lambda col: pl.BlockSpec((None, halo, cw), lambda bi, i: (bi, jnp.maximum(i * hb - 1, 0), col))
    vec = pl.BlockSpec((1, cw), lambda bi, i: (0, 0))
    return pl.pallas_call(
        kern,
        grid=(b, s // ts),
        in_specs=[cur(0), cur(1), prev(0), prev(1),
                  pl.BlockSpec((taps, cw), lambda bi, i: (0, 0)), vec, vec, vec],
        out_specs=pl.BlockSpec((None, ts, cw), lambda bi, i: (bi, i, 0)),
        out_shape=jax.ShapeDtypeStruct((b, s, cw), BF16),
        scratch_shapes=[pltpu.VMEM((halo + ts, cw), F32), pltpu.VMEM((ts, cw), F32),
                        pltpu.VMEM((rc + halo, LANES), F32)],
        compiler_params=_params("parallel", "parallel"),
        name="conv_branch",
    )(z, z, z, z, dw_w, dw_b.reshape(1, cw), ln_g.reshape(1, cw), ln_b.reshape(1, cw))


def _shift_rows(v, d, fill):
    rows = lax.broadcasted_iota(jnp.int32, v.shape, 0)
    return jnp.where(rows >= d, pltpu.roll(v, d, axis=0), fill)


def _scan_linear(a, bv):
    d = 1
    while d < a.shape[0]:
        bv = a * _shift_rows(bv, d, 0.0) + bv
        a = a * _shift_rows(a, d, 1.0)
        d *= 2
    return a, bv


def _scan_sum(v):
    d = 1
    while d < v.shape[0]:
        v = v + _shift_rows(v, d, 0.0)
        d *= 2
    return v


def _cum_kernel(zf_ref, fb_ref, o_ref, car, *, ts, rs):
    @pl.when(pl.program_id(1) == 0)
    def _():
        car[...] = jnp.zeros_like(car)

    h = car[...]
    for r in range(ts // rs):
        x = zf_ref[r * rs:(r + 1) * rs, :] + fb_ref[...]
        log_f = jnp.minimum(x, 0.0) - jnp.log1p(jnp.exp(-jnp.abs(x)))
        v = _scan_sum(log_f) + h
        o_ref[r * rs:(r + 1) * rs, :] = v
        h = v[rs - 1:rs, :]
    car[...] = h


def _forget_cumsum(zf, f_bias_pad):
    b, s, _ = zf.shape
    ts = _pick(s, 512)
    rs = _pick(ts, 128)
    return pl.pallas_call(
        functools.partial(_cum_kernel, ts=ts, rs=rs),
        grid=(b, s // ts),
        in_specs=[pl.BlockSpec((None, ts, LANES), lambda bi, i: (bi, i, 0)),
                  pl.BlockSpec((1, LANES), lambda bi, i: (0, 0))],
        out_specs=pl.BlockSpec((None, ts, LANES), lambda bi, i: (bi, i, 0)),
        out_shape=jax.ShapeDtypeStruct((b, s, LANES), F32),
        scratch_shapes=[pltpu.VMEM((1, LANES), F32)],
        compiler_params=_params("parallel", "arbitrary"),
        name="forget_cumsum",
    )(zf, f_bias_pad)


def _gelu_tanh(x):
    return 0.5 * x * (1.0 + jnp.tanh(0.7978845608028654 * (x + 0.044715 * (x * x * x))))


def _lru_kernel(u_ref, uh_ref, gate_ref, cw_ref, cb_ref, wr_ref, br_ref, wi_ref, bi_ref, lam_ref, og_ref,
                o_ref, ubuf, abuf, bbuf, car, *, taps, ts, halo, rs):
    lw = u_ref.shape[-1]

    @pl.when(pl.program_id(1) == 0)
    def _():
        car[...] = jnp.zeros_like(car)

    ubuf[halo:halo + ts, :] = u_ref[...]
    ubuf[0:halo, :] = jnp.where(pl.program_id(1) > 0, uh_ref[...], 0.0)
    xc = cb_ref[...]
    for k in range(taps):
        off = halo - (taps - 1) + k
        xc = xc + cw_ref[k:k + 1, :] * ubuf[off:off + ts, :]
    xb = xc.astype(BF16)
    r = _sigmoid(jnp.dot(xb, wr_ref[...], preferred_element_type=F32) + br_ref[...])
    ig = _sigmoid(jnp.dot(xb, wi_ref[...], preferred_element_type=F32) + bi_ref[...])
    nl = -lam_ref[...]
    softplus = jnp.maximum(nl, 0.0) + jnp.log1p(jnp.exp(-jnp.abs(nl)))
    log_a = (-LRU_C) * r * softplus
    th = jnp.tanh(log_a)
    abuf[...] = jnp.exp(log_a)
    bbuf[...] = jnp.sqrt(-2.0 * th / (1.0 - th)) * (ig * xc)
    for c in range(lw // LANES):
        cs = slice(c * LANES, (c + 1) * LANES)
        h = car[:, cs]
        for q in range(ts // rs):
            rows = slice(q * rs, (q + 1) * rs)
            a_cum, b_cum = _scan_linear(abuf[rows, cs], bbuf[rows, cs])
            hv = b_cum + a_cum * h
            bbuf[rows, cs] = hv
            h = hv[rs - 1:rs, :]
        car[:, cs] = h
    y = bbuf[...] * _gelu_tanh(gate_ref[...])
    y = y * lax.rsqrt(jnp.mean(y * y, axis=-1, keepdims=True) + EPS) * og_ref[...]
    o_ref[...] = y.astype(o_ref.dtype)


def _lru_branch(z, lw, col_x, conv_w, conv_b, wr_bd, b_r, wi_bd, b_i, lam, out_g):
    b, s, _ = z.shape
    taps = conv_w.shape[0]
    halo = 8
    assert taps - 1 <= halo
    ts = _pick(s, 256)
    rs = _pick(ts, 128)
    hb = ts // halo
    kern = functools.partial(_lru_kernel, taps=taps, ts=ts, halo=halo, rs=rs)
    vec = pl.BlockSpec((1, lw), lambda bi, i: (0, 0))
    mat = pl.BlockSpec((lw, lw), lambda bi, i: (0, 0))
    return pl.pallas_call(
        kern,
        grid=(b, s // ts),
        in_specs=[
            pl.BlockSpec((None, ts, lw), lambda bi, i: (bi, i, col_x)),
            pl.BlockSpec((None, halo, lw), lambda bi, i: (bi, jnp.maximum(i * hb - 1, 0), col_x)),
            pl.BlockSpec((None, ts, lw), lambda bi, i: (bi, i, col_x + 1)),
            pl.BlockSpec((taps, lw), lambda bi, i: (0, 0)), vec, mat, vec, mat, vec, vec, vec,
        ],
        out_specs=pl.BlockSpec((None, ts, lw), lambda bi, i: (bi, i, 0)),
        out_shape=jax.ShapeDtypeStruct((b, s, lw), BF16),
        scratch_shapes=[pltpu.VMEM((halo + ts, lw), F32), pltpu.VMEM((ts, lw), F32),
                        pltpu.VMEM((ts, lw), F32), pltpu.VMEM((1, lw), F32)],
        compiler_params=_params("parallel", "arbitrary"),
        name="lru_branch",
    )(z, z, z, conv_w, conv_b.reshape(1, lw), wr_bd, b_r.reshape(1, lw), wi_bd, b_i.reshape(1, lw),
      lam.reshape(1, lw), out_g.reshape(1, lw))


def _split3(t):
    hi = t.astype(BF16).astype(F32)
    r = t - hi
    mid = r.astype(BF16).astype(F32)
    return hi, mid, r - mid


def _attn_kernel(q_ref, k_ref, v_ref, cum_ref, o_ref, kaug, vaug, qaug, s_a, s_b, p_a, p_b, al_a, al_b, m_ref, acc_ref,
                 *, tq, tk, scale):
    h = pl.program_id(1)
    i = pl.program_id(2)
    s_len, hd = k_ref.shape
    inv_scale = 1.0 / scale
    c_exp = scale * 1.4426950408889634
    s_buf, p_buf, al_buf = (s_a, s_b), (p_a, p_b), (al_a, al_b)

    def head_cum(start, size):
        lane = lax.broadcasted_iota(jnp.int32, (size, LANES), 1)
        rows = pl.ds(pl.multiple_of(start, size), size)
        return jnp.sum(jnp.where(lane == h, cum_ref[rows, :], 0.0), axis=-1, keepdims=True) * inv_scale

    def key_rows(j):
        return pl.ds(pl.multiple_of(j * tk, tk), tk)

    @pl.when(i == 0)
    def _():
        lane = lax.broadcasted_iota(jnp.int32, (tk, LANES), 1)

        def body(t, carry):
            rows = key_rows(t)
            hi, mid, lo = _split3(-head_cum(t * tk, tk))
            aug = jnp.where(lane < 3, 1.0, jnp.where(lane == 3, hi, jnp.where(lane == 4, mid,
                            jnp.where(lane == 5, lo, 0.0))))
            kaug[t, 0:hd, :] = k_ref[rows, :].T.astype(BF16)
            kaug[t, hd:2 * hd, :] = aug.T.astype(BF16)
            vaug[rows, 0:hd] = v_ref[rows, :].astype(BF16)
            vaug[rows, hd:2 * hd] = jnp.where(lane == 0, 1.0, 0.0).astype(BF16)
            return carry

        lax.fori_loop(0, s_len // tk, body, 0)

    lane_q = lax.broadcasted_iota(jnp.int32, (tq, LANES), 1)
    hi, mid, lo = _split3(head_cum(i * tq, tq))
    qa = jnp.where(lane_q == 0, hi, jnp.where(lane_q == 1, mid, jnp.where(lane_q == 2, lo,
                   jnp.where(lane_q < 6, 1.0, 0.0))))
    qaug[:, 0:hd] = q_ref[...].astype(BF16)
    qaug[:, hd:2 * hd] = qa.astype(BF16)

    def scores_into(dst, j, masked=False):
        s = jnp.dot(qaug[...], kaug[j], preferred_element_type=F32)
        if masked:
            row = lax.broadcasted_iota(jnp.int32, s.shape, 0)
            col = lax.broadcasted_iota(jnp.int32, s.shape, 1)
            s = jnp.where(col <= row, s, -jnp.inf)
        dst[...] = s

    def softmax_into(src, dst_p, dst_alpha):
        s = src[...]
        m_old = m_ref[...]
        m_new = jnp.maximum(m_old, jnp.max(s, axis=-1, keepdims=True))
        dst_alpha[...] = jnp.exp2((m_old - m_new) * c_exp)
        m_ref[...] = m_new
        dst_p[...] = jnp.exp2((s - jnp.concatenate([m_new] * (tk // LANES), axis=1)) * c_exp).astype(BF16)

    def accumulate(src_alpha, src_p, j):
        alpha = jnp.concatenate([src_alpha[...]] * (acc_ref.shape[1] // LANES), axis=1)
        acc_ref[...] = alpha * acc_ref[...] + jnp.dot(src_p[...], vaug[key_rows(j), :], preferred_element_type=F32)

    def blk(t):
        return jnp.where(t == 0, i, jnp.maximum(t - 1, 0))

    def tick(t, par):
        scores_into(s_buf[1 - par], t)
        softmax_into(s_buf[par], p_buf[par], al_buf[par])
        accumulate(al_buf[1 - par], p_buf[1 - par], blk(t - 1))

    scores_into(s_buf[0], i, masked=True)
    m_ref[...] = jnp.full(m_ref.shape, -jnp.inf, F32)
    acc_ref[...] = jnp.zeros(acc_ref.shape, F32)
    p_buf[1][...] = jnp.zeros(p_buf[1].shape, BF16)
    al_buf[1][...] = jnp.ones(al_buf[1].shape, F32)

    def body(u, carry):
        tick(2 * u, 0)
        tick(2 * u + 1, 1)
        return carry

    lax.fori_loop(0, i // 2, body, 0)
    odd = i % 2 == 1

    @pl.when(odd)
    def _():
        tick(i - 1, 0)

    def drain(par):
        softmax_into(s_buf[par], p_buf[par], al_buf[par])
        accumulate(al_buf[1 - par], p_buf[1 - par], blk(i - 1))
        accumulate(al_buf[par], p_buf[par], blk(i))

    @pl.when(odd)
    def _():
        drain(1)

    @pl.when(jnp.logical_not(odd))
    def _():
        drain(0)

    o_ref[...] = acc_ref[:, 0:hd] / acc_ref[:, hd:hd + 1]


def _attention(z, cum, heads, hd, col_q, col_k, col_v):
    b, s, _ = z.shape
    tq = _pick(s, 512)
    tk = tq
    kern = functools.partial(_attn_kernel, tq=tq, tk=tk, scale=hd ** -0.5)
    return pl.pallas_call(
        kern,
        grid=(b, heads, s // tq),
        in_specs=[
            pl.BlockSpec((None, tq, hd), lambda bi, h, i: (bi, i, col_q + h)),
            pl.BlockSpec((None, s, hd), lambda bi, h, i: (bi, 0, col_k + h)),
            pl.BlockSpec((None, s, hd), lambda bi, h, i: (bi, 0, col_v + h)),
            pl.BlockSpec((None, s, LANES), lambda bi, h, i: (bi, 0, 0)),
        ],
        out_specs=pl.BlockSpec((None, tq, hd), lambda bi, h, i: (bi, i, h)),
        out_shape=jax.ShapeDtypeStruct((b, s, heads * hd), F32),
        scratch_shapes=[pltpu.VMEM((s // tk, 2 * hd, tk), BF16), pltpu.VMEM((s, 2 * hd), BF16),
                        pltpu.VMEM((tq, 2 * hd), BF16),
                        pltpu.VMEM((tq, tk), F32), pltpu.VMEM((tq, tk), F32),
                        pltpu.VMEM((tq, tk), BF16), pltpu.VMEM((tq, tk), BF16),
                        pltpu.VMEM((tq, LANES), F32), pltpu.VMEM((tq, LANES), F32), pltpu.VMEM((tq, LANES), F32),
                        pltpu.VMEM((tq, 2 * hd), F32)],
        compiler_params=_params("arbitrary", "arbitrary", "arbitrary"),
        name="fox_attention",
    )(z, z, z, cum)


def _outproj_kernel(yc_ref, o_ref, yl_ref, fg_ref, w1_ref, w2_ref, w3_ref, x_ref, g1_ref, out_ref, yf_scr):
    @pl.when(pl.program_id(2) == 0)
    def _():
        o = o_ref[...]
        yf = o * lax.rsqrt(jnp.mean(o * o, axis=-1, keepdims=True) + EPS) * fg_ref[...]
        yf_scr[...] = yf.astype(BF16)

    y = jnp.dot(yc_ref[...], w1_ref[...], preferred_element_type=F32)
    y = y + jnp.dot(yf_scr[...], w2_ref[...], preferred_element_type=F32)
    y = y + jnp.dot(yl_ref[...], w3_ref[...], preferred_element_type=F32)
    out_ref[...] = x_ref[...] + g1_ref[...] * y


def _outproj(y_conv, o_fox, y_lru, fox_g, w1, w2, w3, x, g1):
    b, s, d = x.shape
    cw, fw, lw = y_conv.shape[-1], o_fox.shape[-1], y_lru.shape[-1]
    tm = _pick(s, 512)
    tn = _pick(d, 1024)
    act = lambda w: pl.BlockSpec((None, tm, w), lambda bi, i, j: (bi, i, 0))
    wgt = lambda w: pl.BlockSpec((w, tn), lambda bi, i, j: (0, j))
    return pl.pallas_call(
        _outproj_kernel,
        grid=(b, s // tm, d // tn),
        in_specs=[act(cw), act(fw), act(lw), pl.BlockSpec((1, fw), lambda bi, i, j: (0, 0)),
                  wgt(cw), wgt(fw), wgt(lw),
                  pl.BlockSpec((None, tm, tn), lambda bi, i, j: (bi, i, j)),
                  pl.BlockSpec((None, 1, tn), lambda bi, i, j: (bi, 0, j))],
        out_specs=pl.BlockSpec((None, tm, tn), lambda bi, i, j: (bi, i, j)),
        out_shape=jax.ShapeDtypeStruct((b, s, d), F32),
        scratch_shapes=[pltpu.VMEM((tm, fw), BF16)],
        compiler_params=_params("parallel", "parallel", "arbitrary"),
        name="outproj",
    )(y_conv, o_fox, y_lru, fox_g.reshape(1, fw), w1, w2, w3, x, g1)


def _router_kernel(x_ref, g_ref, sc_ref, sh_ref, whi_ref, wlo_ref, rb_ref, h2_ref, ri_ref, rg_ref, cnt_ref, car,
                   *, n_groups, per_group):
    @pl.when(jnp.logical_and(pl.program_id(0) == 0, pl.program_id(1) == 0))
    def _():
        car[...] = jnp.zeros_like(car)

    h2 = _modulated_norm(x_ref[...], g_ref[...], sc_ref[...], sh_ref[...])
    h2_ref[...] = h2
    hi = h2.astype(BF16)
    lo = (h2 - hi.astype(F32)).astype(BF16)
    logits = (jnp.dot(hi, whi_ref[...], preferred_element_type=F32)
              + jnp.dot(lo, whi_ref[...], preferred_element_type=F32)
              + jnp.dot(hi, wlo_ref[...], preferred_element_type=F32)) + rb_ref[...]
    lane_i = lax.broadcasted_iota(jnp.int32, logits.shape, 1)
    lane = lane_i.astype(F32)
    big = float(LANES)

    def first_lane(hit):
        return jnp.min(jnp.where(hit, lane, big), axis=-1, keepdims=True)

    gmask = lane_i < n_groups
    lg1 = jnp.where(gmask, logits, -jnp.inf)
    m1 = jnp.max(lg1, axis=-1, keepdims=True)
    e1 = jnp.exp(lg1 - m1)
    p1 = e1 / jnp.sum(e1, axis=-1, keepdims=True)
    p_grp = jnp.max(p1, axis=-1, keepdims=True)
    grp = first_lane((p1 == p_grp) & gmask)
    lo_lane = n_groups + grp * per_group
    emask = (lane >= lo_lane) & (lane < lo_lane + per_group)
    lg2 = jnp.where(emask, logits, -jnp.inf)
    m2 = jnp.max(lg2, axis=-1, keepdims=True)
    e2 = jnp.exp(lg2 - m2)
    p2 = e2 / jnp.sum(e2, axis=-1, keepdims=True)
    v1 = jnp.max(jnp.where(emask, p2, -1.0), axis=-1, keepdims=True)
    i1 = first_lane((p2 == v1) & emask)
    rest = emask & (lane != i1)
    v2 = jnp.max(jnp.where(rest, p2, -1.0), axis=-1, keepdims=True)
    i2 = first_lane((p2 == v2) & rest)
    denom = v1 + v2
    gate1 = p_grp * (v1 / denom)
    gate2 = p_grp * (v2 / denom)
    tm = logits.shape[0]
    picked = jnp.where((lane == i1) | (lane == i2), 1.0, 0.0)
    tri = jnp.where(lax.broadcasted_iota(jnp.int32, (tm, tm), 0) > lax.broadcasted_iota(jnp.int32, (tm, tm), 1),
                    1.0, 0.0).astype(BF16)
    before = jnp.dot(tri, picked.astype(BF16), preferred_element_type=F32) + car[...]
    rank1 = jnp.sum(jnp.where(lane == i1, before, 0.0), axis=-1, keepdims=True)
    rank2 = jnp.sum(jnp.where(lane == i2, before, 0.0), axis=-1, keepdims=True)
    total = car[...] + jnp.sum(picked, axis=0, keepdims=True)
    car[...] = total
    cnt_ref[...] = total
    info = jnp.where(lane_i == 0, i1 - n_groups, jnp.where(lane_i == 1, i2 - n_groups,
                     jnp.where(lane_i == 2, rank1, jnp.where(lane_i == 3, rank2, 0.0))))
    ri_ref[...] = info.astype(jnp.int32)
    rg_ref[...] = jnp.where(lane_i == 0, gate1, jnp.where(lane_i == 1, gate2, 0.0))


def _router(x, g, sc, sh, w_hi, w_lo, r_bias, n_groups, per_group):
    b, s, d = x.shape
    tm = _pick(s, 512)
    kern = functools.partial(_router_kernel, n_groups=n_groups, per_group=per_group)
    row = lambda w: pl.BlockSpec((None, tm, w), lambda bi, i: (bi, i, 0))
    mod = pl.BlockSpec((None, 1, d), lambda bi, i: (bi, 0, 0))
    wsp = pl.BlockSpec((d, LANES), lambda bi, i: (0, 0))
    return pl.pallas_call(
        kern,
        grid=(b, s // tm),
        in_specs=[row(d), pl.BlockSpec((1, d), lambda bi, i: (0, 0)), mod, mod, wsp, wsp,
                  pl.BlockSpec((1, LANES), lambda bi, i: (0, 0))],
        out_specs=[row(d), row(LANES), row(LANES), pl.BlockSpec((1, LANES), lambda bi, i: (0, 0))],
        out_shape=[jax.ShapeDtypeStruct((b, s, d), F32), jax.ShapeDtypeStruct((b, s, LANES), jnp.int32),
                   jax.ShapeDtypeStruct((b, s, LANES), F32), jax.ShapeDtypeStruct((1, LANES), F32)],
        scratch_shapes=[pltpu.VMEM((1, LANES), F32)],
        compiler_params=_params("arbitrary", "arbitrary"),
        name="router",
    )(x, g, sc, sh, w_hi, w_lo, r_bias)


def _moe_kernel(bexp_ref, rtok_ref, h2_hbm, wg_ref, wu_ref, wd_ref, y_ref, xbuf_a, xbuf_b, sem, wg_bf, wu_bf, wd_bf,
                *, rows):
    b = pl.program_id(0)
    nb = pl.num_programs(0)
    xbufs = (xbuf_a, xbuf_b)

    def row_copy(tok, r, slot):
        return pltpu.make_async_copy(h2_hbm.at[pl.ds(tok, 1)], xbufs[slot].at[pl.ds(r, 1)], sem.at[slot])

    def wait_block(slot):
        pltpu.make_async_copy(h2_hbm.at[pl.ds(0, rows)], xbufs[slot], sem.at[slot]).wait()

    @pl.when(b == 0)
    def _():
        def body(r, carry):
            row_copy(rtok_ref[r], r, 0).start()
            return carry

        lax.fori_loop(0, rows, body, 0)

    changed = jnp.logical_or(b == 0, bexp_ref[b] != bexp_ref[jnp.maximum(b - 1, 0)])

    @pl.when(changed)
    def _():
        wg_bf[...] = wg_ref[...].astype(BF16)
        wu_bf[...] = wu_ref[...].astype(BF16)
        wd_bf[...] = wd_ref[...].astype(BF16)

    def block(slot):
        wait_block(slot)
        base = jnp.minimum(b + 1, nb - 1) * rows
        for r in range(rows):
            row_copy(rtok_ref[base + r], r, 1 - slot).start()
        xb = xbufs[slot][...].astype(BF16)
        gt = jnp.dot(xb, wg_bf[...], preferred_element_type=F32)
        up = jnp.dot(xb, wu_bf[...], preferred_element_type=F32)
        hb = (gt * _sigmoid(gt) * up).astype(BF16)
        y_ref[...] = jnp.dot(hb, wd_bf[...], preferred_element_type=F32)

        @pl.when(b == nb - 1)
        def _():
            wait_block(1 - slot)

    @pl.when(b % 2 == 0)
    def _():
        block(0)

    @pl.when(b % 2 == 1)
    def _():
        block(1)


def _moe_experts(block_exp, row_tok, h2_rows, w_gate, w_up, w_down, layer):
    _, _, d, f = w_gate.shape
    rows = MOE_ROWS
    n_blocks = block_exp.shape[0]
    grid_spec = pltpu.PrefetchScalarGridSpec(
        num_scalar_prefetch=2,
        grid=(n_blocks,),
        in_specs=[
            pl.BlockSpec(memory_space=pl.ANY),
            pl.BlockSpec((None, None, d, f), lambda bi, be, rt: (layer, be[bi], 0, 0)),
            pl.BlockSpec((None, None, d, f), lambda bi, be, rt: (layer, be[bi], 0, 0)),
            pl.BlockSpec((None, None, f, d), lambda bi, be, rt: (layer, be[bi], 0, 0)),
        ],
        out_specs=pl.BlockSpec((rows, d), lambda bi, be, rt: (bi, 0)),
        scratch_shapes=[pltpu.VMEM((rows, d), F32), pltpu.VMEM((rows, d), F32), pltpu.SemaphoreType.DMA((2,)),
                        pltpu.VMEM((d, f), BF16), pltpu.VMEM((d, f), BF16), pltpu.VMEM((f, d), BF16)],
    )
    return pl.pallas_call(
        functools.partial(_moe_kernel, rows=rows),
        grid_spec=grid_spec,
        out_shape=jax.ShapeDtypeStruct((n_blocks * rows, d), F32),
        compiler_params=_params("arbitrary"),
        name="moe_experts",
    )(block_exp, row_tok, h2_rows, w_gate, w_up, w_down)


def _combine_kernel(dest_ref, y_hbm, x_ref, rg_ref, g2_ref, out_ref, ybuf_a, ybuf_b, sem, *, tm):
    i = pl.program_id(0)
    n = pl.num_programs(0)
    ybufs = (ybuf_a, ybuf_b)

    def start(row, r, k, slot):
        pltpu.make_async_copy(y_hbm.at[pl.ds(row, 1)], ybufs[slot].at[k, pl.ds(r, 1)], sem.at[slot]).start()

    def wait_tile(slot):
        for k in range(TOP_K):
            pltpu.make_async_copy(y_hbm.at[pl.ds(0, tm)], ybufs[slot].at[k], sem.at[slot]).wait()

    @pl.when(i == 0)
    def _():
        def body(r, carry):
            for k in range(TOP_K):
                start(dest_ref[r * TOP_K + k], r, k, 0)
            return carry

        lax.fori_loop(0, tm, body, 0)

    def tile(slot):
        wait_tile(slot)
        base = jnp.minimum(i + 1, n - 1) * (tm * TOP_K)
        for r in range(tm):
            for k in range(TOP_K):
                start(dest_ref[base + r * TOP_K + k], r, k, 1 - slot)
        gates = rg_ref[...]
        gate = [jnp.broadcast_to(gates[:, k:k + 1], (tm, LANES)) for k in range(TOP_K)]
        for j in range(out_ref.shape[1] // LANES):
            cols = slice(j * LANES, (j + 1) * LANES)
            moe = ybufs[slot][0, :, cols] * gate[0]
            for k in range(1, TOP_K):
                moe = moe + ybufs[slot][k, :, cols] * gate[k]
            out_ref[:, cols] = x_ref[:, cols] + g2_ref[:, cols] * moe

        @pl.when(i == n - 1)
        def _():
            wait_tile(1 - slot)

    @pl.when(i % 2 == 0)
    def _():
        tile(0)

    @pl.when(i % 2 == 1)
    def _():
        tile(1)


def _combine(dest, y, x_flat, rg_flat, g2, seq):
    t, d = x_flat.shape
    tm = _pick(seq, 128)
    grid_spec = pltpu.PrefetchScalarGridSpec(
        num_scalar_prefetch=1,
        grid=(t // tm,),
        in_specs=[
            pl.BlockSpec(memory_space=pl.ANY),
            pl.BlockSpec((tm, d), lambda i, de: (i, 0)),
            pl.BlockSpec((tm, LANES), lambda i, de: (i, 0)),
            pl.BlockSpec((None, 1, d), lambda i, de: ((i * tm) // seq, 0, 0)),
        ],
        out_specs=pl.BlockSpec((tm, d), lambda i, de: (i, 0)),
        scratch_shapes=[pltpu.VMEM((TOP_K, tm, d), F32), pltpu.VMEM((TOP_K, tm, d), F32),
                        pltpu.SemaphoreType.DMA((2,))],
    )
    return pl.pallas_call(
        functools.partial(_combine_kernel, tm=tm),
        grid_spec=grid_spec,
        out_shape=jax.ShapeDtypeStruct((t, d), F32),
        compiler_params=_params("arbitrary"),
        name="moe_combine",
    )(dest, y, x_flat, rg_flat, g2)


def _final_norm_kernel(x_ref, g_ref, o_ref):
    x = x_ref[...]
    o_ref[...] = x * lax.rsqrt(jnp.mean(x * x, axis=-1, keepdims=True) + EPS) * g_ref[...]


def _final_norm(x_flat, g):
    t, d = x_flat.shape
    tm = _pick(t, 512)
    return pl.pallas_call(
        _final_norm_kernel,
        grid=(t // tm,),
        in_specs=[pl.BlockSpec((tm, d), lambda i: (i, 0)), pl.BlockSpec((1, d), lambda i: (0, 0))],
        out_specs=pl.BlockSpec((tm, d), lambda i: (i, 0)),
        out_shape=jax.ShapeDtypeStruct((t, d), F32),
        compiler_params=_params("parallel"),
        name="final_norm",
    )(x_flat, g.reshape(1, d))


def _routing_tables(eid, rank, counts, rows):
    n_assign = eid.shape[0]
    n_experts = counts.shape[0]
    padded = (counts + rows - 1) // rows * rows
    pends = jnp.cumsum(padded)
    pstarts = pends - padded
    onehot = eid[:, None] == jnp.arange(n_experts, dtype=jnp.int32)[None, :]
    dest = (jnp.sum(jnp.where(onehot, pstarts[None, :], 0), axis=1) + rank).astype(jnp.int32)
    n_blocks = -(-n_assign // rows) + n_experts
    row_tok = jnp.zeros((n_blocks * rows,), jnp.int32).at[dest].set(jnp.arange(n_assign, dtype=jnp.int32) // TOP_K)
    first_row = jnp.arange(n_blocks, dtype=jnp.int32) * rows
    block_exp = jnp.minimum(jnp.sum((pends[None, :] <= first_row[:, None]).astype(jnp.int32), axis=1), n_experts - 1)
    return block_exp.astype(jnp.int32), row_tok, dest


def _block_diag(w):
    nb, bw, _ = w.shape
    eye = jnp.eye(nb, dtype=w.dtype)
    return (eye[:, None, :, None] * w[:, :, None, :]).reshape(nb * bw, nb * bw)


def kernel(x, c, w_ada, b_ada, ln1_g, w_in, conv_dw_w, conv_dw_b, conv_ln_g, conv_ln_b, fox_f_bias, fox_out_g, lru_conv_w, lru_conv_b, lru_w_r, lru_b_r, lru_w_i, lru_b_i, lru_lambda, lru_out_g, w_out, ln2_g, w_router_group, b_router_group, w_router_expert, b_router_expert, w_gate, w_up, w_down, final_g):
    b, s, d = x.shape
    depth = w_ada.shape[0]
    cw = conv_dw_b.shape[-1]
    fw = fox_out_g.shape[-1]
    lw = lru_lambda.shape[-1]
    heads = fox_f_bias.shape[-1]
    hd = fw // heads
    n_groups, per_group = b_router_expert.shape[1:]
    n_experts = n_groups * per_group
    off_f = 2 * cw + 3 * fw
    assert cw % LANES == 0 and lw == cw and fw % cw == 0 and hd == LANES and heads <= LANES
    assert n_groups + n_experts <= LANES

    c_pad = jnp.zeros((8, d), F32).at[:b].set(c)
    mod_all = _ada_mod(c_pad, w_ada, b_ada)

    for l in range(depth):
        mod = mod_all[l, :b].reshape(b, 1, 6 * d)
        sh1, sc1, g1, sh2, sc2, g2 = [mod[..., k * d:(k + 1) * d] for k in range(6)]

        w_l = w_in[l]
        w_main = jnp.concatenate([w_l[:, :off_f], w_l[:, off_f + heads:]], axis=1).astype(BF16)
        w_f = jnp.zeros((d, LANES), BF16).at[:, :heads].set(w_l[:, off_f:off_f + heads].astype(BF16))
        z, zf = _inproj(x, ln1_g[l].reshape(1, d), sc1, sh1, w_main, w_f)

        y_conv = _conv_branch(z, cw, conv_dw_w[l], conv_dw_b[l], conv_ln_g[l], conv_ln_b[l])

        f_bias = jnp.zeros((1, LANES), F32).at[0, :heads].set(fox_f_bias[l])
        cum = _forget_cumsum(zf, f_bias)
        o_fox = _attention(z, cum, heads, hd, 2 * cw // hd, (2 * cw + fw) // hd, (2 * cw + 2 * fw) // hd)

        y_lru = _lru_branch(z, lw, off_f // lw, lru_conv_w[l], lru_conv_b[l],
                            _block_diag(lru_w_r[l]).astype(BF16), lru_b_r[l],
                            _block_diag(lru_w_i[l]).astype(BF16), lru_b_i[l], lru_lambda[l], lru_out_g[l])

        wo = w_out[l].astype(BF16)
        x = _outproj(y_conv, o_fox, y_lru, fox_out_g[l], wo[:cw], wo[cw:cw + fw], wo[cw + fw:], x, g1)

        w_r = jnp.concatenate(
            [w_router_group[l], w_router_expert[l].transpose(1, 0, 2).reshape(d, n_experts)], axis=1)
        w_r = jnp.zeros((d, LANES), F32).at[:, :n_groups + n_experts].set(w_r)
        w_hi = w_r.astype(BF16)
        w_lo = (w_r - w_hi.astype(F32)).astype(BF16)
        r_bias = jnp.zeros((1, LANES), F32).at[0, :n_groups + n_experts].set(
            jnp.concatenate([b_router_group[l], b_router_expert[l].reshape(-1)]))
        h2, ri, rg, cnt = _router(x, ln2_g[l].reshape(1, d), sc2, sh2, w_hi, w_lo, r_bias, n_groups, per_group)

        eid = ri[:, :, :TOP_K].reshape(-1)
        rank = ri[:, :, TOP_K:2 * TOP_K].reshape(-1)
        counts = cnt[0, n_groups:n_groups + n_experts].astype(jnp.int32)
        block_exp, row_tok, dest = _routing_tables(eid, rank, counts, MOE_ROWS)
        y = _moe_experts(block_exp, row_tok, h2.reshape(b * s, d), w_gate, w_up, w_down, l)
        x = _combine(dest, y, x.reshape(b * s, d), rg.reshape(b * s, LANES), g2, s).reshape(b, s, d)

    return _final_norm(x.reshape(b * s, d), final_g).reshape(b, s, d)
```

```python
import functools

import jax
import jax.numpy as jnp
from jax import lax
from jax.experimental import pallas as pl
from jax.experimental.pallas import tpu as pltpu

EPS = 1e-6
LRU_C = 8.0
TOP_K = 2
LANES = 128
MOE_ROWS = 128
VMEM_LIMIT = 56 * 1024 * 1024

F32 = jnp.float32
BF16 = jnp.bfloat16


def _pick(n, pref):
    if n <= pref:
        return n
    for t in range(pref, 7, -1):
        if n % t == 0 and t % 8 == 0:
            return t
    raise ValueError((n, pref))


def _params(*sem):
    return pltpu.CompilerParams(dimension_semantics=sem, vmem_limit_bytes=VMEM_LIMIT)


def _sigmoid(x):
    return jax.nn.sigmoid(x)


def _ada_kernel(c_ref, w_ref, b_ref, o_ref):
    c = c_ref[...]
    cond = (c * _sigmoid(c)).astype(BF16)
    o_ref[...] = jnp.dot(cond, w_ref[...].astype(BF16), preferred_element_type=F32) + b_ref[...]


def _ada_mod(c_pad, w_ada, b_ada):
    depth, d, n = w_ada.shape
    rows = c_pad.shape[0]
    tn = _pick(n, 1024)
    return pl.pallas_call(
        _ada_kernel,
        grid=(depth, n // tn),
        in_specs=[
            pl.BlockSpec((rows, d), lambda l, j: (0, 0)),
            pl.BlockSpec((None, d, tn), lambda l, j: (l, 0, j)),
            pl.BlockSpec((None, 1, tn), lambda l, j: (l, 0, j)),
        ],
        out_specs=pl.BlockSpec((None, rows, tn), lambda l, j: (l, 0, j)),
        out_shape=jax.ShapeDtypeStruct((depth, rows, n), F32),
        compiler_params=_params("parallel", "parallel"),
        name="ada_mod",
    )(c_pad, w_ada, b_ada.reshape(depth, 1, n))


def _modulated_norm(x, g, sc, sh):
    y = x * lax.rsqrt(jnp.mean(x * x, axis=-1, keepdims=True) + EPS) * g
    return y * (1.0 + sc) + sh


def _inproj_kernel(x_ref, g_ref, sc_ref, sh_ref, w_ref, wf_ref, z_ref, zf_ref, h_scr):
    @pl.when(pl.program_id(2) == 0)
    def _():
        hb = _modulated_norm(x_ref[...], g_ref[...], sc_ref[...], sh_ref[...]).astype(BF16)
        h_scr[...] = hb
        zf_ref[...] = jnp.dot(hb, wf_ref[...], preferred_element_type=F32)

    z_ref[...] = jnp.dot(h_scr[...], w_ref[...], preferred_element_type=F32)


def _inproj(x, g, sc, sh, w_main, w_f):
    b, s, d = x.shape
    n = w_main.shape[1]
    tm = _pick(s, 1024)
    tn = _pick(n, 1024)
    return pl.pallas_call(
        _inproj_kernel,
        grid=(b, s // tm, n // tn),
        in_specs=[
            pl.BlockSpec((None, tm, d), lambda bi, i, j: (bi, i, 0)),
            pl.BlockSpec((1, d), lambda bi, i, j: (0, 0)),
            pl.BlockSpec((None, 1, d), lambda bi, i, j: (bi, 0, 0)),
            pl.BlockSpec((None, 1, d), lambda bi, i, j: (bi, 0, 0)),
            pl.BlockSpec((d, tn), lambda bi, i, j: (0, j)),
            pl.BlockSpec((d, LANES), lambda bi, i, j: (0, 0)),
        ],
        out_specs=[
            pl.BlockSpec((None, tm, tn), lambda bi, i, j: (bi, i, j)),
            pl.BlockSpec((None, tm, LANES), lambda bi, i, j: (bi, i, 0)),
        ],
        out_shape=[jax.ShapeDtypeStruct((b, s, n), F32), jax.ShapeDtypeStruct((b, s, LANES), F32)],
        scratch_shapes=[pltpu.VMEM((tm, d), BF16)],
        compiler_params=_params("parallel", "parallel", "arbitrary"),
        name="inproj",
    )(x, g, sc, sh, w_main, w_f)


def _conv_kernel(a_ref, g_ref, ah_ref, gh_ref, w_ref, b_ref, lg_ref, lb_ref, o_ref, buf, ybuf, shifted,
                 *, taps, ts, halo, rc):
    cw = a_ref.shape[-1]
    buf[halo:halo + ts, :] = a_ref[...] * _sigmoid(g_ref[...])
    prev = ah_ref[...] * _sigmoid(gh_ref[...])
    buf[0:halo, :] = jnp.where(pl.program_id(1) > 0, prev, 0.0)
    first = halo - (taps - 1)
    for c in range(cw // LANES):
        cs = slice(c * LANES, (c + 1) * LANES)
        for r in range(ts // rc):
            acc = b_ref[:, cs]
            for rem in range(8):
                group = [(k, first + k - rem) for k in range(taps) if (first + k) % 8 == rem]
                if not group:
                    continue
                lo = min(a for _, a in group)
                hi = max(a for _, a in group)
                shifted[0:hi - lo + rc, :] = buf[r * rc + rem + lo:r * rc + rem + hi + rc, cs]
                for k, a in group:
                    acc = acc + w_ref[k:k + 1, cs] * shifted[a - lo:a - lo + rc, :]
            ybuf[r * rc:(r + 1) * rc, cs] = acc
    y = ybuf[...]
    mu = jnp.mean(y, axis=-1, keepdims=True)
    yc = y - mu
    var = jnp.mean(yc * yc, axis=-1, keepdims=True)
    yn = yc * lax.rsqrt(var + EPS) * lg_ref[...] + lb_ref[...]
    o_ref[...] = (yn * _sigmoid(yn)).astype(o_ref.dtype)


def _conv_branch(z, cw, dw_w, dw_b, ln_g, ln_b):
    b, s, _ = z.shape
    taps = dw_w.shape[0]
    halo = 32
    assert taps - 1 <= halo
    ts = _pick(s, 256)
    rc = _pick(ts, 128)
    hb = ts // halo
    kern = functools.partial(_conv_kernel, taps=taps, ts=ts, halo=halo, rc=rc)
    cur = lambda col: pl.BlockSpec((None, ts, cw), lambda bi, i: (bi, i, col))
    prev = lambda col: pl.BlockSpec((None, halo, cw), lambda bi, i: (bi, jnp.maximum(i * hb - 1, 0), col))
    vec = pl.BlockSpec((1, cw), lambda bi, i: (0, 0))
    return pl.pallas_call(
        kern,
        grid=(b, s // ts),
        in_specs=[cur(0), cur(1), prev(0), prev(1),
                  pl.BlockSpec((taps, cw), lambda bi, i: (0, 0)), vec, vec, vec],
        out_specs=pl.BlockSpec((None, ts, cw), lambda bi, i: (bi, i, 0)),
        out_shape=jax.ShapeDtypeStruct((b, s, cw), BF16),
        scratch_shapes=[pltpu.VMEM((halo + ts, cw), F32), pltpu.VMEM((ts, cw), F32),
                        pltpu.VMEM((rc + halo, LANES), F32)],
        compiler_params=_params("parallel", "parallel"),
        name="conv_branch",
    )(z, z, z, z, dw_w, dw_b.reshape(1, cw), ln_g.reshape(1, cw), ln_b.reshape(1, cw))


def _shift_rows(v, d, fill):
    rows = lax.broadcasted_iota(jnp.int32, v.shape, 0)
    return jnp.where(rows >= d, pltpu.roll(v, d, axis=0), fill)


def _scan_linear(a, bv):
    d = 1
    while d < a.shape[0]:
        bv = a * _shift_rows(bv, d, 0.0) + bv
        a = a * _shift_rows(a, d, 1.0)
        d *= 2
    return a, bv


def _scan_sum(v):
    d = 1
    while d < v.shape[0]:
        v = v + _shift_rows(v, d, 0.0)
        d *= 2
    return v


def _cum_kernel(zf_ref, fb_ref, o_ref, car, *, ts, rs):
    @pl.when(pl.program_id(1) == 0)
    def _():
        car[...] = jnp.zeros_like(car)

    h = car[...]
    for r in range(ts // rs):
        x = zf_ref[r * rs:(r + 1) * rs, :] + fb_ref[...]
        log_f = jnp.minimum(x, 0.0) - jnp.log1p(jnp.exp(-jnp.abs(x)))
        v = _scan_sum(log_f) + h
        o_ref[r * rs:(r + 1) * rs, :] = v
        h = v[rs - 1:rs, :]
    car[...] = h


def _forget_cumsum(zf, f_bias_pad):
    b, s, _ = zf.shape
    ts = _pick(s, 512)
    rs = _pick(ts, 128)
    return pl.pallas_call(
        functools.partial(_cum_kernel, ts=ts, rs=rs),
        grid=(b, s // ts),
        in_specs=[pl.BlockSpec((None, ts, LANES), lambda bi, i: (bi, i, 0)),
                  pl.BlockSpec((1, LANES), lambda bi, i: (0, 0))],
        out_specs=pl.BlockSpec((None, ts, LANES), lambda bi, i: (bi, i, 0)),
        out_shape=jax.ShapeDtypeStruct((b, s, LANES), F32),
        scratch_shapes=[pltpu.VMEM((1, LANES), F32)],
        compiler_params=_params("parallel", "arbitrary"),
        name="forget_cumsum",
    )(zf, f_bias_pad)


def _gelu_tanh(x):
    return 0.5 * x * (1.0 + jnp.tanh(0.7978845608028654 * (x + 0.044715 * (x * x * x))))


def _lru_kernel(u_ref, uh_ref, gate_ref, cw_ref, cb_ref, wr_ref, br_ref, wi_ref, bi_ref, lam_ref, og_ref,
                o_ref, ubuf, abuf, bbuf, car, *, taps, ts, halo, rs):
    lw = u_ref.shape[-1]

    @pl.when(pl.program_id(1) == 0)
    def _():
        car[...] = jnp.zeros_like(car)

    ubuf[halo:halo + ts, :] = u_ref[...]
    ubuf[0:halo, :] = jnp.where(pl.program_id(1) > 0, uh_ref[...], 0.0)
    xc = cb_ref[...]
    for k in range(taps):
        off = halo - (taps - 1) + k
        xc = xc + cw_ref[k:k + 1, :] * ubuf[off:off + ts, :]
    xb = xc.astype(BF16)
    r = _sigmoid(jnp.dot(xb, wr_ref[...], preferred_element_type=F32) + br_ref[...])
    ig = _sigmoid(jnp.dot(xb, wi_ref[...], preferred_element_type=F32) + bi_ref[...])
    nl = -lam_ref[...]
    softplus = jnp.maximum(nl, 0.0) + jnp.log1p(jnp.exp(-jnp.abs(nl)))
    log_a = (-LRU_C) * r * softplus
    th = jnp.tanh(log_a)
    abuf[...] = jnp.exp(log_a)
    bbuf[...] = jnp.sqrt(-2.0 * th / (1.0 - th)) * (ig * xc)
    for c in range(lw // LANES):
        cs = slice(c * LANES, (c + 1) * LANES)
        h = car[:, cs]
        for q in range(ts // rs):
            rows = slice(q * rs, (q + 1) * rs)
            a_cum, b_cum = _scan_linear(abuf[rows, cs], bbuf[rows, cs])
            hv = b_cum + a_cum * h
            bbuf[rows, cs] = hv
            h = hv[rs - 1:rs, :]
        car[:, cs] = h
    y = bbuf[...] * _gelu_tanh(gate_ref[...])
    y = y * lax.rsqrt(jnp.mean(y * y, axis=-1, keepdims=True) + EPS) * og_ref[...]
    o_ref[...] = y.astype(o_ref.dtype)


def _lru_branch(z, lw, col_x, conv_w, conv_b, wr_bd, b_r, wi_bd, b_i, lam, out_g):
    b, s, _ = z.shape
    taps = conv_w.shape[0]
    halo = 8
    assert taps - 1 <= halo
    ts = _pick(s, 256)
    rs = _pick(ts, 128)
    hb = ts // halo
    kern = functools.partial(_lru_kernel, taps=taps, ts=ts, halo=halo, rs=rs)
    vec = pl.BlockSpec((1, lw), lambda bi, i: (0, 0))
    mat = pl.BlockSpec((lw, lw), lambda bi, i: (0, 0))
    return pl.pallas_call(
        kern,
        grid=(b, s // ts),
        in_specs=[
            pl.BlockSpec((None, ts, lw), lambda bi, i: (bi, i, col_x)),
            pl.BlockSpec((None, halo, lw), lambda bi, i: (bi, jnp.maximum(i * hb - 1, 0), col_x)),
            pl.BlockSpec((None, ts, lw), lambda bi, i: (bi, i, col_x + 1)),
            pl.BlockSpec((taps, lw), lambda bi, i: (0, 0)), vec, mat, vec, mat, vec, vec, vec,
        ],
        out_specs=pl.BlockSpec((None, ts, lw), lambda bi, i: (bi, i, 0)),
        out_shape=jax.ShapeDtypeStruct((b, s, lw), BF16),
        scratch_shapes=[pltpu.VMEM((halo + ts, lw), F32), pltpu.VMEM((ts, lw), F32),
                        pltpu.VMEM((ts, lw), F32), pltpu.VMEM((1, lw), F32)],
        compiler_params=_params("parallel", "arbitrary"),
        name="lru_branch",
    )(z, z, z, conv_w, conv_b.reshape(1, lw), wr_bd, b_r.reshape(1, lw), wi_bd, b_i.reshape(1, lw),
      lam.reshape(1, lw), out_g.reshape(1, lw))


def _split3(t):
    hi = t.astype(BF16).astype(F32)
    r = t - hi
    mid = r.astype(BF16).astype(F32)
    return hi, mid, r - mid


def _attn_kernel(q_ref, k_ref, v_ref, cum_ref, o_ref, kaug, vaug, qaug, s_a, s_b, p_a, p_b, al_a, al_b, m_ref, acc_ref,
                 *, tq, tk, scale):
    h = pl.program_id(1)
    i = pl.program_id(2)
    s_len, hd = k_ref.shape
    inv_scale = 1.0 / scale
    c_exp = scale * 1.4426950408889634
    s_buf, p_buf, al_buf = (s_a, s_b), (p_a, p_b), (al_a, al_b)

    def head_cum(start, size):
        lane = lax.broadcasted_iota(jnp.int32, (size, LANES), 1)
        rows = pl.ds(pl.multiple_of(start, size), size)
        return jnp.sum(jnp.where(lane == h, cum_ref[rows, :], 0.0), axis=-1, keepdims=True) * inv_scale

    def key_rows(j):
        return pl.ds(pl.multiple_of(j * tk, tk), tk)

    @pl.when(i == 0)
    def _():
        lane = lax.broadcasted_iota(jnp.int32, (tk, LANES), 1)

        def body(t, carry):
            rows = key_rows(t)
            hi, mid, lo = _split3(-head_cum(t * tk, tk))
            aug = jnp.where(lane < 3, 1.0, jnp.where(lane == 3, hi, jnp.where(lane == 4, mid,
                            jnp.where(lane == 5, lo, 0.0))))
            kaug[t, 0:hd, :] = k_ref[rows, :].T.astype(BF16)
            kaug[t, hd:2 * hd, :] = aug.T.astype(BF16)
            vaug[rows, 0:hd] = v_ref[rows, :].astype(BF16)
            vaug[rows, hd:2 * hd] = jnp.where(lane == 0, 1.0, 0.0).astype(BF16)
            return carry

        lax.fori_loop(0, s_len // tk, body, 0)

    lane_q = lax.broadcasted_iota(jnp.int32, (tq, LANES), 1)
    hi, mid, lo = _split3(head_cum(i * tq, tq))
    qa = jnp.where(lane_q == 0, hi, jnp.where(lane_q == 1, mid, jnp.where(lane_q == 2, lo,
                   jnp.where(lane_q < 6, 1.0, 0.0))))
    qaug[:, 0:hd] = q_ref[...].astype(BF16)
    qaug[:, hd:2 * hd] = qa.astype(BF16)

    def scores_into(dst, j, masked=False):
        s = jnp.dot(qaug[...], kaug[j], preferred_element_type=F32)
        if masked:
            row = lax.broadcasted_iota(jnp.int32, s.shape, 0)
            col = lax.broadcasted_iota(jnp.int32, s.shape, 1)
            s = jnp.where(col <= row, s, -jnp.inf)
        dst[...] = s

    def softmax_into(src, dst_p, dst_alpha):
        s = src[...]
        m_old = m_ref[...]
        m_new = jnp.maximum(m_old, jnp.max(s, axis=-1, keepdims=True))
        dst_alpha[...] = jnp.exp2((m_old - m_new) * c_exp)
        m_ref[...] = m_new
        dst_p[...] = jnp.exp2((s - jnp.concatenate([m_new] * (tk // LANES), axis=1)) * c_exp).astype(BF16)

    def accumulate(src_alpha, src_p, j):
        alpha = jnp.concatenate([src_alpha[...]] * (acc_ref.shape[1] // LANES), axis=1)
        acc_ref[...] = alpha * acc_ref[...] + jnp.dot(src_p[...], vaug[key_rows(j), :], preferred_element_type=F32)

    def blk(t):
        return jnp.where(t == 0, i, jnp.maximum(t - 1, 0))

    def tick(t, par):
        scores_into(s_buf[1 - par], t)
        softmax_into(s_buf[par], p_buf[par], al_buf[par])
        accumulate(al_buf[1 - par], p_buf[1 - par], blk(t - 1))

    scores_into(s_buf[0], i, masked=True)
    m_ref[...] = jnp.full(m_ref.shape, -jnp.inf, F32)
    acc_ref[...] = jnp.zeros(acc_ref.shape, F32)
    p_buf[1][...] = jnp.zeros(p_buf[1].shape, BF16)
    al_buf[1][...] = jnp.ones(al_buf[1].shape, F32)

    def body(u, carry):
        tick(2 * u, 0)
        tick(2 * u + 1, 1)
        return carry

    lax.fori_loop(0, i // 2, body, 0)
    odd = i % 2 == 1

    @pl.when(odd)
    def _():
        tick(i - 1, 0)

    def drain(par):
        softmax_into(s_buf[par], p_buf[par], al_buf[par])
        accumulate(al_buf[1 - par], p_buf[1 - par], blk(i - 1))
        accumulate(al_buf[par], p_buf[par], blk(i))

    @pl.when(odd)
    def _():
        drain(1)

    @pl.when(jnp.logical_not(odd))
    def _():
        drain(0)

    o_ref[...] = acc_ref[:, 0:hd] / acc_ref[:, hd:hd + 1]


def _attention(z, cum, heads, hd, col_q, col_k, col_v):
    b, s, _ = z.shape
    tq = _pick(s, 512)
    tk = tq
    kern = functools.partial(_attn_kernel, tq=tq, tk=tk, scale=hd ** -0.5)
    return pl.pallas_call(
        kern,
        grid=(b, heads, s // tq),
        in_specs=[
            pl.BlockSpec((None, tq, hd), lambda bi, h, i: (bi, i, col_q + h)),
            pl.BlockSpec((None, s, hd), lambda bi, h, i: (bi, 0, col_k + h)),
            pl.BlockSpec((None, s, hd), lambda bi, h, i: (bi, 0, col_v + h)),
            pl.BlockSpec((None, s, LANES), lambda bi, h, i: (bi, 0, 0)),
        ],
        out_specs=pl.BlockSpec((None, tq, hd), lambda bi, h, i: (bi, i, h)),
        out_shape=jax.ShapeDtypeStruct((b, s, heads * hd), F32),
        scratch_shapes=[pltpu.VMEM((s // tk, 2 * hd, tk), BF16), pltpu.VMEM((s, 2 * hd), BF16),
                        pltpu.VMEM((tq, 2 * hd), BF16),
                        pltpu.VMEM((tq, tk), F32), pltpu.VMEM((tq, tk), F32),
                        pltpu.VMEM((tq, tk), BF16), pltpu.VMEM((tq, tk), BF16),
                        pltpu.VMEM((tq, LANES), F32), pltpu.VMEM((tq, LANES), F32), pltpu.VMEM((tq, LANES), F32),
                        pltpu.VMEM((tq, 2 * hd), F32)],
        compiler_params=_params("arbitrary", "arbitrary", "arbitrary"),
        name="fox_attention",
    )(z, z, z, cum)


def _outproj_kernel(yc_ref, o_ref, yl_ref, fg_ref, w1_ref, w2_ref, w3_ref, x_ref, g1_ref, out_ref, yf_scr):
    @pl.when(pl.program_id(2) == 0)
    def _():
        o = o_ref[...]
        yf = o * lax.rsqrt(jnp.mean(o * o, axis=-1, keepdims=True) + EPS) * fg_ref[...]
        yf_scr[...] = yf.astype(BF16)

    y = jnp.dot(yc_ref[...], w1_ref[...], preferred_element_type=F32)
    y = y + jnp.dot(yf_scr[...], w2_ref[...], preferred_element_type=F32)
    y = y + jnp.dot(yl_ref[...], w3_ref[...], preferred_element_type=F32)
    out_ref[...] = x_ref[...] + g1_ref[...] * y


def _outproj(y_conv, o_fox, y_lru, fox_g, w1, w2, w3, x, g1):
    b, s, d = x.shape
    cw, fw, lw = y_conv.shape[-1], o_fox.shape[-1], y_lru.shape[-1]
    tm = _pick(s, 512)
    tn = _pick(d, 1024)
    act = lambda w: pl.BlockSpec((None, tm, w), lambda bi, i, j: (bi, i, 0))
    wgt = lambda w: pl.BlockSpec((w, tn), lambda bi, i, j: (0, j))
    return pl.pallas_call(
        _outproj_kernel,
        grid=(b, s // tm, d // tn),
        in_specs=[act(cw), act(fw), act(lw), pl.BlockSpec((1, fw), lambda bi, i, j: (0, 0)),
                  wgt(cw), wgt(fw), wgt(lw),
                  pl.BlockSpec((None, tm, tn), lambda bi, i, j: (bi, i, j)),
                  pl.BlockSpec((None, 1, tn), lambda bi, i, j: (bi, 0, j))],
        out_specs=pl.BlockSpec((None, tm, tn), lambda bi, i, j: (bi, i, j)),
        out_shape=jax.ShapeDtypeStruct((b, s, d), F32),
        scratch_shapes=[pltpu.VMEM((tm, fw), BF16)],
        compiler_params=_params("parallel", "parallel", "arbitrary"),
        name="outproj",
    )(y_conv, o_fox, y_lru, fox_g.reshape(1, fw), w1, w2, w3, x, g1)


def _bf16_bits(v):
    return lax.bitcast_convert_type(v.astype(BF16).astype(F32), jnp.uint32)


def _pack_halves(v):
    half = v.shape[1] // 2
    word = _bf16_bits(v[:, half:]) | (_bf16_bits(v[:, :half]) >> 16)
    return lax.bitcast_convert_type(word, jnp.int32)


def _unpack_halves(w):
    word = lax.bitcast_convert_type(w, jnp.uint32)
    lo = lax.bitcast_convert_type(word << 16, F32)
    hi = lax.bitcast_convert_type(word & jnp.uint32(0xFFFF0000), F32)
    return lo, hi


def _router_kernel(x_ref, g_ref, sc_ref, sh_ref, whi_ref, wlo_ref, rb_ref, h2_ref, ri_ref, rg_ref, cnt_ref, car,
                   *, n_groups, per_group):
    @pl.when(jnp.logical_and(pl.program_id(0) == 0, pl.program_id(1) == 0))
    def _():
        car[...] = jnp.zeros_like(car)

    h2 = _modulated_norm(x_ref[...], g_ref[...], sc_ref[...], sh_ref[...])
    h2_ref[...] = _pack_halves(h2)
    hi = h2.astype(BF16)
    lo = (h2 - hi.astype(F32)).astype(BF16)
    logits = (jnp.dot(hi, whi_ref[...], preferred_element_type=F32)
              + jnp.dot(lo, whi_ref[...], preferred_element_type=F32)
              + jnp.dot(hi, wlo_ref[...], preferred_element_type=F32)) + rb_ref[...]
    lane_i = lax.broadcasted_iota(jnp.int32, logits.shape, 1)
    lane = lane_i.astype(F32)
    big = float(LANES)

    def first_lane(hit):
        return jnp.min(jnp.where(hit, lane, big), axis=-1, keepdims=True)

    gmask = lane_i < n_groups
    lg1 = jnp.where(gmask, logits, -jnp.inf)
    m1 = jnp.max(lg1, axis=-1, keepdims=True)
    e1 = jnp.exp(lg1 - m1)
    p1 = e1 / jnp.sum(e1, axis=-1, keepdims=True)
    p_grp = jnp.max(p1, axis=-1, keepdims=True)
    grp = first_lane((p1 == p_grp) & gmask)
    lo_lane = n_groups + grp * per_group
    emask = (lane >= lo_lane) & (lane < lo_lane + per_group)
    lg2 = jnp.where(emask, logits, -jnp.inf)
    m2 = jnp.max(lg2, axis=-1, keepdims=True)
    e2 = jnp.exp(lg2 - m2)
    p2 = e2 / jnp.sum(e2, axis=-1, keepdims=True)
    v1 = jnp.max(jnp.where(emask, p2, -1.0), axis=-1, keepdims=True)
    i1 = first_lane((p2 == v1) & emask)
    rest = emask & (lane != i1)
    v2 = jnp.max(jnp.where(rest, p2, -1.0), axis=-1, keepdims=True)
    i2 = first_lane((p2 == v2) & rest)
    denom = v1 + v2
    gate1 = p_grp * (v1 / denom)
    gate2 = p_grp * (v2 / denom)
    tm = logits.shape[0]
    picked = jnp.where((lane == i1) | (lane == i2), 1.0, 0.0)
    tri = jnp.where(lax.broadcasted_iota(jnp.int32, (tm, tm), 0) > lax.broadcasted_iota(jnp.int32, (tm, tm), 1),
                    1.0, 0.0).astype(BF16)
    before = jnp.dot(tri, picked.astype(BF16), preferred_element_type=F32) + car[...]
    rank1 = jnp.sum(jnp.where(lane == i1, before, 0.0), axis=-1, keepdims=True)
    rank2 = jnp.sum(jnp.where(lane == i2, before, 0.0), axis=-1, keepdims=True)
    total = car[...] + jnp.sum(picked, axis=0, keepdims=True)
    car[...] = total
    cnt_ref[...] = total
    info = jnp.where(lane_i == 0, i1 - n_groups, jnp.where(lane_i == 1, i2 - n_groups,
                     jnp.where(lane_i == 2, rank1, jnp.where(lane_i == 3, rank2, 0.0))))
    ri_ref[...] = info.astype(jnp.int32)
    rg_ref[...] = jnp.where(lane_i == 0, gate1, jnp.where(lane_i == 1, gate2, 0.0))


def _router(x, g, sc, sh, w_hi, w_lo, r_bias, n_groups, per_group):
    b, s, d = x.shape
    tm = _pick(s, 512)
    kern = functools.partial(_router_kernel, n_groups=n_groups, per_group=per_group)
    row = lambda w: pl.BlockSpec((None, tm, w), lambda bi, i: (bi, i, 0))
    mod = pl.BlockSpec((None, 1, d), lambda bi, i: (bi, 0, 0))
    wsp = pl.BlockSpec((d, LANES), lambda bi, i: (0, 0))
    return pl.pallas_call(
        kern,
        grid=(b, s // tm),
        in_specs=[row(d), pl.BlockSpec((1, d), lambda bi, i: (0, 0)), mod, mod, wsp, wsp,
                  pl.BlockSpec((1, LANES), lambda bi, i: (0, 0))],
        out_specs=[row(d // 2), row(LANES), row(LANES), pl.BlockSpec((1, LANES), lambda bi, i: (0, 0))],
        out_shape=[jax.ShapeDtypeStruct((b, s, d // 2), jnp.int32), jax.ShapeDtypeStruct((b, s, LANES), jnp.int32),
                   jax.ShapeDtypeStruct((b, s, LANES), F32), jax.ShapeDtypeStruct((1, LANES), F32)],
        scratch_shapes=[pltpu.VMEM((1, LANES), F32)],
        compiler_params=_params("arbitrary", "arbitrary"),
        name="router",
    )(x, g, sc, sh, w_hi, w_lo, r_bias)


def _moe_kernel(bexp_ref, rtok_ref, h2_hbm, wg_ref, wu_ref, wd_ref, y_ref, xbuf_a, xbuf_b, sem, wg_bf, wu_bf, wd_bf,
                *, rows):
    b = pl.program_id(0)
    nb = pl.num_programs(0)
    xbufs = (xbuf_a, xbuf_b)

    def row_copy(tok, r, slot):
        return pltpu.make_async_copy(h2_hbm.at[pl.ds(tok, 1)], xbufs[slot].at[pl.ds(r, 1)], sem.at[slot])

    def wait_block(slot):
        pltpu.make_async_copy(h2_hbm.at[pl.ds(0, rows)], xbufs[slot], sem.at[slot]).wait()

    @pl.when(b == 0)
    def _():
        def body(r, carry):
            row_copy(rtok_ref[r], r, 0).start()
            return carry

        lax.fori_loop(0, rows, body, 0)

    changed = jnp.logical_or(b == 0, bexp_ref[b] != bexp_ref[jnp.maximum(b - 1, 0)])

    @pl.when(changed)
    def _():
        wg_bf[...] = wg_ref[...].astype(BF16)
        wu_bf[...] = wu_ref[...].astype(BF16)
        wd_bf[...] = wd_ref[...].astype(BF16)

    def block(slot):
        wait_block(slot)
        base = jnp.minimum(b + 1, nb - 1) * rows
        for r in range(rows):
            row_copy(rtok_ref[base + r], r, 1 - slot).start()
        lo, hi = _unpack_halves(xbufs[slot][...])
        xb = jnp.concatenate([lo.astype(BF16), hi.astype(BF16)], axis=1)
        gt = jnp.dot(xb, wg_bf[...], preferred_element_type=F32)
        up = jnp.dot(xb, wu_bf[...], preferred_element_type=F32)
        hb = (gt * _sigmoid(gt) * up).astype(BF16)
        y_ref[...] = _pack_halves(jnp.dot(hb, wd_bf[...], preferred_element_type=F32))

        @pl.when(b == nb - 1)
        def _():
            wait_block(1 - slot)

    @pl.when(b % 2 == 0)
    def _():
        block(0)

    @pl.when(b % 2 == 1)
    def _():
        block(1)


def _moe_experts(block_exp, row_tok, h2_rows, w_gate, w_up, w_down, layer):
    _, _, d, f = w_gate.shape
    rows = MOE_ROWS
    n_blocks = block_exp.shape[0]
    grid_spec = pltpu.PrefetchScalarGridSpec(
        num_scalar_prefetch=2,
        grid=(n_blocks,),
        in_specs=[
            pl.BlockSpec(memory_space=pl.ANY),
            pl.BlockSpec((None, None, d, f), lambda bi, be, rt: (layer, be[bi], 0, 0)),
            pl.BlockSpec((None, None, d, f), lambda bi, be, rt: (layer, be[bi], 0, 0)),
            pl.BlockSpec((None, None, f, d), lambda bi, be, rt: (layer, be[bi], 0, 0)),
        ],
        out_specs=pl.BlockSpec((rows, d // 2), lambda bi, be, rt: (bi, 0)),
        scratch_shapes=[pltpu.VMEM((rows, d // 2), jnp.int32), pltpu.VMEM((rows, d // 2), jnp.int32),
                        pltpu.SemaphoreType.DMA((2,)),
                        pltpu.VMEM((d, f), BF16), pltpu.VMEM((d, f), BF16), pltpu.VMEM((f, d), BF16)],
    )
    return pl.pallas_call(
        functools.partial(_moe_kernel, rows=rows),
        grid_spec=grid_spec,
        out_shape=jax.ShapeDtypeStruct((n_blocks * rows, d // 2), jnp.int32),
        compiler_params=_params("arbitrary"),
        name="moe_experts",
    )(block_exp, row_tok, h2_rows, w_gate, w_up, w_down)


def _combine_kernel(dest_ref, y_hbm, x_ref, rg_ref, g2_ref, out_ref, ybuf_a, ybuf_b, sem, *, tm):
    i = pl.program_id(0)
    n = pl.num_programs(0)
    ybufs = (ybuf_a, ybuf_b)

    def start(row, r, k, slot):
        pltpu.make_async_copy(y_hbm.at[pl.ds(row, 1)], ybufs[slot].at[k, pl.ds(r, 1)], sem.at[slot]).start()

    def wait_tile(slot):
        for k in range(TOP_K):
            pltpu.make_async_copy(y_hbm.at[pl.ds(0, tm)], ybufs[slot].at[k], sem.at[slot]).wait()

    @pl.when(i == 0)
    def _():
        def body(r, carry):
            for k in range(TOP_K):
                start(dest_ref[r * TOP_K + k], r, k, 0)
            return carry

        lax.fori_loop(0, tm, body, 0)

    def tile(slot):
        wait_tile(slot)
        base = jnp.minimum(i + 1, n - 1) * (tm * TOP_K)
        for r in range(tm):
            for k in range(TOP_K):
                start(dest_ref[base + r * TOP_K + k], r, k, 1 - slot)
        gates = rg_ref[...]
        gate = [jnp.broadcast_to(gates[:, k:k + 1], (tm, LANES)) for k in range(TOP_K)]
        half = out_ref.shape[1] // 2
        for j in range(half // LANES):
            words = slice(j * LANES, (j + 1) * LANES)
            moe_lo = moe_hi = None
            for k in range(TOP_K):
                lo, hi = _unpack_halves(ybufs[slot][k, :, words])
                moe_lo = lo * gate[k] if moe_lo is None else moe_lo + lo * gate[k]
                moe_hi = hi * gate[k] if moe_hi is None else moe_hi + hi * gate[k]
            for cols, moe in ((words, moe_lo), (slice(half + j * LANES, half + (j + 1) * LANES), moe_hi)):
                out_ref[:, cols] = x_ref[:, cols] + g2_ref[:, cols] * moe

        @pl.when(i == n - 1)
        def _():
            wait_tile(1 - slot)

    @pl.when(i % 2 == 0)
    def _():
        tile(0)

    @pl.when(i % 2 == 1)
    def _():
        tile(1)


def _combine(dest, y, x_flat, rg_flat, g2, seq):
    t, d = x_flat.shape
    tm = _pick(seq, 128)
    grid_spec = pltpu.PrefetchScalarGridSpec(
        num_scalar_prefetch=1,
        grid=(t // tm,),
        in_specs=[
            pl.BlockSpec(memory_space=pl.ANY),
            pl.BlockSpec((tm, d), lambda i, de: (i, 0)),
            pl.BlockSpec((tm, LANES), lambda i, de: (i, 0)),
            pl.BlockSpec((None, 1, d), lambda i, de: ((i * tm) // seq, 0, 0)),
        ],
        out_specs=pl.BlockSpec((tm, d), lambda i, de: (i, 0)),
        scratch_shapes=[pltpu.VMEM((TOP_K, tm, d // 2), jnp.int32), pltpu.VMEM((TOP_K, tm, d // 2), jnp.int32),
                        pltpu.SemaphoreType.DMA((2,))],
    )
    return pl.pallas_call(
        functools.partial(_combine_kernel, tm=tm),
        grid_spec=grid_spec,
        out_shape=jax.ShapeDtypeStruct((t, d), F32),
        compiler_params=_params("arbitrary"),
        name="moe_combine",
    )(dest, y, x_flat, rg_flat, g2)


def _final_norm_kernel(x_ref, g_ref, o_ref):
    x = x_ref[...]
    o_ref[...] = x * lax.rsqrt(jnp.mean(x * x, axis=-1, keepdims=True) + EPS) * g_ref[...]


def _final_norm(x_flat, g):
    t, d = x_flat.shape
    tm = _pick(t, 512)
    return pl.pallas_call(
        _final_norm_kernel,
        grid=(t // tm,),
        in_specs=[pl.BlockSpec((tm, d), lambda i: (i, 0)), pl.BlockSpec((1, d), lambda i: (0, 0))],
        out_specs=pl.BlockSpec((tm, d), lambda i: (i, 0)),
        out_shape=jax.ShapeDtypeStruct((t, d), F32),
        compiler_params=_params("parallel"),
        name="final_norm",
    )(x_flat, g.reshape(1, d))


def _routing_tables(eid, rank, counts, rows):
    n_assign = eid.shape[0]
    n_experts = counts.shape[0]
    padded = (counts + rows - 1) // rows * rows
    pends = jnp.cumsum(padded)
    pstarts = pends - padded
    onehot = eid[:, None] == jnp.arange(n_experts, dtype=jnp.int32)[None, :]
    dest = (jnp.sum(jnp.where(onehot, pstarts[None, :], 0), axis=1) + rank).astype(jnp.int32)
    n_blocks = -(-n_assign // rows) + n_experts
    row_tok = jnp.zeros((n_blocks * rows,), jnp.int32).at[dest].set(jnp.arange(n_assign, dtype=jnp.int32) // TOP_K)
    first_row = jnp.arange(n_blocks, dtype=jnp.int32) * rows
    block_exp = jnp.minimum(jnp.sum((pends[None, :] <= first_row[:, None]).astype(jnp.int32), axis=1), n_experts - 1)
    return block_exp.astype(jnp.int32), row_tok, dest


def _block_diag(w):
    nb, bw, _ = w.shape
    eye = jnp.eye(nb, dtype=w.dtype)
    return (eye[:, None, :, None] * w[:, :, None, :]).reshape(nb * bw, nb * bw)


def kernel(x, c, w_ada, b_ada, ln1_g, w_in, conv_dw_w, conv_dw_b, conv_ln_g, conv_ln_b, fox_f_bias, fox_out_g, lru_conv_w, lru_conv_b, lru_w_r, lru_b_r, lru_w_i, lru_b_i, lru_lambda, lru_out_g, w_out, ln2_g, w_router_group, b_router_group, w_router_expert, b_router_expert, w_gate, w_up, w_down, final_g):
    b, s, d = x.shape
    depth = w_ada.shape[0]
    cw = conv_dw_b.shape[-1]
    fw = fox_out_g.shape[-1]
    lw = lru_lambda.shape[-1]
    heads = fox_f_bias.shape[-1]
    hd = fw // heads
    n_groups, per_group = b_router_expert.shape[1:]
    n_experts = n_groups * per_group
    off_f = 2 * cw + 3 * fw
    assert cw % LANES == 0 and lw == cw and fw % cw == 0 and hd == LANES and heads <= LANES
    assert n_groups + n_experts <= LANES

    c_pad = jnp.zeros((8, d), F32).at[:b].set(c)
    mod_all = _ada_mod(c_pad, w_ada, b_ada)

    for l in range(depth):
        mod = mod_all[l, :b].reshape(b, 1, 6 * d)
        sh1, sc1, g1, sh2, sc2, g2 = [mod[..., k * d:(k + 1) * d] for k in range(6)]

        w_l = w_in[l]
        w_main = jnp.concatenate([w_l[:, :off_f], w_l[:, off_f + heads:]], axis=1).astype(BF16)
        w_f = jnp.zeros((d, LANES), BF16).at[:, :heads].set(w_l[:, off_f:off_f + heads].astype(BF16))
        z, zf = _inproj(x, ln1_g[l].reshape(1, d), sc1, sh1, w_main, w_f)

        y_conv = _conv_branch(z, cw, conv_dw_w[l], conv_dw_b[l], conv_ln_g[l], conv_ln_b[l])

        f_bias = jnp.zeros((1, LANES), F32).at[0, :heads].set(fox_f_bias[l])
        cum = _forget_cumsum(zf, f_bias)
        o_fox = _attention(z, cum, heads, hd, 2 * cw // hd, (2 * cw + fw) // hd, (2 * cw + 2 * fw) // hd)

        y_lru = _lru_branch(z, lw, off_f // lw, lru_conv_w[l], lru_conv_b[l],
                            _block_diag(lru_w_r[l]).astype(BF16), lru_b_r[l],
                            _block_diag(lru_w_i[l]).astype(BF16), lru_b_i[l], lru_lambda[l], lru_out_g[l])

        wo = w_out[l].astype(BF16)
        x = _outproj(y_conv, o_fox, y_lru, fox_out_g[l], wo[:cw], wo[cw:cw + fw], wo[cw + fw:], x, g1)

        w_r = jnp.concatenate(
            [w_router_group[l], w_router_expert[l].transpose(1, 0, 2).reshape(d, n_experts)], axis=1)
        w_r = jnp.zeros((d, LANES), F32).at[:, :n_groups + n_experts].set(w_r)
        w_hi = w_r.astype(BF16)
        w_lo = (w_r - w_hi.astype(F32)).astype(BF16)
        r_bias = jnp.zeros((1, LANES), F32).at[0, :n_groups + n_experts].set(
            jnp.concatenate([b_router_group[l], b_router_expert[l].reshape(-1)]))
        h2, ri, rg, cnt = _router(x, ln2_g[l].reshape(1, d), sc2, sh2, w_hi, w_lo, r_bias, n_groups, per_group)

        eid = ri[:, :, :TOP_K].reshape(-1)
        rank = ri[:, :, TOP_K:2 * TOP_K].reshape(-1)
        counts = cnt[0, n_groups:n_groups + n_experts].astype(jnp.int32)
        block_exp, row_tok, dest = _routing_tables(eid, rank, counts, MOE_ROWS)
        y = _moe_experts(block_exp, row_tok, h2.reshape(b * s, d // 2), w_gate, w_up, w_down, l)
        x = _combine(dest, y, x.reshape(b * s, d), rg.reshape(b * s, LANES), g2, s).reshape(b, s, d)

    return _final_norm(x.reshape(b * s, d), final_g).reshape(b, s, d)
```

```python
import functools

import jax
import jax.numpy as jnp
from jax import lax
from jax.experimental import pallas as pl
from jax.experimental.pallas import tpu as pltpu

EPS = 1e-6
LRU_C = 8.0
TOP_K = 2
LANES = 128
MOE_ROWS = 128
VMEM_LIMIT = 56 * 1024 * 1024

F32 = jnp.float32
BF16 = jnp.bfloat16


def _pick(n, pref):
    if n <= pref:
        return n
    for t in range(pref, 7, -1):
        if n % t == 0 and t % 8 == 0:
            return t
    raise ValueError((n, pref))


def _params(*sem):
    return pltpu.CompilerParams(dimension_semantics=sem, vmem_limit_bytes=VMEM_LIMIT)


def _sigmoid(x):
    return jax.nn.sigmoid(x)


def _ada_kernel(c_ref, w_ref, b_ref, o_ref):
    c = c_ref[...]
    cond = (c * _sigmoid(c)).astype(BF16)
    o_ref[...] = jnp.dot(cond, w_ref[...].astype(BF16), preferred_element_type=F32) + b_ref[...]


def _ada_mod(c_pad, w_ada, b_ada):
    depth, d, n = w_ada.shape
    rows = c_pad.shape[0]
    tn = _pick(n, 1024)
    return pl.pallas_call(
        _ada_kernel,
        grid=(depth, n // tn),
        in_specs=[
            pl.BlockSpec((rows, d), lambda l, j: (0, 0)),
            pl.BlockSpec((None, d, tn), lambda l, j: (l, 0, j)),
            pl.BlockSpec((None, 1, tn), lambda l, j: (l, 0, j)),
        ],
        out_specs=pl.BlockSpec((None, rows, tn), lambda l, j: (l, 0, j)),
        out_shape=jax.ShapeDtypeStruct((depth, rows, n), F32),
        compiler_params=_params("parallel", "parallel"),
        name="ada_mod",
    )(c_pad, w_ada, b_ada.reshape(depth, 1, n))


def _modulated_norm(x, g, sc, sh):
    y = x * lax.rsqrt(jnp.mean(x * x, axis=-1, keepdims=True) + EPS) * g
    return y * (1.0 + sc) + sh


def _inproj_kernel(x_ref, g_ref, sc_ref, sh_ref, w_ref, wf_ref, z_ref, zf_ref, h_scr):
    @pl.when(pl.program_id(2) == 0)
    def _():
        hb = _modulated_norm(x_ref[...], g_ref[...], sc_ref[...], sh_ref[...]).astype(BF16)
        h_scr[...] = hb
        zf_ref[...] = jnp.dot(hb, wf_ref[...], preferred_element_type=F32)

    z_ref[...] = jnp.dot(h_scr[...], w_ref[...], preferred_element_type=F32)


def _inproj(x, g, sc, sh, w_main, w_f):
    b, s, d = x.shape
    n = w_main.shape[1]
    tm = _pick(s, 1024)
    tn = _pick(n, 1024)
    return pl.pallas_call(
        _inproj_kernel,
        grid=(b, s // tm, n // tn),
        in_specs=[
            pl.BlockSpec((None, tm, d), lambda bi, i, j: (bi, i, 0)),
            pl.BlockSpec((1, d), lambda bi, i, j: (0, 0)),
            pl.BlockSpec((None, 1, d), lambda bi, i, j: (bi, 0, 0)),
            pl.BlockSpec((None, 1, d), lambda bi, i, j: (bi, 0, 0)),
            pl.BlockSpec((d, tn), lambda bi, i, j: (0, j)),
            pl.BlockSpec((d, LANES), lambda bi, i, j: (0, 0)),
        ],
        out_specs=[
            pl.BlockSpec((None, tm, tn), lambda bi, i, j: (bi, i, j)),
            pl.BlockSpec((None, tm, LANES), lambda bi, i, j: (bi, i, 0)),
        ],
        out_shape=[jax.ShapeDtypeStruct((b, s, n), F32), jax.ShapeDtypeStruct((b, s, LANES), F32)],
        scratch_shapes=[pltpu.VMEM((tm, d), BF16)],
        compiler_params=_params("parallel", "parallel", "arbitrary"),
        name="inproj",
    )(x, g, sc, sh, w_main, w_f)


def _conv_kernel(a_ref, g_ref, ah_ref, gh_ref, w_ref, b_ref, lg_ref, lb_ref, o_ref, buf, ybuf, shifted,
                 *, taps, ts, halo, rc):
    cw = a_ref.shape[-1]
    buf[halo:halo + ts, :] = a_ref[...] * _sigmoid(g_ref[...])
    prev = ah_ref[...] * _sigmoid(gh_ref[...])
    buf[0:halo, :] = jnp.where(pl.program_id(1) > 0, prev, 0.0)
    first = halo - (taps - 1)
    for c in range(cw // LANES):
        cs = slice(c * LANES, (c + 1) * LANES)
        for r in range(ts // rc):
            acc = b_ref[:, cs]
            for rem in range(8):
                group = [(k, first + k - rem) for k in range(taps) if (first + k) % 8 == rem]
                if not group:
                    continue
                lo = min(a for _, a in group)
                hi = max(a for _, a in group)
                shifted[0:hi - lo + rc, :] = buf[r * rc + rem + lo:r * rc + rem + hi + rc, cs]
                for k, a in group:
                    acc = acc + w_ref[k:k + 1, cs] * shifted[a - lo:a - lo + rc, :]
            ybuf[r * rc:(r + 1) * rc, cs] = acc
    y = ybuf[...]
    mu = jnp.mean(y, axis=-1, keepdims=True)
    yc = y - mu
    var = jnp.mean(yc * yc, axis=-1, keepdims=True)
    yn = yc * lax.rsqrt(var + EPS) * lg_ref[...] + lb_ref[...]
    o_ref[...] = (yn * _sigmoid(yn)).astype(o_ref.dtype)


def _conv_branch(z, cw, dw_w, dw_b, ln_g, ln_b):
    b, s, _ = z.shape
    taps = dw_w.shape[0]
    halo = 32
    assert taps - 1 <= halo
    ts = _pick(s, 256)
    rc = _pick(ts, 128)
    hb = ts // halo
    kern = functools.partial(_conv_kernel, taps=taps, ts=ts, halo=halo, rc=rc)
    cur = lambda col: pl.BlockSpec((None, ts, cw), lambda bi, i: (bi, i, col))
    prev = lambda col: pl.BlockSpec((None, halo, cw), lambda bi, i: (bi, jnp.maximum(i * hb - 1, 0), col))
    vec = pl.BlockSpec((1, cw), lambda bi, i: (0, 0))
    return pl.pallas_call(
        kern,
        grid=(b, s // ts),
        in_specs=[cur(0), cur(1), prev(0), prev(1),
                  pl.BlockSpec((taps, cw), lambda bi, i: (0, 0)), vec, vec, vec],
        out_specs=pl.BlockSpec((None, ts, cw), lambda bi, i: (bi, i, 0)),
        out_shape=jax.ShapeDtypeStruct((b, s, cw), BF16),
        scratch_shapes=[pltpu.VMEM((halo + ts, cw), F32), pltpu.VMEM((ts, cw), F32),
                        pltpu.VMEM((rc + halo, LANES), F32)],
        compiler_params=_params("parallel", "parallel"),
        name="conv_branch",
    )(z, z, z, z, dw_w, dw_b.reshape(1, cw), ln_g.reshape(1, cw), ln_b.reshape(1, cw))


def _shift_rows(v, d, fill):
    if d % 8 == 0:
        return jnp.concatenate([jnp.full((d, v.shape[1]), fill, v.dtype), v[:v.shape[0] - d]], axis=0)
    rows = lax.broadcasted_iota(jnp.int32, v.shape, 0)
    return jnp.where(rows >= d, pltpu.roll(v, d, axis=0), fill)


def _scan_linear(a, bv):
    d = 1
    while d < a.shape[0]:
        bv = a * _shift_rows(bv, d, 0.0) + bv
        a = a * _shift_rows(a, d, 1.0)
        d *= 2
    return a, bv


def _scan_sum(v):
    d = 1
    while d < v.shape[0]:
        v = v + _shift_rows(v, d, 0.0)
        d *= 2
    return v


def _cum_kernel(zf_ref, fb_ref, o_ref, car, *, ts, rs):
    @pl.when(pl.program_id(1) == 0)
    def _():
        car[...] = jnp.zeros_like(car)

    h = car[...]
    for r in range(ts // rs):
        x = zf_ref[r * rs:(r + 1) * rs, :] + fb_ref[...]
        log_f = jnp.minimum(x, 0.0) - jnp.log1p(jnp.exp(-jnp.abs(x)))
        v = _scan_sum(log_f) + h
        o_ref[r * rs:(r + 1) * rs, :] = v
        h = v[rs - 1:rs, :]
    car[...] = h


def _forget_cumsum(zf, f_bias_pad):
    b, s, _ = zf.shape
    ts = _pick(s, 512)
    rs = _pick(ts, 128)
    return pl.pallas_call(
        functools.partial(_cum_kernel, ts=ts, rs=rs),
        grid=(b, s // ts),
        in_specs=[pl.BlockSpec((None, ts, LANES), lambda bi, i: (bi, i, 0)),
                  pl.BlockSpec((1, LANES), lambda bi, i: (0, 0))],
        out_specs=pl.BlockSpec((None, ts, LANES), lambda bi, i: (bi, i, 0)),
        out_shape=jax.ShapeDtypeStruct((b, s, LANES), F32),
        scratch_shapes=[pltpu.VMEM((1, LANES), F32)],
        compiler_params=_params("parallel", "arbitrary"),
        name="forget_cumsum",
    )(zf, f_bias_pad)


def _gelu_tanh(x):
    return 0.5 * x * (1.0 + jnp.tanh(0.7978845608028654 * (x + 0.044715 * (x * x * x))))


def _lru_kernel(u_ref, uh_ref, gate_ref, cw_ref, cb_ref, wr_ref, br_ref, wi_ref, bi_ref, lam_ref, og_ref,
                o_ref, ubuf, abuf, bbuf, car, *, taps, ts, halo, rs):
    lw = u_ref.shape[-1]

    @pl.when(pl.program_id(1) == 0)
    def _():
        car[...] = jnp.zeros_like(car)

    ubuf[halo:halo + ts, :] = u_ref[...]
    ubuf[0:halo, :] = jnp.where(pl.program_id(1) > 0, uh_ref[...], 0.0)
    xc = cb_ref[...]
    for k in range(taps):
        off = halo - (taps - 1) + k
        xc = xc + cw_ref[k:k + 1, :] * ubuf[off:off + ts, :]
    xb = xc.astype(BF16)
    r = _sigmoid(jnp.dot(xb, wr_ref[...], preferred_element_type=F32) + br_ref[...])
    ig = _sigmoid(jnp.dot(xb, wi_ref[...], preferred_element_type=F32) + bi_ref[...])
    nl = -lam_ref[...]
    softplus = jnp.maximum(nl, 0.0) + jnp.log1p(jnp.exp(-jnp.abs(nl)))
    log_a = (-LRU_C) * r * softplus
    th = jnp.tanh(log_a)
    abuf[...] = jnp.exp(log_a)
    bbuf[...] = jnp.sqrt(-2.0 * th / (1.0 - th)) * (ig * xc)
    for c in range(lw // LANES):
        cs = slice(c * LANES, (c + 1) * LANES)
        h = car[:, cs]
        for q in range(ts // rs):
            rows = slice(q * rs, (q + 1) * rs)
            a_cum, b_cum = _scan_linear(abuf[rows, cs], bbuf[rows, cs])
            hv = b_cum + a_cum * h
            bbuf[rows, cs] = hv
            h = hv[rs - 1:rs, :]
        car[:, cs] = h
    y = bbuf[...] * _gelu_tanh(gate_ref[...])
    y = y * lax.rsqrt(jnp.mean(y * y, axis=-1, keepdims=True) + EPS) * og_ref[...]
    o_ref[...] = y.astype(o_ref.dtype)


def _lru_branch(z, lw, col_x, conv_w, conv_b, wr_bd, b_r, wi_bd, b_i, lam, out_g):
    b, s, _ = z.shape
    taps = conv_w.shape[0]
    halo = 8
    assert taps - 1 <= halo
    ts = _pick(s, 256)
    rs = _pick(ts, 128)
    hb = ts // halo
    kern = functools.partial(_lru_kernel, taps=taps, ts=ts, halo=halo, rs=rs)
    vec = pl.BlockSpec((1, lw), lambda bi, i: (0, 0))
    mat = pl.BlockSpec((lw, lw), lambda bi, i: (0, 0))
    return pl.pallas_call(
        kern,
        grid=(b, s // ts),
        in_specs=[
            pl.BlockSpec((None, ts, lw), lambda bi, i: (bi, i, col_x)),
            pl.BlockSpec((None, halo, lw), lambda bi, i: (bi, jnp.maximum(i * hb - 1, 0), col_x)),
            pl.BlockSpec((None, ts, lw), lambda bi, i: (bi, i, col_x + 1)),
            pl.BlockSpec((taps, lw), lambda bi, i: (0, 0)), vec, mat, vec, mat, vec, vec, vec,
        ],
        out_specs=pl.BlockSpec((None, ts, lw), lambda bi, i: (bi, i, 0)),
        out_shape=jax.ShapeDtypeStruct((b, s, lw), BF16),
        scratch_shapes=[pltpu.VMEM((halo + ts, lw), F32), pltpu.VMEM((ts, lw), F32),
                        pltpu.VMEM((ts, lw), F32), pltpu.VMEM((1, lw), F32)],
        compiler_params=_params("parallel", "arbitrary"),
        name="lru_branch",
    )(z, z, z, conv_w, conv_b.reshape(1, lw), wr_bd, b_r.reshape(1, lw), wi_bd, b_i.reshape(1, lw),
      lam.reshape(1, lw), out_g.reshape(1, lw))


def _split3(t):
    hi = t.astype(BF16).astype(F32)
    r = t - hi
    mid = r.astype(BF16).astype(F32)
    return hi, mid, r - mid


def _attn_kernel(q_ref, k_ref, v_ref, cum_ref, o_ref, kaug, vaug, qaug, s_a, s_b, p_a, p_b, al_a, al_b, m_ref, acc_ref,
                 *, tq, tk, scale):
    h = pl.program_id(1)
    i = pl.program_id(2)
    s_len, hd = k_ref.shape
    inv_scale = 1.0 / scale
    c_exp = scale * 1.4426950408889634
    s_buf, p_buf, al_buf = (s_a, s_b), (p_a, p_b), (al_a, al_b)

    def head_cum(start, size):
        lane = lax.broadcasted_iota(jnp.int32, (size, LANES), 1)
        rows = pl.ds(pl.multiple_of(start, size), size)
        return jnp.sum(jnp.where(lane == h, cum_ref[rows, :], 0.0), axis=-1, keepdims=True) * inv_scale

    def key_rows(j):
        return pl.ds(pl.multiple_of(j * tk, tk), tk)

    @pl.when(i == 0)
    def _():
        lane = lax.broadcasted_iota(jnp.int32, (tk, LANES), 1)

        def body(t, carry):
            rows = key_rows(t)
            hi, mid, lo = _split3(-head_cum(t * tk, tk))
            aug = jnp.where(lane < 3, 1.0, jnp.where(lane == 3, hi, jnp.where(lane == 4, mid,
                            jnp.where(lane == 5, lo, 0.0))))
            kaug[t, 0:hd, :] = k_ref[rows, :].T.astype(BF16)
            kaug[t, hd:2 * hd, :] = aug.T.astype(BF16)
            vaug[rows, 0:hd] = v_ref[rows, :].astype(BF16)
            vaug[rows, hd:2 * hd] = jnp.where(lane == 0, 1.0, 0.0).astype(BF16)
            return carry

        lax.fori_loop(0, s_len // tk, body, 0)

    lane_q = lax.broadcasted_iota(jnp.int32, (tq, LANES), 1)
    hi, mid, lo = _split3(head_cum(i * tq, tq))
    qa = jnp.where(lane_q == 0, hi, jnp.where(lane_q == 1, mid, jnp.where(lane_q == 2, lo,
                   jnp.where(lane_q < 6, 1.0, 0.0))))
    qaug[:, 0:hd] = q_ref[...].astype(BF16)
    qaug[:, hd:2 * hd] = qa.astype(BF16)

    def scores_into(dst, j, masked=False):
        s = jnp.dot(qaug[...], kaug[j], preferred_element_type=F32)
        if masked:
            row = lax.broadcasted_iota(jnp.int32, s.shape, 0)
            col = lax.broadcasted_iota(jnp.int32, s.shape, 1)
            s = jnp.where(col <= row, s, -jnp.inf)
        dst[...] = s

    def softmax_into(src, dst_p, dst_alpha):
        s = src[...]
        m_old = m_ref[...]
        m_new = jnp.maximum(m_old, jnp.max(s, axis=-1, keepdims=True))
        dst_alpha[...] = jnp.exp2((m_old - m_new) * c_exp)
        m_ref[...] = m_new
        dst_p[...] = jnp.exp2((s - jnp.concatenate([m_new] * (tk // LANES), axis=1)) * c_exp).astype(BF16)

    def accumulate(src_alpha, src_p, j):
        alpha = jnp.concatenate([src_alpha[...]] * (acc_ref.shape[1] // LANES), axis=1)
        acc_ref[...] = alpha * acc_ref[...] + jnp.dot(src_p[...], vaug[key_rows(j), :], preferred_element_type=F32)

    def blk(t):
        return jnp.where(t == 0, i, jnp.maximum(t - 1, 0))

    def tick(t, par):
        scores_into(s_buf[1 - par], t)
        softmax_into(s_buf[par], p_buf[par], al_buf[par])
        accumulate(al_buf[1 - par], p_buf[1 - par], blk(t - 1))

    scores_into(s_buf[0], i, masked=True)
    m_ref[...] = jnp.full(m_ref.shape, -jnp.inf, F32)
    acc_ref[...] = jnp.zeros(acc_ref.shape, F32)
    p_buf[1][...] = jnp.zeros(p_buf[1].shape, BF16)
    al_buf[1][...] = jnp.ones(al_buf[1].shape, F32)

    def body(u, carry):
        tick(2 * u, 0)
        tick(2 * u + 1, 1)
        return carry

    lax.fori_loop(0, i // 2, body, 0)
    odd = i % 2 == 1

    @pl.when(odd)
    def _():
        tick(i - 1, 0)

    def drain(par):
        softmax_into(s_buf[par], p_buf[par], al_buf[par])
        accumulate(al_buf[1 - par], p_buf[1 - par], blk(i - 1))
        accumulate(al_buf[par], p_buf[par], blk(i))

    @pl.when(odd)
    def _():
        drain(1)

    @pl.when(jnp.logical_not(odd))
    def _():
        drain(0)

    o_ref[...] = acc_ref[:, 0:hd] / acc_ref[:, hd:hd + 1]


def _attention(z, cum, heads, hd, col_q, col_k, col_v):
    b, s, _ = z.shape
    tq = _pick(s, 512)
    tk = tq
    kern = functools.partial(_attn_kernel, tq=tq, tk=tk, scale=hd ** -0.5)
    return pl.pallas_call(
        kern,
        grid=(b, heads, s // tq),
        in_specs=[
            pl.BlockSpec((None, tq, hd), lambda bi, h, i: (bi, i, col_q + h)),
            pl.BlockSpec((None, s, hd), lambda bi, h, i: (bi, 0, col_k + h)),
            pl.BlockSpec((None, s, hd), lambda bi, h, i: (bi, 0, col_v + h)),
            pl.BlockSpec((None, s, LANES), lambda bi, h, i: (bi, 0, 0)),
        ],
        out_specs=pl.BlockSpec((None, tq, hd), lambda bi, h, i: (bi, i, h)),
        out_shape=jax.ShapeDtypeStruct((b, s, heads * hd), F32),
        scratch_shapes=[pltpu.VMEM((s // tk, 2 * hd, tk), BF16), pltpu.VMEM((s, 2 * hd), BF16),
                        pltpu.VMEM((tq, 2 * hd), BF16),
                        pltpu.VMEM((tq, tk), F32), pltpu.VMEM((tq, tk), F32),
                        pltpu.VMEM((tq, tk), BF16), pltpu.VMEM((tq, tk), BF16),
                        pltpu.VMEM((tq, LANES), F32), pltpu.VMEM((tq, LANES), F32), pltpu.VMEM((tq, LANES), F32),
                        pltpu.VMEM((tq, 2 * hd), F32)],
        compiler_params=_params("arbitrary", "arbitrary", "arbitrary"),
        name="fox_attention",
    )(z, z, z, cum)


def _outproj_kernel(yc_ref, o_ref, yl_ref, fg_ref, w1_ref, w2_ref, w3_ref, x_ref, g1_ref, out_ref, yf_scr):
    @pl.when(pl.program_id(2) == 0)
    def _():
        o = o_ref[...]
        yf = o * lax.rsqrt(jnp.mean(o * o, axis=-1, keepdims=True) + EPS) * fg_ref[...]
        yf_scr[...] = yf.astype(BF16)

    y = jnp.dot(yc_ref[...], w1_ref[...], preferred_element_type=F32)
    y = y + jnp.dot(yf_scr[...], w2_ref[...], preferred_element_type=F32)
    y = y + jnp.dot(yl_ref[...], w3_ref[...], preferred_element_type=F32)
    out_ref[...] = x_ref[...] + g1_ref[...] * y


def _outproj(y_conv, o_fox, y_lru, fox_g, w1, w2, w3, x, g1):
    b, s, d = x.shape
    cw, fw, lw = y_conv.shape[-1], o_fox.shape[-1], y_lru.shape[-1]
    tm = _pick(s, 512)
    tn = _pick(d, 1024)
    act = lambda w: pl.BlockSpec((None, tm, w), lambda bi, i, j: (bi, i, 0))
    wgt = lambda w: pl.BlockSpec((w, tn), lambda bi, i, j: (0, j))
    return pl.pallas_call(
        _outproj_kernel,
        grid=(b, s // tm, d // tn),
        in_specs=[act(cw), act(fw), act(lw), pl.BlockSpec((1, fw), lambda bi, i, j: (0, 0)),
                  wgt(cw), wgt(fw), wgt(lw),
                  pl.BlockSpec((None, tm, tn), lambda bi, i, j: (bi, i, j)),
                  pl.BlockSpec((None, 1, tn), lambda bi, i, j: (bi, 0, j))],
        out_specs=pl.BlockSpec((None, tm, tn), lambda bi, i, j: (bi, i, j)),
        out_shape=jax.ShapeDtypeStruct((b, s, d), F32),
        scratch_shapes=[pltpu.VMEM((tm, fw), BF16)],
        compiler_params=_params("parallel", "parallel", "arbitrary"),
        name="outproj",
    )(y_conv, o_fox, y_lru, fox_g.reshape(1, fw), w1, w2, w3, x, g1)


def _bf16_bits(v):
    return lax.bitcast_convert_type(v.astype(BF16).astype(F32), jnp.uint32)


def _pack_halves(v):
    half = v.shape[1] // 2
    word = _bf16_bits(v[:, half:]) | (_bf16_bits(v[:, :half]) >> 16)
    return lax.bitcast_convert_type(word, jnp.int32)


def _unpack_halves(w):
    word = lax.bitcast_convert_type(w, jnp.uint32)
    lo = lax.bitcast_convert_type(word << 16, F32)
    hi = lax.bitcast_convert_type(word & jnp.uint32(0xFFFF0000), F32)
    return lo, hi


def _router_kernel(x_ref, g_ref, sc_ref, sh_ref, whi_ref, wlo_ref, rb_ref, h2_ref, ri_ref, rg_ref, cnt_ref, car,
                   *, n_groups, per_group):
    @pl.when(jnp.logical_and(pl.program_id(0) == 0, pl.program_id(1) == 0))
    def _():
        car[...] = jnp.zeros_like(car)

    h2 = _modulated_norm(x_ref[...], g_ref[...], sc_ref[...], sh_ref[...])
    h2_ref[...] = _pack_halves(h2)
    hi = h2.astype(BF16)
    lo = (h2 - hi.astype(F32)).astype(BF16)
    logits = (jnp.dot(hi, whi_ref[...], preferred_element_type=F32)
              + jnp.dot(lo, whi_ref[...], preferred_element_type=F32)
              + jnp.dot(hi, wlo_ref[...], preferred_element_type=F32)) + rb_ref[...]
    lane_i = lax.broadcasted_iota(jnp.int32, logits.shape, 1)
    lane = lane_i.astype(F32)
    big = float(LANES)

    def first_lane(hit):
        return jnp.min(jnp.where(hit, lane, big), axis=-1, keepdims=True)

    gmask = lane_i < n_groups
    lg1 = jnp.where(gmask, logits, -jnp.inf)
    m1 = jnp.max(lg1, axis=-1, keepdims=True)
    e1 = jnp.exp(lg1 - m1)
    p1 = e1 / jnp.sum(e1, axis=-1, keepdims=True)
    p_grp = jnp.max(p1, axis=-1, keepdims=True)
    grp = first_lane((p1 == p_grp) & gmask)
    lo_lane = n_groups + grp * per_group
    emask = (lane >= lo_lane) & (lane < lo_lane + per_group)
    lg2 = jnp.where(emask, logits, -jnp.inf)
    m2 = jnp.max(lg2, axis=-1, keepdims=True)
    e2 = jnp.exp(lg2 - m2)
    p2 = e2 / jnp.sum(e2, axis=-1, keepdims=True)
    v1 = jnp.max(jnp.where(emask, p2, -1.0), axis=-1, keepdims=True)
    i1 = first_lane((p2 == v1) & emask)
    rest = emask & (lane != i1)
    v2 = jnp.max(jnp.where(rest, p2, -1.0), axis=-1, keepdims=True)
    i2 = first_lane((p2 == v2) & rest)
    denom = v1 + v2
    gate1 = p_grp * (v1 / denom)
    gate2 = p_grp * (v2 / denom)
    tm = logits.shape[0]
    picked = jnp.where((lane == i1) | (lane == i2), 1.0, 0.0)
    tri = jnp.where(lax.broadcasted_iota(jnp.int32, (tm, tm), 0) > lax.broadcasted_iota(jnp.int32, (tm, tm), 1),
                    1.0, 0.0).astype(BF16)
    before = jnp.dot(tri, picked.astype(BF16), preferred_element_type=F32) + car[...]
    rank1 = jnp.sum(jnp.where(lane == i1, before, 0.0), axis=-1, keepdims=True)
    rank2 = jnp.sum(jnp.where(lane == i2, before, 0.0), axis=-1, keepdims=True)
    total = car[...] + jnp.sum(picked, axis=0, keepdims=True)
    car[...] = total
    cnt_ref[...] = total
    info = jnp.where(lane_i == 0, i1 - n_groups, jnp.where(lane_i == 1, i2 - n_groups,
                     jnp.where(lane_i == 2, rank1, jnp.where(lane_i == 3, rank2, 0.0))))
    ri_ref[...] = info.astype(jnp.int32)
    rg_ref[...] = jnp.where(lane_i == 0, gate1, jnp.where(lane_i == 1, gate2, 0.0))


def _router(x, g, sc, sh, w_hi, w_lo, r_bias, n_groups, per_group):
    b, s, d = x.shape
    tm = _pick(s, 512)
    kern = functools.partial(_router_kernel, n_groups=n_groups, per_group=per_group)
    row = lambda w: pl.BlockSpec((None, tm, w), lambda bi, i: (bi, i, 0))
    mod = pl.BlockSpec((None, 1, d), lambda bi, i: (bi, 0, 0))
    wsp = pl.BlockSpec((d, LANES), lambda bi, i: (0, 0))
    return pl.pallas_call(
        kern,
        grid=(b, s // tm),
        in_specs=[row(d), pl.BlockSpec((1, d), lambda bi, i: (0, 0)), mod, mod, wsp, wsp,
                  pl.BlockSpec((1, LANES), lambda bi, i: (0, 0))],
        out_specs=[row(d // 2), row(LANES), row(LANES), pl.BlockSpec((1, LANES), lambda bi, i: (0, 0))],
        out_shape=[jax.ShapeDtypeStruct((b, s, d // 2), jnp.int32), jax.ShapeDtypeStruct((b, s, LANES), jnp.int32),
                   jax.ShapeDtypeStruct((b, s, LANES), F32), jax.ShapeDtypeStruct((1, LANES), F32)],
        scratch_shapes=[pltpu.VMEM((1, LANES), F32)],
        compiler_params=_params("arbitrary", "arbitrary"),
        name="router",
    )(x, g, sc, sh, w_hi, w_lo, r_bias)


def _moe_kernel(bexp_ref, rtok_ref, wslot_ref, nexte_ref, h2_hbm, wg_hbm, wu_hbm, wd_hbm, y_ref,
                xbuf_a, xbuf_b, sem, wg_bf, wu_bf, wd_bf, wg_raw, wu_raw, wd_raw, wsem, *, rows, layer):
    b = pl.program_id(0)
    nb = pl.num_programs(0)
    xbufs = (xbuf_a, xbuf_b)

    def row_copy(tok, r, slot):
        return pltpu.make_async_copy(h2_hbm.at[pl.ds(tok, 1)], xbufs[slot].at[pl.ds(r, 1)], sem.at[slot])

    def wait_block(slot):
        pltpu.make_async_copy(h2_hbm.at[pl.ds(0, rows)], xbufs[slot], sem.at[slot]).wait()

    @pl.when(b == 0)
    def _():
        def body(r, carry):
            row_copy(rtok_ref[r], r, 0).start()
            return carry

        lax.fori_loop(0, rows, body, 0)

    def weight_copies(e, s):
        return [pltpu.make_async_copy(src.at[layer, e], dst.at[s], wsem.at[s])
                for src, dst in ((wg_hbm, wg_raw), (wu_hbm, wu_raw), (wd_hbm, wd_raw))]

    @pl.when(b == 0)
    def _():
        for copy in weight_copies(bexp_ref[0], 0):
            copy.start()

    changed = jnp.logical_or(b == 0, bexp_ref[b] != bexp_ref[jnp.maximum(b - 1, 0)])

    @pl.when(changed)
    def _():
        s = wslot_ref[b]
        for copy in weight_copies(bexp_ref[b], s):
            copy.wait()
        for static_s in range(2):
            @pl.when(s == static_s)
            def _():
                wg_bf[...] = wg_raw[static_s].astype(BF16)
                wu_bf[...] = wu_raw[static_s].astype(BF16)
                wd_bf[...] = wd_raw[static_s].astype(BF16)

        nxt = nexte_ref[b]

        @pl.when(nxt >= 0)
        def _():
            for copy in weight_copies(nxt, 1 - s):
                copy.start()

    def block(slot):
        wait_block(slot)
        base = jnp.minimum(b + 1, nb - 1) * rows
        for r in range(rows):
            row_copy(rtok_ref[base + r], r, 1 - slot).start(priority=r % 2)
        lo, hi = _unpack_halves(xbufs[slot][...])
        xb = jnp.concatenate([lo.astype(BF16), hi.astype(BF16)], axis=1)
        gt = jnp.dot(xb, wg_bf[...], preferred_element_type=F32)
        up = jnp.dot(xb, wu_bf[...], preferred_element_type=F32)
        hb = (gt * _sigmoid(gt) * up).astype(BF16)
        y_ref[...] = _pack_halves(jnp.dot(hb, wd_bf[...], preferred_element_type=F32))

        @pl.when(b == nb - 1)
        def _():
            wait_block(1 - slot)

    @pl.when(b % 2 == 0)
    def _():
        block(0)

    @pl.when(b % 2 == 1)
    def _():
        block(1)


def _moe_experts(block_exp, row_tok, h2_rows, w_gate, w_up, w_down, layer):
    _, _, d, f = w_gate.shape
    rows = MOE_ROWS
    n_blocks = block_exp.shape[0]
    changed = jnp.concatenate([jnp.ones((1,), jnp.int32), (block_exp[1:] != block_exp[:-1]).astype(jnp.int32)])
    w_slot = (jnp.cumsum(changed) - 1) % 2
    n_le = jnp.sum((block_exp[None, :] <= block_exp[:, None]).astype(jnp.int32), axis=1)
    next_exp = jnp.where(n_le < n_blocks, block_exp[jnp.minimum(n_le, n_blocks - 1)], -1)
    any_spec = pl.BlockSpec(memory_space=pl.ANY)
    grid_spec = pltpu.PrefetchScalarGridSpec(
        num_scalar_prefetch=4,
        grid=(n_blocks,),
        in_specs=[any_spec, any_spec, any_spec, any_spec],
        out_specs=pl.BlockSpec((rows, d // 2), lambda bi, *_: (bi, 0)),
        scratch_shapes=[pltpu.VMEM((rows, d // 2), jnp.int32), pltpu.VMEM((rows, d // 2), jnp.int32),
                        pltpu.SemaphoreType.DMA((2,)),
                        pltpu.VMEM((d, f), BF16), pltpu.VMEM((d, f), BF16), pltpu.VMEM((f, d), BF16),
                        pltpu.VMEM((2, d, f), F32), pltpu.VMEM((2, d, f), F32), pltpu.VMEM((2, f, d), F32),
                        pltpu.SemaphoreType.DMA((2,))],
    )
    return pl.pallas_call(
        functools.partial(_moe_kernel, rows=rows, layer=layer),
        grid_spec=grid_spec,
        out_shape=jax.ShapeDtypeStruct((n_blocks * rows, d // 2), jnp.int32),
        compiler_params=_params("arbitrary"),
        name="moe_experts",
    )(block_exp, row_tok, w_slot.astype(jnp.int32), next_exp.astype(jnp.int32), h2_rows, w_gate, w_up, w_down)


def _combine_kernel(dest_ref, y_hbm, x_ref, rg_ref, g2_ref, fg_ref, out_ref, ybuf_a, ybuf_b, sem, *, tm, final_norm):
    i = pl.program_id(0)
    n = pl.num_programs(0)
    ybufs = (ybuf_a, ybuf_b)

    def start(row, r, k, slot):
        copy = pltpu.make_async_copy(y_hbm.at[pl.ds(row, 1)], ybufs[slot].at[k, pl.ds(r, 1)], sem.at[slot])
        copy.start(priority=k % 2)

    def wait_tile(slot):
        for k in range(TOP_K):
            pltpu.make_async_copy(y_hbm.at[pl.ds(0, tm)], ybufs[slot].at[k], sem.at[slot]).wait()

    @pl.when(i == 0)
    def _():
        def body(r, carry):
            for k in range(TOP_K):
                start(dest_ref[r * TOP_K + k], r, k, 0)
            return carry

        lax.fori_loop(0, tm, body, 0)

    def tile(slot):
        wait_tile(slot)
        base = jnp.minimum(i + 1, n - 1) * (tm * TOP_K)
        for r in range(tm):
            for k in range(TOP_K):
                start(dest_ref[base + r * TOP_K + k], r, k, 1 - slot)
        gates = rg_ref[...]
        gate = [jnp.broadcast_to(gates[:, k:k + 1], (tm, LANES)) for k in range(TOP_K)]
        half = out_ref.shape[1] // 2
        for j in range(half // LANES):
            words = slice(j * LANES, (j + 1) * LANES)
            moe_lo = moe_hi = None
            for k in range(TOP_K):
                lo, hi = _unpack_halves(ybufs[slot][k, :, words])
                moe_lo = lo * gate[k] if moe_lo is None else moe_lo + lo * gate[k]
                moe_hi = hi * gate[k] if moe_hi is None else moe_hi + hi * gate[k]
            for cols, moe in ((words, moe_lo), (slice(half + j * LANES, half + (j + 1) * LANES), moe_hi)):
                out_ref[:, cols] = x_ref[:, cols] + g2_ref[:, cols] * moe
        if final_norm:
            v = out_ref[...]
            out_ref[...] = v * lax.rsqrt(jnp.mean(v * v, axis=-1, keepdims=True) + EPS) * fg_ref[...]

        @pl.when(i == n - 1)
        def _():
            wait_tile(1 - slot)

    @pl.when(i % 2 == 0)
    def _():
        tile(0)

    @pl.when(i % 2 == 1)
    def _():
        tile(1)


def _combine(dest, y, x_flat, rg_flat, g2, final_g, seq, final_norm):
    t, d = x_flat.shape
    tm = _pick(seq, 128)
    grid_spec = pltpu.PrefetchScalarGridSpec(
        num_scalar_prefetch=1,
        grid=(t // tm,),
        in_specs=[
            pl.BlockSpec(memory_space=pl.ANY),
            pl.BlockSpec((tm, d), lambda i, de: (i, 0)),
            pl.BlockSpec((tm, LANES), lambda i, de: (i, 0)),
            pl.BlockSpec((None, 1, d), lambda i, de: ((i * tm) // seq, 0, 0)),
            pl.BlockSpec((1, d), lambda i, de: (0, 0)),
        ],
        out_specs=pl.BlockSpec((tm, d), lambda i, de: (i, 0)),
        scratch_shapes=[pltpu.VMEM((TOP_K, tm, d // 2), jnp.int32), pltpu.VMEM((TOP_K, tm, d // 2), jnp.int32),
                        pltpu.SemaphoreType.DMA((2,))],
    )
    return pl.pallas_call(
        functools.partial(_combine_kernel, tm=tm, final_norm=final_norm),
        grid_spec=grid_spec,
        out_shape=jax.ShapeDtypeStruct((t, d), F32),
        compiler_params=_params("arbitrary"),
        name="moe_combine",
    )(dest, y, x_flat, rg_flat, g2, final_g.reshape(1, d))


def _routing_tables(eid, rank, counts, rows):
    n_assign = eid.shape[0]
    n_experts = counts.shape[0]
    padded = (counts + rows - 1) // rows * rows
    pends = jnp.cumsum(padded)
    pstarts = pends - padded
    onehot = eid[:, None] == jnp.arange(n_experts, dtype=jnp.int32)[None, :]
    dest = (jnp.sum(jnp.where(onehot, pstarts[None, :], 0), axis=1) + rank).astype(jnp.int32)
    n_blocks = -(-n_assign // rows) + n_experts
    row_tok = jnp.zeros((n_blocks * rows,), jnp.int32).at[dest].set(jnp.arange(n_assign, dtype=jnp.int32) // TOP_K)
    first_row = jnp.arange(n_blocks, dtype=jnp.int32) * rows
    block_exp = jnp.minimum(jnp.sum((pends[None, :] <= first_row[:, None]).astype(jnp.int32), axis=1), n_experts - 1)
    return block_exp.astype(jnp.int32), row_tok, dest


def _block_diag(w):
    nb, bw, _ = w.shape
    eye = jnp.eye(nb, dtype=w.dtype)
    return (eye[:, None, :, None] * w[:, :, None, :]).reshape(nb * bw, nb * bw)


def kernel(x, c, w_ada, b_ada, ln1_g, w_in, conv_dw_w, conv_dw_b, conv_ln_g, conv_ln_b, fox_f_bias, fox_out_g, lru_conv_w, lru_conv_b, lru_w_r, lru_b_r, lru_w_i, lru_b_i, lru_lambda, lru_out_g, w_out, ln2_g, w_router_group, b_router_group, w_router_expert, b_router_expert, w_gate, w_up, w_down, final_g):
    b, s, d = x.shape
    depth = w_ada.shape[0]
    cw = conv_dw_b.shape[-1]
    fw = fox_out_g.shape[-1]
    lw = lru_lambda.shape[-1]
    heads = fox_f_bias.shape[-1]
    hd = fw // heads
    n_groups, per_group = b_router_expert.shape[1:]
    n_experts = n_groups * per_group
    off_f = 2 * cw + 3 * fw
    assert cw % LANES == 0 and lw == cw and fw % cw == 0 and hd == LANES and heads <= LANES
    assert n_groups + n_experts <= LANES

    c_pad = jnp.zeros((8, d), F32).at[:b].set(c)
    mod_all = _ada_mod(c_pad, w_ada, b_ada)

    for l in range(depth):
        mod = mod_all[l, :b].reshape(b, 1, 6 * d)
        sh1, sc1, g1, sh2, sc2, g2 = [mod[..., k * d:(k + 1) * d] for k in range(6)]

        w_l = w_in[l]
        w_main = jnp.concatenate([w_l[:, :off_f], w_l[:, off_f + heads:]], axis=1).astype(BF16)
        w_f = jnp.zeros((d, LANES), BF16).at[:, :heads].set(w_l[:, off_f:off_f + heads].astype(BF16))
        z, zf = _inproj(x, ln1_g[l].reshape(1, d), sc1, sh1, w_main, w_f)

        y_conv = _conv_branch(z, cw, conv_dw_w[l], conv_dw_b[l], conv_ln_g[l], conv_ln_b[l])

        f_bias = jnp.zeros((1, LANES), F32).at[0, :heads].set(fox_f_bias[l])
        cum = _forget_cumsum(zf, f_bias)
        o_fox = _attention(z, cum, heads, hd, 2 * cw // hd, (2 * cw + fw) // hd, (2 * cw + 2 * fw) // hd)

        y_lru = _lru_branch(z, lw, off_f // lw, lru_conv_w[l], lru_conv_b[l],
                            _block_diag(lru_w_r[l]).astype(BF16), lru_b_r[l],
                            _block_diag(lru_w_i[l]).astype(BF16), lru_b_i[l], lru_lambda[l], lru_out_g[l])

        wo = w_out[l].astype(BF16)
        x = _outproj(y_conv, o_fox, y_lru, fox_out_g[l], wo[:cw], wo[cw:cw + fw], wo[cw + fw:], x, g1)

        w_r = jnp.concatenate(
            [w_router_group[l], w_router_expert[l].transpose(1, 0, 2).reshape(d, n_experts)], axis=1)
        w_r = jnp.zeros((d, LANES), F32).at[:, :n_groups + n_experts].set(w_r)
        w_hi = w_r.astype(BF16)
        w_lo = (w_r - w_hi.astype(F32)).astype(BF16)
        r_bias = jnp.zeros((1, LANES), F32).at[0, :n_groups + n_experts].set(
            jnp.concatenate([b_router_group[l], b_router_expert[l].reshape(-1)]))
        h2, ri, rg, cnt = _router(x, ln2_g[l].reshape(1, d), sc2, sh2, w_hi, w_lo, r_bias, n_groups, per_group)

        eid = ri[:, :, :TOP_K].reshape(-1)
        rank = ri[:, :, TOP_K:2 * TOP_K].reshape(-1)
        counts = cnt[0, n_groups:n_groups + n_experts].astype(jnp.int32)
        block_exp, row_tok, dest = _routing_tables(eid, rank, counts, MOE_ROWS)
        y = _moe_experts(block_exp, row_tok, h2.reshape(b * s, d // 2), w_gate, w_up, w_down, l)
        x = _combine(dest, y, x.reshape(b * s, d), rg.reshape(b * s, LANES), g2, final_g, s,
                     final_norm=(l == depth - 1)).reshape(b, s, d)

    return x
```

```python
import functools

import jax
import jax.numpy as jnp
from jax import lax
from jax.experimental import pallas as pl
from jax.experimental.pallas import tpu as pltpu

EPS = 1e-6
LRU_C = 8.0
TOP_K = 2
LANES = 128
MOE_ROWS = 128
VMEM_LIMIT = 56 * 1024 * 1024

F32 = jnp.float32
BF16 = jnp.bfloat16


def _pick(n, pref):
    if n <= pref:
        return n
    for t in range(pref, 7, -1):
        if n % t == 0 and t % 8 == 0:
            return t
    raise ValueError((n, pref))


def _params(*sem):
    return pltpu.CompilerParams(dimension_semantics=sem, vmem_limit_bytes=VMEM_LIMIT)


def _sigmoid(x):
    return jax.nn.sigmoid(x)


def _ada_kernel(c_ref, w_ref, b_ref, o_ref):
    c = c_ref[...]
    cond = (c * _sigmoid(c)).astype(BF16)
    o_ref[...] = jnp.dot(cond, w_ref[...].astype(BF16), preferred_element_type=F32) + b_ref[...]


def _ada_mod(c_pad, w_ada, b_ada):
    depth, d, n = w_ada.shape
    rows = c_pad.shape[0]
    tn = _pick(n, 1024)
    return pl.pallas_call(
        _ada_kernel,
        grid=(depth, n // tn),
        in_specs=[
            pl.BlockSpec((rows, d), lambda l, j: (0, 0)),
            pl.BlockSpec((None, d, tn), lambda l, j: (l, 0, j)),
            pl.BlockSpec((None, 1, tn), lambda l, j: (l, 0, j)),
        ],
        out_specs=pl.BlockSpec((None, rows, tn), lambda l, j: (l, 0, j)),
        out_shape=jax.ShapeDtypeStruct((depth, rows, n), F32),
        compiler_params=_params("parallel", "parallel"),
        name="ada_mod",
    )(c_pad, w_ada, b_ada.reshape(depth, 1, n))


def _modulated_norm(x, g, sc, sh):
    y = x * lax.rsqrt(jnp.mean(x * x, axis=-1, keepdims=True) + EPS) * g
    return y * (1.0 + sc) + sh


def _inproj_kernel(x_ref, g_ref, sc_ref, sh_ref, w_ref, wf_ref, z_ref, zf_ref, h_scr):
    @pl.when(pl.program_id(2) == 0)
    def _():
        hb = _modulated_norm(x_ref[...], g_ref[...], sc_ref[...], sh_ref[...]).astype(BF16)
        h_scr[...] = hb
        zf_ref[...] = jnp.dot(hb, wf_ref[...], preferred_element_type=F32)

    z_ref[...] = jnp.dot(h_scr[...], w_ref[...], preferred_element_type=F32)


def _inproj(x, g, sc, sh, w_main, w_f):
    b, s, d = x.shape
    n = w_main.shape[1]
    tm = _pick(s, 1024)
    tn = _pick(n, 1024)
    return pl.pallas_call(
        _inproj_kernel,
        grid=(b, s // tm, n // tn),
        in_specs=[
            pl.BlockSpec((None, tm, d), lambda bi, i, j: (bi, i, 0)),
            pl.BlockSpec((1, d), lambda bi, i, j: (0, 0)),
            pl.BlockSpec((None, 1, d), lambda bi, i, j: (bi, 0, 0)),
            pl.BlockSpec((None, 1, d), lambda bi, i, j: (bi, 0, 0)),
            pl.BlockSpec((d, tn), lambda bi, i, j: (0, j)),
            pl.BlockSpec((d, LANES), lambda bi, i, j: (0, 0)),
        ],
        out_specs=[
            pl.BlockSpec((None, tm, tn), lambda bi, i, j: (bi, i, j)),
            pl.BlockSpec((None, tm, LANES), lambda bi, i, j: (bi, i, 0)),
        ],
        out_shape=[jax.ShapeDtypeStruct((b, s, n), F32), jax.ShapeDtypeStruct((b, s, LANES), F32)],
        scratch_shapes=[pltpu.VMEM((tm, d), BF16)],
        compiler_params=_params("parallel", "parallel", "arbitrary"),
        name="inproj",
    )(x, g, sc, sh, w_main, w_f)


def _conv_kernel(a_ref, g_ref, ah_ref, gh_ref, w_ref, b_ref, lg_ref, lb_ref, o_ref, buf, ybuf, shifted,
                 *, taps, ts, halo, rc):
    cw = a_ref.shape[-1]
    buf[halo:halo + ts, :] = a_ref[...] * _sigmoid(g_ref[...])
    prev = ah_ref[...] * _sigmoid(gh_ref[...])
    buf[0:halo, :] = jnp.where(pl.program_id(1) > 0, prev, 0.0)
    first = halo - (taps - 1)
    for c in range(cw // LANES):
        cs = slice(c * LANES, (c + 1) * LANES)
        for r in range(ts // rc):
            acc = b_ref[:, cs]
            for rem in range(8):
                group = [(k, first + k - rem) for k in range(taps) if (first + k) % 8 == rem]
                if not group:
                    continue
                lo = min(a for _, a in group)
                hi = max(a for _, a in group)
                shifted[0:hi - lo + rc, :] = buf[r * rc + rem + lo:r * rc + rem + hi + rc, cs]
                for k, a in group:
                    acc = acc + w_ref[k:k + 1, cs] * shifted[a - lo:a - lo + rc, :]
            ybuf[r * rc:(r + 1) * rc, cs] = acc
    y = ybuf[...]
    mu = jnp.mean(y, axis=-1, keepdims=True)
    yc = y - mu
    var = jnp.mean(yc * yc, axis=-1, keepdims=True)
    yn = yc * lax.rsqrt(var + EPS) * lg_ref[...] + lb_ref[...]
    o_ref[...] = (yn * _sigmoid(yn)).astype(o_ref.dtype)


def _conv_branch(z, cw, dw_w, dw_b, ln_g, ln_b):
    b, s, _ = z.shape
    taps = dw_w.shape[0]
    halo = 32
    assert taps - 1 <= halo
    ts = _pick(s, 256)
    rc = _pick(ts, 128)
    hb = ts // halo
    kern = functools.partial(_conv_kernel, taps=taps, ts=ts, halo=halo, rc=rc)
    cur = lambda col: pl.BlockSpec((None, ts, cw), lambda bi, i: (bi, i, col))
    prev = lambda col: pl.BlockSpec((None, halo, cw), lambda bi, i: (bi, jnp.maximum(i * hb - 1, 0), col))
    vec = pl.BlockSpec((1, cw), lambda bi, i: (0, 0))
    return pl.pallas_call(
        kern,
        grid=(b, s // ts),
        in_specs=[cur(0), cur(1), prev(0), prev(1),
                  pl.BlockSpec((taps, cw), lambda bi, i: (0, 0)), vec, vec, vec],
        out_specs=pl.BlockSpec((None, ts, cw), lambda bi, i: (bi, i, 0)),
        out_shape=jax.ShapeDtypeStruct((b, s, cw), BF16),
        scratch_shapes=[pltpu.VMEM((halo + ts, cw), F32), pltpu.VMEM((ts, cw), F32),
                        pltpu.VMEM((rc + halo, LANES), F32)],
        compiler_params=_params("parallel", "parallel"),
        name="conv_branch",
    )(z, z, z, z, dw_w, dw_b.reshape(1, cw), ln_g.reshape(1, cw), ln_b.reshape(1, cw))


def _shift_rows(v, d, fill):
    if d % 8 == 0:
        return jnp.concatenate([jnp.full((d, v.shape[1]), fill, v.dtype), v[:v.shape[0] - d]], axis=0)
    rows = lax.broadcasted_iota(jnp.int32, v.shape, 0)
    return jnp.where(rows >= d, pltpu.roll(v, d, axis=0), fill)


def _scan_linear(a, bv):
    d = 1
    while d < a.shape[0]:
        bv = a * _shift_rows(bv, d, 0.0) + bv
        a = a * _shift_rows(a, d, 1.0)
        d *= 2
    return a, bv


def _scan_sum(v):
    d = 1
    while d < v.shape[0]:
        v = v + _shift_rows(v, d, 0.0)
        d *= 2
    return v


def _cum_kernel(zf_ref, fb_ref, o_ref, car, *, ts, rs):
    @pl.when(pl.program_id(1) == 0)
    def _():
        car[...] = jnp.zeros_like(car)

    h = car[...]
    for r in range(ts // rs):
        x = zf_ref[r * rs:(r + 1) * rs, :] + fb_ref[...]
        log_f = jnp.minimum(x, 0.0) - jnp.log1p(jnp.exp(-jnp.abs(x)))
        v = _scan_sum(log_f) + h
        o_ref[r * rs:(r + 1) * rs, :] = v
        h = v[rs - 1:rs, :]
    car[...] = h


def _forget_cumsum(zf, f_bias_pad):
    b, s, _ = zf.shape
    ts = _pick(s, 512)
    rs = _pick(ts, 128)
    return pl.pallas_call(
        functools.partial(_cum_kernel, ts=ts, rs=rs),
        grid=(b, s // ts),
        in_specs=[pl.BlockSpec((None, ts, LANES), lambda bi, i: (bi, i, 0)),
                  pl.BlockSpec((1, LANES), lambda bi, i: (0, 0))],
        out_specs=pl.BlockSpec((None, ts, LANES), lambda bi, i: (bi, i, 0)),
        out_shape=jax.ShapeDtypeStruct((b, s, LANES), F32),
        scratch_shapes=[pltpu.VMEM((1, LANES), F32)],
        compiler_params=_params("parallel", "arbitrary"),
        name="forget_cumsum",
    )(zf, f_bias_pad)


def _gelu_tanh(x):
    return 0.5 * x * (1.0 + jnp.tanh(0.7978845608028654 * (x + 0.044715 * (x * x * x))))


def _lru_kernel(u_ref, uh_ref, gate_ref, cw_ref, cb_ref, wr_ref, br_ref, wi_ref, bi_ref, lam_ref, og_ref,
                o_ref, ubuf, abuf, bbuf, car, *, taps, ts, halo, rs):
    lw = u_ref.shape[-1]

    @pl.when(pl.program_id(1) == 0)
    def _():
        car[...] = jnp.zeros_like(car)

    ubuf[halo:halo + ts, :] = u_ref[...]
    ubuf[0:halo, :] = jnp.where(pl.program_id(1) > 0, uh_ref[...], 0.0)
    xc = cb_ref[...]
    for k in range(taps):
        off = halo - (taps - 1) + k
        xc = xc + cw_ref[k:k + 1, :] * ubuf[off:off + ts, :]
    xb = xc.astype(BF16)
    r = _sigmoid(jnp.dot(xb, wr_ref[...], preferred_element_type=F32) + br_ref[...])
    ig = _sigmoid(jnp.dot(xb, wi_ref[...], preferred_element_type=F32) + bi_ref[...])
    nl = -lam_ref[...]
    softplus = jnp.maximum(nl, 0.0) + jnp.log1p(jnp.exp(-jnp.abs(nl)))
    log_a = (-LRU_C) * r * softplus
    th = jnp.tanh(log_a)
    abuf[...] = jnp.exp(log_a)
    bbuf[...] = jnp.sqrt(-2.0 * th / (1.0 - th)) * (ig * xc)
    for c in range(lw // LANES):
        cs = slice(c * LANES, (c + 1) * LANES)
        h = car[:, cs]
        for q in range(ts // rs):
            rows = slice(q * rs, (q + 1) * rs)
            a_cum, b_cum = _scan_linear(abuf[rows, cs], bbuf[rows, cs])
            hv = b_cum + a_cum * h
            bbuf[rows, cs] = hv
            h = hv[rs - 1:rs, :]
        car[:, cs] = h
    y = bbuf[...] * _gelu_tanh(gate_ref[...])
    y = y * lax.rsqrt(jnp.mean(y * y, axis=-1, keepdims=True) + EPS) * og_ref[...]
    o_ref[...] = y.astype(o_ref.dtype)


def _lru_branch(z, lw, col_x, conv_w, conv_b, wr_bd, b_r, wi_bd, b_i, lam, out_g):
    b, s, _ = z.shape
    taps = conv_w.shape[0]
    halo = 8
    assert taps - 1 <= halo
    ts = _pick(s, 256)
    rs = _pick(ts, 128)
    hb = ts // halo
    kern = functools.partial(_lru_kernel, taps=taps, ts=ts, halo=halo, rs=rs)
    vec = pl.BlockSpec((1, lw), lambda bi, i: (0, 0))
    mat = pl.BlockSpec((lw, lw), lambda bi, i: (0, 0))
    return pl.pallas_call(
        kern,
        grid=(b, s // ts),
        in_specs=[
            pl.BlockSpec((None, ts, lw), lambda bi, i: (bi, i, col_x)),
            pl.BlockSpec((None, halo, lw), lambda bi, i: (bi, jnp.maximum(i * hb - 1, 0), col_x)),
            pl.BlockSpec((None, ts, lw), lambda bi, i: (bi, i, col_x + 1)),
            pl.BlockSpec((taps, lw), lambda bi, i: (0, 0)), vec, mat, vec, mat, vec, vec, vec,
        ],
        out_specs=pl.BlockSpec((None, ts, lw), lambda bi, i: (bi, i, 0)),
        out_shape=jax.ShapeDtypeStruct((b, s, lw), BF16),
        scratch_shapes=[pltpu.VMEM((halo + ts, lw), F32), pltpu.VMEM((ts, lw), F32),
                        pltpu.VMEM((ts, lw), F32), pltpu.VMEM((1, lw), F32)],
        compiler_params=_params("parallel", "arbitrary"),
        name="lru_branch",
    )(z, z, z, conv_w, conv_b.reshape(1, lw), wr_bd, b_r.reshape(1, lw), wi_bd, b_i.reshape(1, lw),
      lam.reshape(1, lw), out_g.reshape(1, lw))


def _split3(t):
    hi = t.astype(BF16).astype(F32)
    r = t - hi
    mid = r.astype(BF16).astype(F32)
    return hi, mid, r - mid


def _attn_kernel(q_ref, k_ref, v_ref, cum_ref, o_ref, kaug, vaug, qaug, s_a, s_b, p_a, p_b, al_a, al_b, m_ref, acc_ref,
                 *, tq, tk, scale):
    h = pl.program_id(1)
    i = pl.program_id(2)
    s_len, hd = k_ref.shape
    inv_scale = 1.0 / scale
    c_exp = scale * 1.4426950408889634
    s_buf, p_buf, al_buf = (s_a, s_b), (p_a, p_b), (al_a, al_b)

    def head_cum(start, size):
        lane = lax.broadcasted_iota(jnp.int32, (size, LANES), 1)
        rows = pl.ds(pl.multiple_of(start, size), size)
        return jnp.sum(jnp.where(lane == h, cum_ref[rows, :], 0.0), axis=-1, keepdims=True) * inv_scale

    def key_rows(j):
        return pl.ds(pl.multiple_of(j * tk, tk), tk)

    @pl.when(i == 0)
    def _():
        lane = lax.broadcasted_iota(jnp.int32, (tk, LANES), 1)

        def body(t, carry):
            rows = key_rows(t)
            hi, mid, lo = _split3(-head_cum(t * tk, tk))
            aug = jnp.where(lane < 3, 1.0, jnp.where(lane == 3, hi, jnp.where(lane == 4, mid,
                            jnp.where(lane == 5, lo, 0.0))))
            kaug[t, 0:hd, :] = k_ref[rows, :].T.astype(BF16)
            kaug[t, hd:2 * hd, :] = aug.T.astype(BF16)
            vaug[rows, 0:hd] = v_ref[rows, :].astype(BF16)
            vaug[rows, hd:2 * hd] = jnp.where(lane == 0, 1.0, 0.0).astype(BF16)
            return carry

        lax.fori_loop(0, s_len // tk, body, 0)

    lane_q = lax.broadcasted_iota(jnp.int32, (tq, LANES), 1)
    hi, mid, lo = _split3(head_cum(i * tq, tq))
    qa = jnp.where(lane_q == 0, hi, jnp.where(lane_q == 1, mid, jnp.where(lane_q == 2, lo,
                   jnp.where(lane_q < 6, 1.0, 0.0))))
    qaug[:, 0:hd] = q_ref[...].astype(BF16)
    qaug[:, hd:2 * hd] = qa.astype(BF16)

    def scores_into(dst, j, masked=False):
        s = jnp.dot(qaug[...], kaug[j], preferred_element_type=F32)
        if masked:
            row = lax.broadcasted_iota(jnp.int32, s.shape, 0)
            col = lax.broadcasted_iota(jnp.int32, s.shape, 1)
            s = jnp.where(col <= row, s, -jnp.inf)
        dst[...] = s

    def softmax_into(src, dst_p, dst_alpha):
        s = src[...]
        m_old = m_ref[...]
        m_new = jnp.maximum(m_old, jnp.max(s, axis=-1, keepdims=True))
        dst_alpha[...] = jnp.exp2((m_old - m_new) * c_exp)
        m_ref[...] = m_new
        dst_p[...] = jnp.exp2((s - jnp.concatenate([m_new] * (tk // LANES), axis=1)) * c_exp).astype(BF16)

    def accumulate(src_alpha, src_p, j):
        alpha = jnp.concatenate([src_alpha[...]] * (acc_ref.shape[1] // LANES), axis=1)
        acc_ref[...] = alpha * acc_ref[...] + jnp.dot(src_p[...], vaug[key_rows(j), :], preferred_element_type=F32)

    def blk(t):
        return jnp.where(t == 0, i, jnp.maximum(t - 1, 0))

    def tick(t, par):
        scores_into(s_buf[1 - par], t)
        softmax_into(s_buf[par], p_buf[par], al_buf[par])
        accumulate(al_buf[1 - par], p_buf[1 - par], blk(t - 1))

    scores_into(s_buf[0], i, masked=True)
    m_ref[...] = jnp.full(m_ref.shape, -jnp.inf, F32)
    acc_ref[...] = jnp.zeros(acc_ref.shape, F32)
    p_buf[1][...] = jnp.zeros(p_buf[1].shape, BF16)
    al_buf[1][...] = jnp.ones(al_buf[1].shape, F32)

    def body(u, carry):
        tick(2 * u, 0)
        tick(2 * u + 1, 1)
        return carry

    lax.fori_loop(0, i // 2, body, 0)
    odd = i % 2 == 1

    @pl.when(odd)
    def _():
        tick(i - 1, 0)

    def drain(par):
        softmax_into(s_buf[par], p_buf[par], al_buf[par])
        accumulate(al_buf[1 - par], p_buf[1 - par], blk(i - 1))
        accumulate(al_buf[par], p_buf[par], blk(i))

    @pl.when(odd)
    def _():
        drain(1)

    @pl.when(jnp.logical_not(odd))
    def _():
        drain(0)

    o_ref[...] = acc_ref[:, 0:hd] / acc_ref[:, hd:hd + 1]


def _attention(z, cum, heads, hd, col_q, col_k, col_v):
    b, s, _ = z.shape
    tq = _pick(s, 512)
    tk = tq
    kern = functools.partial(_attn_kernel, tq=tq, tk=tk, scale=hd ** -0.5)
    return pl.pallas_call(
        kern,
        grid=(b, heads, s // tq),
        in_specs=[
            pl.BlockSpec((None, tq, hd), lambda bi, h, i: (bi, i, col_q + h)),
            pl.BlockSpec((None, s, hd), lambda bi, h, i: (bi, 0, col_k + h)),
            pl.BlockSpec((None, s, hd), lambda bi, h, i: (bi, 0, col_v + h)),
            pl.BlockSpec((None, s, LANES), lambda bi, h, i: (bi, 0, 0)),
        ],
        out_specs=pl.BlockSpec((None, tq, hd), lambda bi, h, i: (bi, i, h)),
        out_shape=jax.ShapeDtypeStruct((b, s, heads * hd), F32),
        scratch_shapes=[pltpu.VMEM((s // tk, 2 * hd, tk), BF16), pltpu.VMEM((s, 2 * hd), BF16),
                        pltpu.VMEM((tq, 2 * hd), BF16),
                        pltpu.VMEM((tq, tk), F32), pltpu.VMEM((tq, tk), F32),
                        pltpu.VMEM((tq, tk), BF16), pltpu.VMEM((tq, tk), BF16),
                        pltpu.VMEM((tq, LANES), F32), pltpu.VMEM((tq, LANES), F32), pltpu.VMEM((tq, LANES), F32),
                        pltpu.VMEM((tq, 2 * hd), F32)],
        compiler_params=_params("arbitrary", "arbitrary", "arbitrary"),
        name="fox_attention",
    )(z, z, z, cum)


def _outproj_kernel(yc_ref, o_ref, yl_ref, fg_ref, w1_ref, w2_ref, w3_ref, x_ref, g1_ref, out_ref, yf_scr):
    @pl.when(pl.program_id(2) == 0)
    def _():
        o = o_ref[...]
        yf = o * lax.rsqrt(jnp.mean(o * o, axis=-1, keepdims=True) + EPS) * fg_ref[...]
        yf_scr[...] = yf.astype(BF16)

    y = jnp.dot(yc_ref[...], w1_ref[...], preferred_element_type=F32)
    y = y + jnp.dot(yf_scr[...], w2_ref[...], preferred_element_type=F32)
    y = y + jnp.dot(yl_ref[...], w3_ref[...], preferred_element_type=F32)
    out_ref[...] = x_ref[...] + g1_ref[...] * y


def _outproj(y_conv, o_fox, y_lru, fox_g, w1, w2, w3, x, g1):
    b, s, d = x.shape
    cw, fw, lw = y_conv.shape[-1], o_fox.shape[-1], y_lru.shape[-1]
    tm = _pick(s, 512)
    tn = _pick(d, 1024)
    act = lambda w: pl.BlockSpec((None, tm, w), lambda bi, i, j: (bi, i, 0))
    wgt = lambda w: pl.BlockSpec((w, tn), lambda bi, i, j: (0, j))
    return pl.pallas_call(
        _outproj_kernel,
        grid=(b, s // tm, d // tn),
        in_specs=[act(cw), act(fw), act(lw), pl.BlockSpec((1, fw), lambda bi, i, j: (0, 0)),
                  wgt(cw), wgt(fw), wgt(lw),
                  pl.BlockSpec((None, tm, tn), lambda bi, i, j: (bi, i, j)),
                  pl.BlockSpec((None, 1, tn), lambda bi, i, j: (bi, 0, j))],
        out_specs=pl.BlockSpec((None, tm, tn), lambda bi, i, j: (bi, i, j)),
        out_shape=jax.ShapeDtypeStruct((b, s, d), F32),
        scratch_shapes=[pltpu.VMEM((tm, fw), BF16)],
        compiler_params=_params("parallel", "parallel", "arbitrary"),
        name="outproj",
    )(y_conv, o_fox, y_lru, fox_g.reshape(1, fw), w1, w2, w3, x, g1)


def _bf16_bits(v):
    return lax.bitcast_convert_type(v.astype(BF16).astype(F32), jnp.uint32)


def _pack_halves(v):
    half = v.shape[1] // 2
    word = _bf16_bits(v[:, half:]) | (_bf16_bits(v[:, :half]) >> 16)
    return lax.bitcast_convert_type(word, jnp.int32)


def _unpack_halves(w):
    word = lax.bitcast_convert_type(w, jnp.uint32)
    lo = lax.bitcast_convert_type(word << 16, F32)
    hi = lax.bitcast_convert_type(word & jnp.uint32(0xFFFF0000), F32)
    return lo, hi


def _router_kernel(x_ref, g_ref, sc_ref, sh_ref, whi_ref, wlo_ref, rb_ref, h2_ref, ri_ref, rg_ref, cnt_ref, car,
                   *, n_groups, per_group):
    @pl.when(jnp.logical_and(pl.program_id(0) == 0, pl.program_id(1) == 0))
    def _():
        car[...] = jnp.zeros_like(car)

    h2 = _modulated_norm(x_ref[...], g_ref[...], sc_ref[...], sh_ref[...])
    h2_ref[...] = _pack_halves(h2)
    hi = h2.astype(BF16)
    lo = (h2 - hi.astype(F32)).astype(BF16)
    logits = (jnp.dot(hi, whi_ref[...], preferred_element_type=F32)
              + jnp.dot(lo, whi_ref[...], preferred_element_type=F32)
              + jnp.dot(hi, wlo_ref[...], preferred_element_type=F32)) + rb_ref[...]
    lane_i = lax.broadcasted_iota(jnp.int32, logits.shape, 1)
    lane = lane_i.astype(F32)
    big = float(LANES)

    def first_lane(hit):
        return jnp.min(jnp.where(hit, lane, big), axis=-1, keepdims=True)

    gmask = lane_i < n_groups
    lg1 = jnp.where(gmask, logits, -jnp.inf)
    m1 = jnp.max(lg1, axis=-1, keepdims=True)
    e1 = jnp.exp(lg1 - m1)
    p1 = e1 / jnp.sum(e1, axis=-1, keepdims=True)
    p_grp = jnp.max(p1, axis=-1, keepdims=True)
    grp = first_lane((p1 == p_grp) & gmask)
    lo_lane = n_groups + grp * per_group
    emask = (lane >= lo_lane) & (lane < lo_lane + per_group)
    lg2 = jnp.where(emask, logits, -jnp.inf)
    m2 = jnp.max(lg2, axis=-1, keepdims=True)
    e2 = jnp.exp(lg2 - m2)
    p2 = e2 / jnp.sum(e2, axis=-1, keepdims=True)
    v1 = jnp.max(jnp.where(emask, p2, -1.0), axis=-1, keepdims=True)
    i1 = first_lane((p2 == v1) & emask)
    rest = emask & (lane != i1)
    v2 = jnp.max(jnp.where(rest, p2, -1.0), axis=-1, keepdims=True)
    i2 = first_lane((p2 == v2) & rest)
    denom = v1 + v2
    gate1 = p_grp * (v1 / denom)
    gate2 = p_grp * (v2 / denom)
    tm = logits.shape[0]
    picked = jnp.where((lane == i1) | (lane == i2), 1.0, 0.0)
    tri = jnp.where(lax.broadcasted_iota(jnp.int32, (tm, tm), 0) > lax.broadcasted_iota(jnp.int32, (tm, tm), 1),
                    1.0, 0.0).astype(BF16)
    before = jnp.dot(tri, picked.astype(BF16), preferred_element_type=F32) + car[...]
    rank1 = jnp.sum(jnp.where(lane == i1, before, 0.0), axis=-1, keepdims=True)
    rank2 = jnp.sum(jnp.where(lane == i2, before, 0.0), axis=-1, keepdims=True)
    total = car[...] + jnp.sum(picked, axis=0, keepdims=True)
    car[...] = total
    cnt_ref[...] = total
    info = jnp.where(lane_i == 0, i1 - n_groups, jnp.where(lane_i == 1, i2 - n_groups,
                     jnp.where(lane_i == 2, rank1, jnp.where(lane_i == 3, rank2, 0.0))))
    ri_ref[...] = info.astype(jnp.int32)
    rg_ref[...] = jnp.where(lane_i == 0, gate1, jnp.where(lane_i == 1, gate2, 0.0))


def _router(x, g, sc, sh, w_hi, w_lo, r_bias, n_groups, per_group):
    b, s, d = x.shape
    tm = _pick(s, 512)
    kern = functools.partial(_router_kernel, n_groups=n_groups, per_group=per_group)
    row = lambda w: pl.BlockSpec((None, tm, w), lambda bi, i: (bi, i, 0))
    mod = pl.BlockSpec((None, 1, d), lambda bi, i: (bi, 0, 0))
    wsp = pl.BlockSpec((d, LANES), lambda bi, i: (0, 0))
    return pl.pallas_call(
        kern,
        grid=(b, s // tm),
        in_specs=[row(d), pl.BlockSpec((1, d), lambda bi, i: (0, 0)), mod, mod, wsp, wsp,
                  pl.BlockSpec((1, LANES), lambda bi, i: (0, 0))],
        out_specs=[row(d // 2), row(LANES), row(LANES), pl.BlockSpec((1, LANES), lambda bi, i: (0, 0))],
        out_shape=[jax.ShapeDtypeStruct((b, s, d // 2), jnp.int32), jax.ShapeDtypeStruct((b, s, LANES), jnp.int32),
                   jax.ShapeDtypeStruct((b, s, LANES), F32), jax.ShapeDtypeStruct((1, LANES), F32)],
        scratch_shapes=[pltpu.VMEM((1, LANES), F32)],
        compiler_params=_params("arbitrary", "arbitrary"),
        name="router",
    )(x, g, sc, sh, w_hi, w_lo, r_bias)


def _moe_kernel(bexp_ref, rtok_ref, wslot_ref, nexte_ref, h2_hbm, wg_hbm, wu_hbm, wd_hbm, y_ref,
                xbuf_a, xbuf_b, xbuf_c, sem, wg_bf, wu_bf, wd_bf, wg_raw, wu_raw, wd_raw, wsem, *, rows, layer):
    b = pl.program_id(0)
    nb = pl.num_programs(0)
    xbufs = (xbuf_a, xbuf_b, xbuf_c)
    depth = len(xbufs)

    def row_copy(tok, r, slot):
        return pltpu.make_async_copy(h2_hbm.at[pl.ds(tok, 1)], xbufs[slot].at[pl.ds(r, 1)], sem.at[slot])

    def wait_block(slot):
        pltpu.make_async_copy(h2_hbm.at[pl.ds(0, rows)], xbufs[slot], sem.at[slot]).wait()

    @pl.when(b == 0)
    def _():
        for ahead in range(depth - 1):
            def body(r, carry, ahead=ahead):
                row_copy(rtok_ref[ahead * rows + r], r, ahead).start()
                return carry

            lax.fori_loop(0, rows, body, 0)

    def weight_copies(e, s):
        return [pltpu.make_async_copy(src.at[layer, e], dst.at[s], wsem.at[s])
                for src, dst in ((wg_hbm, wg_raw), (wu_hbm, wu_raw), (wd_hbm, wd_raw))]

    @pl.when(b == 0)
    def _():
        for copy in weight_copies(bexp_ref[0], 0):
            copy.start()

    changed = jnp.logical_or(b == 0, bexp_ref[b] != bexp_ref[jnp.maximum(b - 1, 0)])

    @pl.when(changed)
    def _():
        s = wslot_ref[b]
        for copy in weight_copies(bexp_ref[b], s):
            copy.wait()
        for static_s in range(2):
            @pl.when(s == static_s)
            def _():
                wg_bf[...] = wg_raw[static_s].astype(BF16)
                wu_bf[...] = wu_raw[static_s].astype(BF16)
                wd_bf[...] = wd_raw[static_s].astype(BF16)

        nxt = nexte_ref[b]

        @pl.when(nxt >= 0)
        def _():
            for copy in weight_copies(nxt, 1 - s):
                copy.start()

    def block(slot):
        wait_block(slot)
        ahead_slot = (slot + depth - 1) % depth
        base = jnp.minimum(b + depth - 1, nb - 1) * rows
        for r in range(rows):
            row_copy(rtok_ref[base + r], r, ahead_slot).start(priority=r % 2)
        lo, hi = _unpack_halves(xbufs[slot][...])
        xb = jnp.concatenate([lo.astype(BF16), hi.astype(BF16)], axis=1)
        gt = jnp.dot(xb, wg_bf[...], preferred_element_type=F32)
        up = jnp.dot(xb, wu_bf[...], preferred_element_type=F32)
        hb = (gt * _sigmoid(gt) * up).astype(BF16)
        y_ref[...] = _pack_halves(jnp.dot(hb, wd_bf[...], preferred_element_type=F32))

        @pl.when(b == nb - 1)
        def _():
            for other in range(1, depth):
                wait_block((slot + other) % depth)

    for static_slot in range(depth):
        @pl.when(b % depth == static_slot)
        def _(static_slot=static_slot):
            block(static_slot)


def _moe_experts(block_exp, row_tok, h2_rows, w_gate, w_up, w_down, layer):
    _, _, d, f = w_gate.shape
    rows = MOE_ROWS
    n_blocks = block_exp.shape[0]
    changed = jnp.concatenate([jnp.ones((1,), jnp.int32), (block_exp[1:] != block_exp[:-1]).astype(jnp.int32)])
    w_slot = (jnp.cumsum(changed) - 1) % 2
    n_le = jnp.sum((block_exp[None, :] <= block_exp[:, None]).astype(jnp.int32), axis=1)
    next_exp = jnp.where(n_le < n_blocks, block_exp[jnp.minimum(n_le, n_blocks - 1)], -1)
    any_spec = pl.BlockSpec(memory_space=pl.ANY)
    grid_spec = pltpu.PrefetchScalarGridSpec(
        num_scalar_prefetch=4,
        grid=(n_blocks,),
        in_specs=[any_spec, any_spec, any_spec, any_spec],
        out_specs=pl.BlockSpec((rows, d // 2), lambda bi, *_: (bi, 0)),
        scratch_shapes=[pltpu.VMEM((rows, d // 2), jnp.int32), pltpu.VMEM((rows, d // 2), jnp.int32),
                        pltpu.VMEM((rows, d // 2), jnp.int32), pltpu.SemaphoreType.DMA((3,)),
                        pltpu.VMEM((d, f), BF16), pltpu.VMEM((d, f), BF16), pltpu.VMEM((f, d), BF16),
                        pltpu.VMEM((2, d, f), F32), pltpu.VMEM((2, d, f), F32), pltpu.VMEM((2, f, d), F32),
                        pltpu.SemaphoreType.DMA((2,))],
    )
    return pl.pallas_call(
        functools.partial(_moe_kernel, rows=rows, layer=layer),
        grid_spec=grid_spec,
        out_shape=jax.ShapeDtypeStruct((n_blocks * rows, d // 2), jnp.int32),
        compiler_params=_params("arbitrary"),
        name="moe_experts",
    )(block_exp, row_tok, w_slot.astype(jnp.int32), next_exp.astype(jnp.int32), h2_rows, w_gate, w_up, w_down)


def _combine_kernel(dest_ref, y_hbm, x_ref, rg_ref, g2_ref, fg_ref, out_ref, ybuf_a, ybuf_b, ybuf_c, sem,
                    *, tm, final_norm):
    i = pl.program_id(0)
    n = pl.num_programs(0)
    ybufs = (ybuf_a, ybuf_b, ybuf_c)
    depth = len(ybufs)

    def start(row, r, k, slot):
        copy = pltpu.make_async_copy(y_hbm.at[pl.ds(row, 1)], ybufs[slot].at[k, pl.ds(r, 1)], sem.at[slot])
        copy.start(priority=k % 2)

    def wait_tile(slot):
        for k in range(TOP_K):
            pltpu.make_async_copy(y_hbm.at[pl.ds(0, tm)], ybufs[slot].at[k], sem.at[slot]).wait()

    @pl.when(i == 0)
    def _():
        for ahead in range(depth - 1):
            def body(r, carry, ahead=ahead):
                for k in range(TOP_K):
                    start(dest_ref[(ahead * tm + r) * TOP_K + k], r, k, ahead)
                return carry

            lax.fori_loop(0, tm, body, 0)

    def tile(slot):
        wait_tile(slot)
        ahead_slot = (slot + depth - 1) % depth
        base = jnp.minimum(i + depth - 1, n - 1) * (tm * TOP_K)
        for r in range(tm):
            for k in range(TOP_K):
                start(dest_ref[base + r * TOP_K + k], r, k, ahead_slot)
        gates = rg_ref[...]
        gate = [jnp.broadcast_to(gates[:, k:k + 1], (tm, LANES)) for k in range(TOP_K)]
        half = out_ref.shape[1] // 2
        for j in range(half // LANES):
            words = slice(j * LANES, (j + 1) * LANES)
            moe_lo = moe_hi = None
            for k in range(TOP_K):
                lo, hi = _unpack_halves(ybufs[slot][k, :, words])
                moe_lo = lo * gate[k] if moe_lo is None else moe_lo + lo * gate[k]
                moe_hi = hi * gate[k] if moe_hi is None else moe_hi + hi * gate[k]
            for cols, moe in ((words, moe_lo), (slice(half + j * LANES, half + (j + 1) * LANES), moe_hi)):
                out_ref[:, cols] = x_ref[:, cols] + g2_ref[:, cols] * moe
        if final_norm:
            v = out_ref[...]
            out_ref[...] = v * lax.rsqrt(jnp.mean(v * v, axis=-1, keepdims=True) + EPS) * fg_ref[...]

        @pl.when(i == n - 1)
        def _():
            for other in range(1, depth):
                wait_tile((slot + other) % depth)

    for static_slot in range(depth):
        @pl.when(i % depth == static_slot)
        def _(static_slot=static_slot):
            tile(static_slot)


def _combine(dest, y, x_flat, rg_flat, g2, final_g, seq, final_norm):
    t, d = x_flat.shape
    tm = _pick(seq, 128)
    grid_spec = pltpu.PrefetchScalarGridSpec(
        num_scalar_prefetch=1,
        grid=(t // tm,),
        in_specs=[
            pl.BlockSpec(memory_space=pl.ANY),
            pl.BlockSpec((tm, d), lambda i, de: (i, 0)),
            pl.BlockSpec((tm, LANES), lambda i, de: (i, 0)),
            pl.BlockSpec((None, 1, d), lambda i, de: ((i * tm) // seq, 0, 0)),
            pl.BlockSpec((1, d), lambda i, de: (0, 0)),
        ],
        out_specs=pl.BlockSpec((tm, d), lambda i, de: (i, 0)),
        scratch_shapes=[pltpu.VMEM((TOP_K, tm, d // 2), jnp.int32), pltpu.VMEM((TOP_K, tm, d // 2), jnp.int32),
                        pltpu.VMEM((TOP_K, tm, d // 2), jnp.int32), pltpu.SemaphoreType.DMA((3,))],
    )
    return pl.pallas_call(
        functools.partial(_combine_kernel, tm=tm, final_norm=final_norm),
        grid_spec=grid_spec,
        out_shape=jax.ShapeDtypeStruct((t, d), F32),
        compiler_params=_params("arbitrary"),
        name="moe_combine",
    )(dest, y, x_flat, rg_flat, g2, final_g.reshape(1, d))


def _routing_tables(eid, rank, counts, rows):
    n_assign = eid.shape[0]
    n_experts = counts.shape[0]
    padded = (counts + rows - 1) // rows * rows
    pends = jnp.cumsum(padded)
    pstarts = pends - padded
    onehot = eid[:, None] == jnp.arange(n_experts, dtype=jnp.int32)[None, :]
    dest = (jnp.sum(jnp.where(onehot, pstarts[None, :], 0), axis=1) + rank).astype(jnp.int32)
    n_blocks = -(-n_assign // rows) + n_experts
    row_tok = jnp.zeros((n_blocks * rows,), jnp.int32).at[dest].set(jnp.arange(n_assign, dtype=jnp.int32) // TOP_K)
    first_row = jnp.arange(n_blocks, dtype=jnp.int32) * rows
    block_exp = jnp.minimum(jnp.sum((pends[None, :] <= first_row[:, None]).astype(jnp.int32), axis=1), n_experts - 1)
    return block_exp.astype(jnp.int32), row_tok, dest


def _block_diag(w):
    nb, bw, _ = w.shape
    eye = jnp.eye(nb, dtype=w.dtype)
    return (eye[:, None, :, None] * w[:, :, None, :]).reshape(nb * bw, nb * bw)


def kernel(x, c, w_ada, b_ada, ln1_g, w_in, conv_dw_w, conv_dw_b, conv_ln_g, conv_ln_b, fox_f_bias, fox_out_g, lru_conv_w, lru_conv_b, lru_w_r, lru_b_r, lru_w_i, lru_b_i, lru_lambda, lru_out_g, w_out, ln2_g, w_router_group, b_router_group, w_router_expert, b_router_expert, w_gate, w_up, w_down, final_g):
    b, s, d = x.shape
    depth = w_ada.shape[0]
    cw = conv_dw_b.shape[-1]
    fw = fox_out_g.shape[-1]
    lw = lru_lambda.shape[-1]
    heads = fox_f_bias.shape[-1]
    hd = fw // heads
    n_groups, per_group = b_router_expert.shape[1:]
    n_experts = n_groups * per_group
    off_f = 2 * cw + 3 * fw
    assert cw % LANES == 0 and lw == cw and fw % cw == 0 and hd == LANES and heads <= LANES
    assert n_groups + n_experts <= LANES

    c_pad = jnp.zeros((8, d), F32).at[:b].set(c)
    mod_all = _ada_mod(c_pad, w_ada, b_ada)

    for l in range(depth):
        mod = mod_all[l, :b].reshape(b, 1, 6 * d)
        sh1, sc1, g1, sh2, sc2, g2 = [mod[..., k * d:(k + 1) * d] for k in range(6)]

        w_l = w_in[l]
        w_main = jnp.concatenate([w_l[:, :off_f], w_l[:, off_f + heads:]], axis=1).astype(BF16)
        w_f = jnp.zeros((d, LANES), BF16).at[:, :heads].set(w_l[:, off_f:off_f + heads].astype(BF16))
        z, zf = _inproj(x, ln1_g[l].reshape(1, d), sc1, sh1, w_main, w_f)

        y_conv = _conv_branch(z, cw, conv_dw_w[l], conv_dw_b[l], conv_ln_g[l], conv_ln_b[l])

        f_bias = jnp.zeros((1, LANES), F32).at[0, :heads].set(fox_f_bias[l])
        cum = _forget_cumsum(zf, f_bias)
        o_fox = _attention(z, cum, heads, hd, 2 * cw // hd, (2 * cw + fw) // hd, (2 * cw + 2 * fw) // hd)

        y_lru = _lru_branch(z, lw, off_f // lw, lru_conv_w[l], lru_conv_b[l],
                            _block_diag(lru_w_r[l]).astype(BF16), lru_b_r[l],
                            _block_diag(lru_w_i[l]).astype(BF16), lru_b_i[l], lru_lambda[l], lru_out_g[l])

        wo = w_out[l].astype(BF16)
        x = _outproj(y_conv, o_fox, y_lru, fox_out_g[l], wo[:cw], wo[cw:cw + fw], wo[cw + fw:], x, g1)

        w_r = jnp.concatenate(
            [w_router_group[l], w_router_expert[l].transpose(1, 0, 2).reshape(d, n_experts)], axis=1)
        w_r = jnp.zeros((d, LANES), F32).at[:, :n_groups + n_experts].set(w_r)
        w_hi = w_r.astype(BF16)
        w_lo = (w_r - w_hi.astype(F32)).astype(BF16)
        r_bias = jnp.zeros((1, LANES), F32).at[0, :n_groups + n_experts].set(
            jnp.concatenate([b_router_group[l], b_router_expert[l].reshape(-1)]))
        h2, ri, rg, cnt = _router(x, ln2_g[l].reshape(1, d), sc2, sh2, w_hi, w_lo, r_bias, n_groups, per_group)

        eid = ri[:, :, :TOP_K].reshape(-1)
        rank = ri[:, :, TOP_K:2 * TOP_K].reshape(-1)
        counts = cnt[0, n_groups:n_groups + n_experts].astype(jnp.int32)
        block_exp, row_tok, dest = _routing_tables(eid, rank, counts, MOE_ROWS)
        y = _moe_experts(block_exp, row_tok, h2.reshape(b * s, d // 2), w_gate, w_up, w_down, l)
        x = _combine(dest, y, x.reshape(b * s, d), rg.reshape(b * s, LANES), g2, final_g, s,
                     final_norm=(l == depth - 1)).reshape(b, s, d)

    return x
```

```python
import functools

import jax
import jax.numpy as jnp
from jax import lax
from jax.experimental import pallas as pl
from jax.experimental.pallas import tpu as pltpu

EPS = 1e-6
LRU_C = 8.0
TOP_K = 2
LANES = 128
MOE_ROWS = 128
GATHER_DEPTH = 5
VMEM_LIMIT = 56 * 1024 * 1024

F32 = jnp.float32
BF16 = jnp.bfloat16


def _pick(n, pref):
    if n <= pref:
        return n
    for t in range(pref, 7, -1):
        if n % t == 0 and t % 8 == 0:
            return t
    raise ValueError((n, pref))


def _params(*sem):
    return pltpu.CompilerParams(dimension_semantics=sem, vmem_limit_bytes=VMEM_LIMIT)


def _sigmoid(x):
    return jax.nn.sigmoid(x)


def _ada_kernel(c_ref, w_ref, b_ref, o_ref):
    c = c_ref[...]
    cond = (c * _sigmoid(c)).astype(BF16)
    o_ref[...] = jnp.dot(cond, w_ref[...].astype(BF16), preferred_element_type=F32) + b_ref[...]


def _ada_mod(c_pad, w_ada, b_ada):
    depth, d, n = w_ada.shape
    rows = c_pad.shape[0]
    tn = _pick(n, 1024)
    return pl.pallas_call(
        _ada_kernel,
        grid=(depth, n // tn),
        in_specs=[
            pl.BlockSpec((rows, d), lambda l, j: (0, 0)),
            pl.BlockSpec((None, d, tn), lambda l, j: (l, 0, j)),
            pl.BlockSpec((None, 1, tn), lambda l, j: (l, 0, j)),
        ],
        out_specs=pl.BlockSpec((None, rows, tn), lambda l, j: (l, 0, j)),
        out_shape=jax.ShapeDtypeStruct((depth, rows, n), F32),
        compiler_params=_params("parallel", "parallel"),
        name="ada_mod",
    )(c_pad, w_ada, b_ada.reshape(depth, 1, n))


def _modulated_norm(x, g, sc, sh):
    y = x * lax.rsqrt(jnp.mean(x * x, axis=-1, keepdims=True) + EPS) * g
    return y * (1.0 + sc) + sh


def _inproj_kernel(x_ref, g_ref, sc_ref, sh_ref, w_ref, wf_ref, z_ref, zf_ref, h_scr):
    @pl.when(pl.program_id(2) == 0)
    def _():
        hb = _modulated_norm(x_ref[...], g_ref[...], sc_ref[...], sh_ref[...]).astype(BF16)
        h_scr[...] = hb
        zf_ref[...] = jnp.dot(hb, wf_ref[...], preferred_element_type=F32)

    z_ref[...] = jnp.dot(h_scr[...], w_ref[...], preferred_element_type=F32)


def _inproj(x, g, sc, sh, w_main, w_f):
    b, s, d = x.shape
    n = w_main.shape[1]
    tm = _pick(s, 1024)
    tn = _pick(n, 1024)
    return pl.pallas_call(
        _inproj_kernel,
        grid=(b, s // tm, n // tn),
        in_specs=[
            pl.BlockSpec((None, tm, d), lambda bi, i, j: (bi, i, 0)),
            pl.BlockSpec((1, d), lambda bi, i, j: (0, 0)),
            pl.BlockSpec((None, 1, d), lambda bi, i, j: (bi, 0, 0)),
            pl.BlockSpec((None, 1, d), lambda bi, i, j: (bi, 0, 0)),
            pl.BlockSpec((d, tn), lambda bi, i, j: (0, j)),
            pl.BlockSpec((d, LANES), lambda bi, i, j: (0, 0)),
        ],
        out_specs=[
            pl.BlockSpec((None, tm, tn), lambda bi, i, j: (bi, i, j)),
            pl.BlockSpec((None, tm, LANES), lambda bi, i, j: (bi, i, 0)),
        ],
        out_shape=[jax.ShapeDtypeStruct((b, s, n), F32), jax.ShapeDtypeStruct((b, s, LANES), F32)],
        scratch_shapes=[pltpu.VMEM((tm, d), BF16)],
        compiler_params=_params("parallel", "parallel", "arbitrary"),
        name="inproj",
    )(x, g, sc, sh, w_main, w_f)


def _conv_kernel(a_ref, g_ref, ah_ref, gh_ref, w_ref, b_ref, lg_ref, lb_ref, o_ref, buf, ybuf, shifted,
                 *, taps, ts, halo, rc):
    cw = a_ref.shape[-1]
    buf[halo:halo + ts, :] = a_ref[...] * _sigmoid(g_ref[...])
    prev = ah_ref[...] * _sigmoid(gh_ref[...])
    buf[0:halo, :] = jnp.where(pl.program_id(1) > 0, prev, 0.0)
    first = halo - (taps - 1)
    for c in range(cw // LANES):
        cs = slice(c * LANES, (c + 1) * LANES)
        for r in range(ts // rc):
            acc = b_ref[:, cs]
            for rem in range(8):
                group = [(k, first + k - rem) for k in range(taps) if (first + k) % 8 == rem]
                if not group:
                    continue
                lo = min(a for _, a in group)
                hi = max(a for _, a in group)
                shifted[0:hi - lo + rc, :] = buf[r * rc + rem + lo:r * rc + rem + hi + rc, cs]
                for k, a in group:
                    acc = acc + w_ref[k:k + 1, cs] * shifted[a - lo:a - lo + rc, :]
            ybuf[r * rc:(r + 1) * rc, cs] = acc
    y = ybuf[...]
    mu = jnp.mean(y, axis=-1, keepdims=True)
    yc = y - mu
    var = jnp.mean(yc * yc, axis=-1, keepdims=True)
    yn = yc * lax.rsqrt(var + EPS) * lg_ref[...] + lb_ref[...]
    o_ref[...] = (yn * _sigmoid(yn)).astype(o_ref.dtype)


def _conv_branch(z, cw, dw_w, dw_b, ln_g, ln_b):
    b, s, _ = z.shape
    taps = dw_w.shape[0]
    halo = 32
    assert taps - 1 <= halo
    ts = _pick(s, 256)
    rc = _pick(ts, 128)
    hb = ts // halo
    kern = functools.partial(_conv_kernel, taps=taps, ts=ts, halo=halo, rc=rc)
    cur = lambda col: pl.BlockSpec((None, ts, cw), lambda bi, i: (bi, i, col))
    prev = lambda col: pl.BlockSpec((None, halo, cw), lambda bi, i: (bi, jnp.maximum(i * hb - 1, 0), col))
    vec = pl.BlockSpec((1, cw), lambda bi, i: (0, 0))
    return pl.pallas_call(
        kern,
        grid=(b, s // ts),
        in_specs=[cur(0), cur(1), prev(0), prev(1),
                  pl.BlockSpec((taps, cw), lambda bi, i: (0, 0)), vec, vec, vec],
        out_specs=pl.BlockSpec((None, ts, cw), lambda bi, i: (bi, i, 0)),
        out_shape=jax.ShapeDtypeStruct((b, s, cw), BF16),
        scratch_shapes=[pltpu.VMEM((halo + ts, cw), F32), pltpu.VMEM((ts, cw), F32),
                        pltpu.VMEM((rc + halo, LANES), F32)],
        compiler_params=_params("parallel", "parallel"),
        name="conv_branch",
    )(z, z, z, z, dw_w, dw_b.reshape(1, cw), ln_g.reshape(1, cw), ln_b.reshape(1, cw))


def _shift_rows(v, d, fill):
    if d % 8 == 0:
        return jnp.concatenate([jnp.full((d, v.shape[1]), fill, v.dtype), v[:v.shape[0] - d]], axis=0)
    rows = lax.broadcasted_iota(jnp.int32, v.shape, 0)
    return jnp.where(rows >= d, pltpu.roll(v, d, axis=0), fill)


def _scan_linear(a, bv):
    d = 1
    while d < a.shape[0]:
        bv = a * _shift_rows(bv, d, 0.0) + bv
        a = a * _shift_rows(a, d, 1.0)
        d *= 2
    return a, bv


def _scan_sum(v):
    d = 1
    while d < v.shape[0]:
        v = v + _shift_rows(v, d, 0.0)
        d *= 2
    return v


def _cum_kernel(zf_ref, fb_ref, o_ref, car, *, ts, rs):
    @pl.when(pl.program_id(1) == 0)
    def _():
        car[...] = jnp.zeros_like(car)

    h = car[...]
    for r in range(ts // rs):
        x = zf_ref[r * rs:(r + 1) * rs, :] + fb_ref[...]
        log_f = jnp.minimum(x, 0.0) - jnp.log1p(jnp.exp(-jnp.abs(x)))
        v = _scan_sum(log_f) + h
        o_ref[r * rs:(r + 1) * rs, :] = v
        h = v[rs - 1:rs, :]
    car[...] = h


def _forget_cumsum(zf, f_bias_pad):
    b, s, _ = zf.shape
    ts = _pick(s, 512)
    rs = _pick(ts, 128)
    return pl.pallas_call(
        functools.partial(_cum_kernel, ts=ts, rs=rs),
        grid=(b, s // ts),
        in_specs=[pl.BlockSpec((None, ts, LANES), lambda bi, i: (bi, i, 0)),
                  pl.BlockSpec((1, LANES), lambda bi, i: (0, 0))],
        out_specs=pl.BlockSpec((None, ts, LANES), lambda bi, i: (bi, i, 0)),
        out_shape=jax.ShapeDtypeStruct((b, s, LANES), F32),
        scratch_shapes=[pltpu.VMEM((1, LANES), F32)],
        compiler_params=_params("parallel", "arbitrary"),
        name="forget_cumsum",
    )(zf, f_bias_pad)


def _gelu_tanh(x):
    return 0.5 * x * (1.0 + jnp.tanh(0.7978845608028654 * (x + 0.044715 * (x * x * x))))


def _lru_kernel(u_ref, uh_ref, gate_ref, cw_ref, cb_ref, wr_ref, br_ref, wi_ref, bi_ref, lam_ref, og_ref,
                o_ref, ubuf, abuf, bbuf, car, *, taps, ts, halo, rs):
    lw = u_ref.shape[-1]

    @pl.when(pl.program_id(1) == 0)
    def _():
        car[...] = jnp.zeros_like(car)

    ubuf[halo:halo + ts, :] = u_ref[...]
    ubuf[0:halo, :] = jnp.where(pl.program_id(1) > 0, uh_ref[...], 0.0)
    xc = cb_ref[...]
    for k in range(taps):
        off = halo - (taps - 1) + k
        xc = xc + cw_ref[k:k + 1, :] * ubuf[off:off + ts, :]
    xb = xc.astype(BF16)
    r = _sigmoid(jnp.dot(xb, wr_ref[...], preferred_element_type=F32) + br_ref[...])
    ig = _sigmoid(jnp.dot(xb, wi_ref[...], preferred_element_type=F32) + bi_ref[...])
    nl = -lam_ref[...]
    softplus = jnp.maximum(nl, 0.0) + jnp.log1p(jnp.exp(-jnp.abs(nl)))
    log_a = (-LRU_C) * r * softplus
    th = jnp.tanh(log_a)
    abuf[...] = jnp.exp(log_a)
    bbuf[...] = jnp.sqrt(-2.0 * th / (1.0 - th)) * (ig * xc)
    for c in range(lw // LANES):
        cs = slice(c * LANES, (c + 1) * LANES)
        h = car[:, cs]
        for q in range(ts // rs):
            rows = slice(q * rs, (q + 1) * rs)
            a_cum, b_cum = _scan_linear(abuf[rows, cs], bbuf[rows, cs])
            hv = b_cum + a_cum * h
            bbuf[rows, cs] = hv
            h = hv[rs - 1:rs, :]
        car[:, cs] = h
    y = bbuf[...] * _gelu_tanh(gate_ref[...])
    y = y * lax.rsqrt(jnp.mean(y * y, axis=-1, keepdims=True) + EPS) * og_ref[...]
    o_ref[...] = y.astype(o_ref.dtype)


def _lru_branch(z, lw, col_x, conv_w, conv_b, wr_bd, b_r, wi_bd, b_i, lam, out_g):
    b, s, _ = z.shape
    taps = conv_w.shape[0]
    halo = 8
    assert taps - 1 <= halo
    ts = _pick(s, 256)
    rs = _pick(ts, 128)
    hb = ts // halo
    kern = functools.partial(_lru_kernel, taps=taps, ts=ts, halo=halo, rs=rs)
    vec = pl.BlockSpec((1, lw), lambda bi, i: (0, 0))
    mat = pl.BlockSpec((lw, lw), lambda bi, i: (0, 0))
    return pl.pallas_call(
        kern,
        grid=(b, s // ts),
        in_specs=[
            pl.BlockSpec((None, ts, lw), lambda bi, i: (bi, i, col_x)),
            pl.BlockSpec((None, halo, lw), lambda bi, i: (bi, jnp.maximum(i * hb - 1, 0), col_x)),
            pl.BlockSpec((None, ts, lw), lambda bi, i: (bi, i, col_x + 1)),
            pl.BlockSpec((taps, lw), lambda bi, i: (0, 0)), vec, mat, vec, mat, vec, vec, vec,
        ],
        out_specs=pl.BlockSpec((None, ts, lw), lambda bi, i: (bi, i, 0)),
        out_shape=jax.ShapeDtypeStruct((b, s, lw), BF16),
        scratch_shapes=[pltpu.VMEM((halo + ts, lw), F32), pltpu.VMEM((ts, lw), F32),
                        pltpu.VMEM((ts, lw), F32), pltpu.VMEM((1, lw), F32)],
        compiler_params=_params("parallel", "arbitrary"),
        name="lru_branch",
    )(z, z, z, conv_w, conv_b.reshape(1, lw), wr_bd, b_r.reshape(1, lw), wi_bd, b_i.reshape(1, lw),
      lam.reshape(1, lw), out_g.reshape(1, lw))


def _split3(t):
    hi = t.astype(BF16).astype(F32)
    r = t - hi
    mid = r.astype(BF16).astype(F32)
    return hi, mid, r - mid


def _attn_kernel(q_ref, k_ref, v_ref, cum_ref, o_ref, kaug, vaug, qaug, s_a, s_b, p_a, p_b, al_a, al_b, m_ref, acc_ref,
                 *, tq, tk, scale):
    h = pl.program_id(1)
    i = pl.program_id(2)
    s_len, hd = k_ref.shape
    inv_scale = 1.0 / scale
    c_exp = scale * 1.4426950408889634
    s_buf, p_buf, al_buf = (s_a, s_b), (p_a, p_b), (al_a, al_b)

    def head_cum(start, size):
        lane = lax.broadcasted_iota(jnp.int32, (size, LANES), 1)
        rows = pl.ds(pl.multiple_of(start, size), size)
        return jnp.sum(jnp.where(lane == h, cum_ref[rows, :], 0.0), axis=-1, keepdims=True) * inv_scale

    def key_rows(j):
        return pl.ds(pl.multiple_of(j * tk, tk), tk)

    @pl.when(i == 0)
    def _():
        lane = lax.broadcasted_iota(jnp.int32, (tk, LANES), 1)

        def body(t, carry):
            rows = key_rows(t)
            hi, mid, lo = _split3(-head_cum(t * tk, tk))
            aug = jnp.where(lane < 3, 1.0, jnp.where(lane == 3, hi, jnp.where(lane == 4, mid,
                            jnp.where(lane == 5, lo, 0.0))))
            kaug[t, 0:hd, :] = k_ref[rows, :].T.astype(BF16)
            kaug[t, hd:2 * hd, :] = aug.T.astype(BF16)
            vaug[rows, 0:hd] = v_ref[rows, :].astype(BF16)
            vaug[rows, hd:2 * hd] = jnp.where(lane == 0, 1.0, 0.0).astype(BF16)
            return carry

        lax.fori_loop(0, s_len // tk, body, 0)

    lane_q = lax.broadcasted_iota(jnp.int32, (tq, LANES), 1)
    hi, mid, lo = _split3(head_cum(i * tq, tq))
    qa = jnp.where(lane_q == 0, hi, jnp.where(lane_q == 1, mid, jnp.where(lane_q == 2, lo,
                   jnp.where(lane_q < 6, 1.0, 0.0))))
    qaug[:, 0:hd] = q_ref[...].astype(BF16)
    qaug[:, hd:2 * hd] = qa.astype(BF16)

    def scores_into(dst, j, masked=False):
        s = jnp.dot(qaug[...], kaug[j], preferred_element_type=F32)
        if masked:
            row = lax.broadcasted_iota(jnp.int32, s.shape, 0)
            col = lax.broadcasted_iota(jnp.int32, s.shape, 1)
            s = jnp.where(col <= row, s, -jnp.inf)
        dst[...] = s

    def softmax_into(src, dst_p, dst_alpha):
        s = src[...]
        m_old = m_ref[...]
        m_new = jnp.maximum(m_old, jnp.max(s, axis=-1, keepdims=True))
        dst_alpha[...] = jnp.exp2((m_old - m_new) * c_exp)
        m_ref[...] = m_new
        dst_p[...] = jnp.exp2((s - jnp.concatenate([m_new] * (tk // LANES), axis=1)) * c_exp).astype(BF16)

    def accumulate(src_alpha, src_p, j):
        alpha = jnp.concatenate([src_alpha[...]] * (acc_ref.shape[1] // LANES), axis=1)
        acc_ref[...] = alpha * acc_ref[...] + jnp.dot(src_p[...], vaug[key_rows(j), :], preferred_element_type=F32)

    def blk(t):
        return jnp.where(t == 0, i, jnp.maximum(t - 1, 0))

    def tick(t, par):
        scores_into(s_buf[1 - par], t)
        softmax_into(s_buf[par], p_buf[par], al_buf[par])
        accumulate(al_buf[1 - par], p_buf[1 - par], blk(t - 1))

    scores_into(s_buf[0], i, masked=True)
    m_ref[...] = jnp.full(m_ref.shape, -jnp.inf, F32)
    acc_ref[...] = jnp.zeros(acc_ref.shape, F32)
    p_buf[1][...] = jnp.zeros(p_buf[1].shape, BF16)
    al_buf[1][...] = jnp.ones(al_buf[1].shape, F32)

    def body(u, carry):
        tick(2 * u, 0)
        tick(2 * u + 1, 1)
        return carry

    lax.fori_loop(0, i // 2, body, 0)
    odd = i % 2 == 1

    @pl.when(odd)
    def _():
        tick(i - 1, 0)

    def drain(par):
        softmax_into(s_buf[par], p_buf[par], al_buf[par])
        accumulate(al_buf[1 - par], p_buf[1 - par], blk(i - 1))
        accumulate(al_buf[par], p_buf[par], blk(i))

    @pl.when(odd)
    def _():
        drain(1)

    @pl.when(jnp.logical_not(odd))
    def _():
        drain(0)

    o_ref[...] = acc_ref[:, 0:hd] / acc_ref[:, hd:hd + 1]


def _attention(z, cum, heads, hd, col_q, col_k, col_v):
    b, s, _ = z.shape
    tq = _pick(s, 512)
    tk = tq
    kern = functools.partial(_attn_kernel, tq=tq, tk=tk, scale=hd ** -0.5)
    return pl.pallas_call(
        kern,
        grid=(b, heads, s // tq),
        in_specs=[
            pl.BlockSpec((None, tq, hd), lambda bi, h, i: (bi, i, col_q + h)),
            pl.BlockSpec((None, s, hd), lambda bi, h, i: (bi, 0, col_k + h)),
            pl.BlockSpec((None, s, hd), lambda bi, h, i: (bi, 0, col_v + h)),
            pl.BlockSpec((None, s, LANES), lambda bi, h, i: (bi, 0, 0)),
        ],
        out_specs=pl.BlockSpec((None, tq, hd), lambda bi, h, i: (bi, i, h)),
        out_shape=jax.ShapeDtypeStruct((b, s, heads * hd), F32),
        scratch_shapes=[pltpu.VMEM((s // tk, 2 * hd, tk), BF16), pltpu.VMEM((s, 2 * hd), BF16),
                        pltpu.VMEM((tq, 2 * hd), BF16),
                        pltpu.VMEM((tq, tk), F32), pltpu.VMEM((tq, tk), F32),
                        pltpu.VMEM((tq, tk), BF16), pltpu.VMEM((tq, tk), BF16),
                        pltpu.VMEM((tq, LANES), F32), pltpu.VMEM((tq, LANES), F32), pltpu.VMEM((tq, LANES), F32),
                        pltpu.VMEM((tq, 2 * hd), F32)],
        compiler_params=_params("arbitrary", "arbitrary", "arbitrary"),
        name="fox_attention",
    )(z, z, z, cum)


def _outproj_kernel(yc_ref, o_ref, yl_ref, fg_ref, w1_ref, w2_ref, w3_ref, x_ref, g1_ref, out_ref, yf_scr):
    @pl.when(pl.program_id(2) == 0)
    def _():
        o = o_ref[...]
        yf = o * lax.rsqrt(jnp.mean(o * o, axis=-1, keepdims=True) + EPS) * fg_ref[...]
        yf_scr[...] = yf.astype(BF16)

    y = jnp.dot(yc_ref[...], w1_ref[...], preferred_element_type=F32)
    y = y + jnp.dot(yf_scr[...], w2_ref[...], preferred_element_type=F32)
    y = y + jnp.dot(yl_ref[...], w3_ref[...], preferred_element_type=F32)
    out_ref[...] = x_ref[...] + g1_ref[...] * y


def _outproj(y_conv, o_fox, y_lru, fox_g, w1, w2, w3, x, g1):
    b, s, d = x.shape
    cw, fw, lw = y_conv.shape[-1], o_fox.shape[-1], y_lru.shape[-1]
    tm = _pick(s, 512)
    tn = _pick(d, 1024)
    act = lambda w: pl.BlockSpec((None, tm, w), lambda bi, i, j: (bi, i, 0))
    wgt = lambda w: pl.BlockSpec((w, tn), lambda bi, i, j: (0, j))
    return pl.pallas_call(
        _outproj_kernel,
        grid=(b, s // tm, d // tn),
        in_specs=[act(cw), act(fw), act(lw), pl.BlockSpec((1, fw), lambda bi, i, j: (0, 0)),
                  wgt(cw), wgt(fw), wgt(lw),
                  pl.BlockSpec((None, tm, tn), lambda bi, i, j: (bi, i, j)),
                  pl.BlockSpec((None, 1, tn), lambda bi, i, j: (bi, 0, j))],
        out_specs=pl.BlockSpec((None, tm, tn), lambda bi, i, j: (bi, i, j)),
        out_shape=jax.ShapeDtypeStruct((b, s, d), F32),
        scratch_shapes=[pltpu.VMEM((tm, fw), BF16)],
        compiler_params=_params("parallel", "parallel", "arbitrary"),
        name="outproj",
    )(y_conv, o_fox, y_lru, fox_g.reshape(1, fw), w1, w2, w3, x, g1)


def _bf16_bits(v):
    return lax.bitcast_convert_type(v.astype(BF16).astype(F32), jnp.uint32)


def _pack_halves(v):
    half = v.shape[1] // 2
    word = _bf16_bits(v[:, half:]) | (_bf16_bits(v[:, :half]) >> 16)
    return lax.bitcast_convert_type(word, jnp.int32)


def _unpack_halves(w):
    word = lax.bitcast_convert_type(w, jnp.uint32)
    lo = lax.bitcast_convert_type(word << 16, F32)
    hi = lax.bitcast_convert_type(word & jnp.uint32(0xFFFF0000), F32)
    return lo, hi


def _router_kernel(x_ref, g_ref, sc_ref, sh_ref, whi_ref, wlo_ref, rb_ref, h2_ref, ri_ref, rg_ref, cnt_ref, car,
                   *, n_groups, per_group):
    @pl.when(jnp.logical_and(pl.program_id(0) == 0, pl.program_id(1) == 0))
    def _():
        car[...] = jnp.zeros_like(car)

    h2 = _modulated_norm(x_ref[...], g_ref[...], sc_ref[...], sh_ref[...])
    h2_ref[...] = _pack_halves(h2)
    hi = h2.astype(BF16)
    lo = (h2 - hi.astype(F32)).astype(BF16)
    logits = (jnp.dot(hi, whi_ref[...], preferred_element_type=F32)
              + jnp.dot(lo, whi_ref[...], preferred_element_type=F32)
              + jnp.dot(hi, wlo_ref[...], preferred_element_type=F32)) + rb_ref[...]
    lane_i = lax.broadcasted_iota(jnp.int32, logits.shape, 1)
    lane = lane_i.astype(F32)
    big = float(LANES)

    def first_lane(hit):
        return jnp.min(jnp.where(hit, lane, big), axis=-1, keepdims=True)

    gmask = lane_i < n_groups
    lg1 = jnp.where(gmask, logits, -jnp.inf)
    m1 = jnp.max(lg1, axis=-1, keepdims=True)
    e1 = jnp.exp(lg1 - m1)
    p1 = e1 / jnp.sum(e1, axis=-1, keepdims=True)
    p_grp = jnp.max(p1, axis=-1, keepdims=True)
    grp = first_lane((p1 == p_grp) & gmask)
    lo_lane = n_groups + grp * per_group
    emask = (lane >= lo_lane) & (lane < lo_lane + per_group)
    lg2 = jnp.where(emask, logits, -jnp.inf)
    m2 = jnp.max(lg2, axis=-1, keepdims=True)
    e2 = jnp.exp(lg2 - m2)
    p2 = e2 / jnp.sum(e2, axis=-1, keepdims=True)
    v1 = jnp.max(jnp.where(emask, p2, -1.0), axis=-1, keepdims=True)
    i1 = first_lane((p2 == v1) & emask)
    rest = emask & (lane != i1)
    v2 = jnp.max(jnp.where(rest, p2, -1.0), axis=-1, keepdims=True)
    i2 = first_lane((p2 == v2) & rest)
    denom = v1 + v2
    gate1 = p_grp * (v1 / denom)
    gate2 = p_grp * (v2 / denom)
    tm = logits.shape[0]
    picked = jnp.where((lane == i1) | (lane == i2), 1.0, 0.0)
    tri = jnp.where(lax.broadcasted_iota(jnp.int32, (tm, tm), 0) > lax.broadcasted_iota(jnp.int32, (tm, tm), 1),
                    1.0, 0.0).astype(BF16)
    before = jnp.dot(tri, picked.astype(BF16), preferred_element_type=F32) + car[...]
    rank1 = jnp.sum(jnp.where(lane == i1, before, 0.0), axis=-1, keepdims=True)
    rank2 = jnp.sum(jnp.where(lane == i2, before, 0.0), axis=-1, keepdims=True)
    total = car[...] + jnp.sum(picked, axis=0, keepdims=True)
    car[...] = total
    cnt_ref[...] = total
    info = jnp.where(lane_i == 0, i1 - n_groups, jnp.where(lane_i == 1, i2 - n_groups,
                     jnp.where(lane_i == 2, rank1, jnp.where(lane_i == 3, rank2, 0.0))))
    ri_ref[...] = info.astype(jnp.int32)
    rg_ref[...] = jnp.where(lane_i == 0, gate1, jnp.where(lane_i == 1, gate2, 0.0))


def _router(x, g, sc, sh, w_hi, w_lo, r_bias, n_groups, per_group):
    b, s, d = x.shape
    tm = _pick(s, 512)
    kern = functools.partial(_router_kernel, n_groups=n_groups, per_group=per_group)
    row = lambda w: pl.BlockSpec((None, tm, w), lambda bi, i: (bi, i, 0))
    mod = pl.BlockSpec((None, 1, d), lambda bi, i: (bi, 0, 0))
    wsp = pl.BlockSpec((d, LANES), lambda bi, i: (0, 0))
    return pl.pallas_call(
        kern,
        grid=(b, s // tm),
        in_specs=[row(d), pl.BlockSpec((1, d), lambda bi, i: (0, 0)), mod, mod, wsp, wsp,
                  pl.BlockSpec((1, LANES), lambda bi, i: (0, 0))],
        out_specs=[row(d // 2), row(LANES), row(LANES), pl.BlockSpec((1, LANES), lambda bi, i: (0, 0))],
        out_shape=[jax.ShapeDtypeStruct((b, s, d // 2), jnp.int32), jax.ShapeDtypeStruct((b, s, LANES), jnp.int32),
                   jax.ShapeDtypeStruct((b, s, LANES), F32), jax.ShapeDtypeStruct((1, LANES), F32)],
        scratch_shapes=[pltpu.VMEM((1, LANES), F32)],
        compiler_params=_params("arbitrary", "arbitrary"),
        name="router",
    )(x, g, sc, sh, w_hi, w_lo, r_bias)


def _moe_kernel(bexp_ref, rtok_ref, wslot_ref, nexte_ref, h2_hbm, wg_hbm, wu_hbm, wd_hbm, y_ref,
                *scratch, rows, layer):
    xbufs = scratch[:GATHER_DEPTH]
    sem, wg_bf, wu_bf, wd_bf, wg_raw, wu_raw, wd_raw, wsem = scratch[GATHER_DEPTH:]
    b = pl.program_id(0)
    nb = pl.num_programs(0)
    depth = GATHER_DEPTH

    def row_copy(tok, r, slot):
        return pltpu.make_async_copy(h2_hbm.at[pl.ds(tok, 1)], xbufs[slot].at[pl.ds(r, 1)], sem.at[slot])

    def wait_block(slot):
        pltpu.make_async_copy(h2_hbm.at[pl.ds(0, rows)], xbufs[slot], sem.at[slot]).wait()

    @pl.when(b == 0)
    def _():
        for ahead in range(depth - 1):
            def body(r, carry, ahead=ahead):
                row_copy(rtok_ref[ahead * rows + r], r, ahead).start()
                return carry

            lax.fori_loop(0, rows, body, 0)

    def weight_copies(e, s):
        return [pltpu.make_async_copy(src.at[layer, e], dst.at[s], wsem.at[s])
                for src, dst in ((wg_hbm, wg_raw), (wu_hbm, wu_raw), (wd_hbm, wd_raw))]

    @pl.when(b == 0)
    def _():
        for copy in weight_copies(bexp_ref[0], 0):
            copy.start()

    changed = jnp.logical_or(b == 0, bexp_ref[b] != bexp_ref[jnp.maximum(b - 1, 0)])

    @pl.when(changed)
    def _():
        s = wslot_ref[b]
        for copy in weight_copies(bexp_ref[b], s):
            copy.wait()
        for static_s in range(2):
            @pl.when(s == static_s)
            def _():
                wg_bf[...] = wg_raw[static_s].astype(BF16)
                wu_bf[...] = wu_raw[static_s].astype(BF16)
                wd_bf[...] = wd_raw[static_s].astype(BF16)

        nxt = nexte_ref[b]

        @pl.when(nxt >= 0)
        def _():
            for copy in weight_copies(nxt, 1 - s):
                copy.start()

    def block(slot):
        wait_block(slot)
        ahead_slot = (slot + depth - 1) % depth
        base = jnp.minimum(b + depth - 1, nb - 1) * rows
        for r in range(rows):
            row_copy(rtok_ref[base + r], r, ahead_slot).start(priority=r % 2)
        lo, hi = _unpack_halves(xbufs[slot][...])
        xb = jnp.concatenate([lo.astype(BF16), hi.astype(BF16)], axis=1)
        gt = jnp.dot(xb, wg_bf[...], preferred_element_type=F32)
        up = jnp.dot(xb, wu_bf[...], preferred_element_type=F32)
        hb = (gt * _sigmoid(gt) * up).astype(BF16)
        y_ref[...] = _pack_halves(jnp.dot(hb, wd_bf[...], preferred_element_type=F32))

        @pl.when(b == nb - 1)
        def _():
            for other in range(1, depth):
                wait_block((slot + other) % depth)

    for static_slot in range(depth):
        @pl.when(b % depth == static_slot)
        def _(static_slot=static_slot):
            block(static_slot)


def _moe_experts(block_exp, row_tok, h2_rows, w_gate, w_up, w_down, layer):
    _, _, d, f = w_gate.shape
    rows = MOE_ROWS
    n_blocks = block_exp.shape[0]
    changed = jnp.concatenate([jnp.ones((1,), jnp.int32), (block_exp[1:] != block_exp[:-1]).astype(jnp.int32)])
    w_slot = (jnp.cumsum(changed) - 1) % 2
    n_le = jnp.sum((block_exp[None, :] <= block_exp[:, None]).astype(jnp.int32), axis=1)
    next_exp = jnp.where(n_le < n_blocks, block_exp[jnp.minimum(n_le, n_blocks - 1)], -1)
    any_spec = pl.BlockSpec(memory_space=pl.ANY)
    grid_spec = pltpu.PrefetchScalarGridSpec(
        num_scalar_prefetch=4,
        grid=(n_blocks,),
        in_specs=[any_spec, any_spec, any_spec, any_spec],
        out_specs=pl.BlockSpec((rows, d // 2), lambda bi, *_: (bi, 0)),
        scratch_shapes=[pltpu.VMEM((rows, d // 2), jnp.int32)] * GATHER_DEPTH + [
                        pltpu.SemaphoreType.DMA((GATHER_DEPTH,)),
                        pltpu.VMEM((d, f), BF16), pltpu.VMEM((d, f), BF16), pltpu.VMEM((f, d), BF16),
                        pltpu.VMEM((2, d, f), F32), pltpu.VMEM((2, d, f), F32), pltpu.VMEM((2, f, d), F32),
                        pltpu.SemaphoreType.DMA((2,))],
    )
    return pl.pallas_call(
        functools.partial(_moe_kernel, rows=rows, layer=layer),
        grid_spec=grid_spec,
        out_shape=jax.ShapeDtypeStruct((n_blocks * rows, d // 2), jnp.int32),
        compiler_params=_params("arbitrary"),
        name="moe_experts",
    )(block_exp, row_tok, w_slot.astype(jnp.int32), next_exp.astype(jnp.int32), h2_rows, w_gate, w_up, w_down)


def _combine_kernel(dest_ref, y_hbm, x_ref, rg_ref, g2_ref, fg_ref, out_ref, *scratch, tm, final_norm):
    ybufs, sem = scratch[:GATHER_DEPTH], scratch[GATHER_DEPTH]
    i = pl.program_id(0)
    n = pl.num_programs(0)
    depth = GATHER_DEPTH

    def start(row, r, k, slot):
        copy = pltpu.make_async_copy(y_hbm.at[pl.ds(row, 1)], ybufs[slot].at[k, pl.ds(r, 1)], sem.at[slot])
        copy.start(priority=k % 2)

    def wait_tile(slot):
        for k in range(TOP_K):
            pltpu.make_async_copy(y_hbm.at[pl.ds(0, tm)], ybufs[slot].at[k], sem.at[slot]).wait()

    @pl.when(i == 0)
    def _():
        for ahead in range(depth - 1):
            def body(r, carry, ahead=ahead):
                for k in range(TOP_K):
                    start(dest_ref[(ahead * tm + r) * TOP_K + k], r, k, ahead)
                return carry

            lax.fori_loop(0, tm, body, 0)

    def tile(slot):
        wait_tile(slot)
        ahead_slot = (slot + depth - 1) % depth
        base = jnp.minimum(i + depth - 1, n - 1) * (tm * TOP_K)
        for r in range(tm):
            for k in range(TOP_K):
                start(dest_ref[base + r * TOP_K + k], r, k, ahead_slot)
        gates = rg_ref[...]
        gate = [jnp.broadcast_to(gates[:, k:k + 1], (tm, LANES)) for k in range(TOP_K)]
        half = out_ref.shape[1] // 2
        for j in range(half // LANES):
            words = slice(j * LANES, (j + 1) * LANES)
            moe_lo = moe_hi = None
            for k in range(TOP_K):
                lo, hi = _unpack_halves(ybufs[slot][k, :, words])
                moe_lo = lo * gate[k] if moe_lo is None else moe_lo + lo * gate[k]
                moe_hi = hi * gate[k] if moe_hi is None else moe_hi + hi * gate[k]
            for cols, moe in ((words, moe_lo), (slice(half + j * LANES, half + (j + 1) * LANES), moe_hi)):
                out_ref[:, cols] = x_ref[:, cols] + g2_ref[:, cols] * moe
        if final_norm:
            v = out_ref[...]
            out_ref[...] = v * lax.rsqrt(jnp.mean(v * v, axis=-1, keepdims=True) + EPS) * fg_ref[...]

        @pl.when(i == n - 1)
        def _():
            for other in range(1, depth):
                wait_tile((slot + other) % depth)

    for static_slot in range(depth):
        @pl.when(i % depth == static_slot)
        def _(static_slot=static_slot):
            tile(static_slot)


def _combine(dest, y, x_flat, rg_flat, g2, final_g, seq, final_norm):
    t, d = x_flat.shape
    tm = _pick(seq, 128)
    grid_spec = pltpu.PrefetchScalarGridSpec(
        num_scalar_prefetch=1,
        grid=(t // tm,),
        in_specs=[
            pl.BlockSpec(memory_space=pl.ANY),
            pl.BlockSpec((tm, d), lambda i, de: (i, 0)),
            pl.BlockSpec((tm, LANES), lambda i, de: (i, 0)),
            pl.BlockSpec((None, 1, d), lambda i, de: ((i * tm) // seq, 0, 0)),
            pl.BlockSpec((1, d), lambda i, de: (0, 0)),
        ],
        out_specs=pl.BlockSpec((tm, d), lambda i, de: (i, 0)),
        scratch_shapes=[pltpu.VMEM((TOP_K, tm, d // 2), jnp.int32)] * GATHER_DEPTH + [
                        pltpu.SemaphoreType.DMA((GATHER_DEPTH,))],
    )
    return pl.pallas_call(
        functools.partial(_combine_kernel, tm=tm, final_norm=final_norm),
        grid_spec=grid_spec,
        out_shape=jax.ShapeDtypeStruct((t, d), F32),
        compiler_params=_params("arbitrary"),
        name="moe_combine",
    )(dest, y, x_flat, rg_flat, g2, final_g.reshape(1, d))


def _routing_tables(eid, rank, counts, rows):
    n_assign = eid.shape[0]
    n_experts = counts.shape[0]
    padded = (counts + rows - 1) // rows * rows
    pends = jnp.cumsum(padded)
    pstarts = pends - padded
    onehot = eid[:, None] == jnp.arange(n_experts, dtype=jnp.int32)[None, :]
    dest = (jnp.sum(jnp.where(onehot, pstarts[None, :], 0), axis=1) + rank).astype(jnp.int32)
    n_blocks = -(-n_assign // rows) + n_experts
    row_tok = jnp.zeros((n_blocks * rows,), jnp.int32).at[dest].set(jnp.arange(n_assign, dtype=jnp.int32) // TOP_K)
    first_row = jnp.arange(n_blocks, dtype=jnp.int32) * rows
    block_exp = jnp.minimum(jnp.sum((pends[None, :] <= first_row[:, None]).astype(jnp.int32), axis=1), n_experts - 1)
    return block_exp.astype(jnp.int32), row_tok, dest


def _block_diag(w):
    nb, bw, _ = w.shape
    eye = jnp.eye(nb, dtype=w.dtype)
    return (eye[:, None, :, None] * w[:, :, None, :]).reshape(nb * bw, nb * bw)


def kernel(x, c, w_ada, b_ada, ln1_g, w_in, conv_dw_w, conv_dw_b, conv_ln_g, conv_ln_b, fox_f_bias, fox_out_g, lru_conv_w, lru_conv_b, lru_w_r, lru_b_r, lru_w_i, lru_b_i, lru_lambda, lru_out_g, w_out, ln2_g, w_router_group, b_router_group, w_router_expert, b_router_expert, w_gate, w_up, w_down, final_g):
    b, s, d = x.shape
    depth = w_ada.shape[0]
    cw = conv_dw_b.shape[-1]
    fw = fox_out_g.shape[-1]
    lw = lru_lambda.shape[-1]
    heads = fox_f_bias.shape[-1]
    hd = fw // heads
    n_groups, per_group = b_router_expert.shape[1:]
    n_experts = n_groups * per_group
    off_f = 2 * cw + 3 * fw
    assert cw % LANES == 0 and lw == cw and fw % cw == 0 and hd == LANES and heads <= LANES
    assert n_groups + n_experts <= LANES

    c_pad = jnp.zeros((8, d), F32).at[:b].set(c)
    mod_all = _ada_mod(c_pad, w_ada, b_ada)

    for l in range(depth):
        mod = mod_all[l, :b].reshape(b, 1, 6 * d)
        sh1, sc1, g1, sh2, sc2, g2 = [mod[..., k * d:(k + 1) * d] for k in range(6)]

        w_l = w_in[l]
        w_main = jnp.concatenate([w_l[:, :off_f], w_l[:, off_f + heads:]], axis=1).astype(BF16)
        w_f = jnp.zeros((d, LANES), BF16).at[:, :heads].set(w_l[:, off_f:off_f + heads].astype(BF16))
        z, zf = _inproj(x, ln1_g[l].reshape(1, d), sc1, sh1, w_main, w_f)

        y_conv = _conv_branch(z, cw, conv_dw_w[l], conv_dw_b[l], conv_ln_g[l], conv_ln_b[l])

        f_bias = jnp.zeros((1, LANES), F32).at[0, :heads].set(fox_f_bias[l])
        cum = _forget_cumsum(zf, f_bias)
        o_fox = _attention(z, cum, heads, hd, 2 * cw // hd, (2 * cw + fw) // hd, (2 * cw + 2 * fw) // hd)

        y_lru = _lru_branch(z, lw, off_f // lw, lru_conv_w[l], lru_conv_b[l],
                            _block_diag(lru_w_r[l]).astype(BF16), lru_b_r[l],
                            _block_diag(lru_w_i[l]).astype(BF16), lru_b_i[l], lru_lambda[l], lru_out_g[l])

        wo = w_out[l].astype(BF16)
        x = _outproj(y_conv, o_fox, y_lru, fox_out_g[l], wo[:cw], wo[cw:cw + fw], wo[cw + fw:], x, g1)

        w_r = jnp.concatenate(
            [w_router_group[l], w_router_expert[l].transpose(1, 0, 2).reshape(d, n_experts)], axis=1)
        w_r = jnp.zeros((d, LANES), F32).at[:, :n_groups + n_experts].set(w_r)
        w_hi = w_r.astype(BF16)
        w_lo = (w_r - w_hi.astype(F32)).astype(BF16)
        r_bias = jnp.zeros((1, LANES), F32).at[0, :n_groups + n_experts].set(
            jnp.concatenate([b_router_group[l], b_router_expert[l].reshape(-1)]))
        h2, ri, rg, cnt = _router(x, ln2_g[l].reshape(1, d), sc2, sh2, w_hi, w_lo, r_bias, n_groups, per_group)

        eid = ri[:, :, :TOP_K].reshape(-1)
        rank = ri[:, :, TOP_K:2 * TOP_K].reshape(-1)
        counts = cnt[0, n_groups:n_groups + n_experts].astype(jnp.int32)
        block_exp, row_tok, dest = _routing_tables(eid, rank, counts, MOE_ROWS)
        y = _moe_experts(block_exp, row_tok, h2.reshape(b * s, d // 2), w_gate, w_up, w_down, l)
        x = _combine(dest, y, x.reshape(b * s, d), rg.reshape(b * s, LANES), g2, final_g, s,
                     final_norm=(l == depth - 1)).reshape(b, s, d)

    return x
```

```python
import functools

import jax
import jax.numpy as jnp
from jax import lax
from jax.experimental import pallas as pl
from jax.experimental.pallas import tpu as pltpu

EPS = 1e-6
LRU_C = 8.0
TOP_K = 2
LANES = 128
MOE_ROWS = 128
GATHER_DEPTH = 5
EXPERT_GATHER_DEPTH = 8
VMEM_LIMIT = 56 * 1024 * 1024

F32 = jnp.float32
BF16 = jnp.bfloat16


def _pick(n, pref):
    if n <= pref:
        return n
    for t in range(pref, 7, -1):
        if n % t == 0 and t % 8 == 0:
            return t
    raise ValueError((n, pref))


def _params(*sem):
    return pltpu.CompilerParams(dimension_semantics=sem, vmem_limit_bytes=VMEM_LIMIT)


def _sigmoid(x):
    return jax.nn.sigmoid(x)


def _ada_kernel(c_ref, w_ref, b_ref, o_ref):
    c = c_ref[...]
    cond = (c * _sigmoid(c)).astype(BF16)
    o_ref[...] = jnp.dot(cond, w_ref[...].astype(BF16), preferred_element_type=F32) + b_ref[...]


def _ada_mod(c_pad, w_ada, b_ada):
    depth, d, n = w_ada.shape
    rows = c_pad.shape[0]
    tn = _pick(n, 1024)
    return pl.pallas_call(
        _ada_kernel,
        grid=(depth, n // tn),
        in_specs=[
            pl.BlockSpec((rows, d), lambda l, j: (0, 0)),
            pl.BlockSpec((None, d, tn), lambda l, j: (l, 0, j)),
            pl.BlockSpec((None, 1, tn), lambda l, j: (l, 0, j)),
        ],
        out_specs=pl.BlockSpec((None, rows, tn), lambda l, j: (l, 0, j)),
        out_shape=jax.ShapeDtypeStruct((depth, rows, n), F32),
        compiler_params=_params("parallel", "parallel"),
        name="ada_mod",
    )(c_pad, w_ada, b_ada.reshape(depth, 1, n))


def _modulated_norm(x, g, sc, sh):
    y = x * lax.rsqrt(jnp.mean(x * x, axis=-1, keepdims=True) + EPS) * g
    return y * (1.0 + sc) + sh


def _inproj_kernel(x_ref, g_ref, sc_ref, sh_ref, w_ref, wf_ref, z_ref, zf_ref, h_scr):
    @pl.when(pl.program_id(2) == 0)
    def _():
        hb = _modulated_norm(x_ref[...], g_ref[...], sc_ref[...], sh_ref[...]).astype(BF16)
        h_scr[...] = hb
        zf_ref[...] = jnp.dot(hb, wf_ref[...], preferred_element_type=F32)

    z_ref[...] = jnp.dot(h_scr[...], w_ref[...], preferred_element_type=F32)


def _inproj(x, g, sc, sh, w_main, w_f):
    b, s, d = x.shape
    n = w_main.shape[1]
    tm = _pick(s, 1024)
    tn = _pick(n, 1280)
    return pl.pallas_call(
        _inproj_kernel,
        grid=(b, s // tm, n // tn),
        in_specs=[
            pl.BlockSpec((None, tm, d), lambda bi, i, j: (bi, i, 0)),
            pl.BlockSpec((1, d), lambda bi, i, j: (0, 0)),
            pl.BlockSpec((None, 1, d), lambda bi, i, j: (bi, 0, 0)),
            pl.BlockSpec((None, 1, d), lambda bi, i, j: (bi, 0, 0)),
            pl.BlockSpec((d, tn), lambda bi, i, j: (0, j)),
            pl.BlockSpec((d, LANES), lambda bi, i, j: (0, 0)),
        ],
        out_specs=[
            pl.BlockSpec((None, tm, tn), lambda bi, i, j: (bi, i, j)),
            pl.BlockSpec((None, tm, LANES), lambda bi, i, j: (bi, i, 0)),
        ],
        out_shape=[jax.ShapeDtypeStruct((b, s, n), F32), jax.ShapeDtypeStruct((b, s, LANES), F32)],
        scratch_shapes=[pltpu.VMEM((tm, d), BF16)],
        compiler_params=_params("parallel", "parallel", "arbitrary"),
        name="inproj",
    )(x, g, sc, sh, w_main, w_f)


def _conv_kernel(a_ref, g_ref, ah_ref, gh_ref, w_ref, b_ref, lg_ref, lb_ref, o_ref, buf, ybuf, shifted,
                 *, taps, ts, halo, rc):
    cw = a_ref.shape[-1]
    buf[halo:halo + ts, :] = a_ref[...] * _sigmoid(g_ref[...])
    prev = ah_ref[...] * _sigmoid(gh_ref[...])
    buf[0:halo, :] = jnp.where(pl.program_id(1) > 0, prev, 0.0)
    first = halo - (taps - 1)
    for c in range(cw // LANES):
        cs = slice(c * LANES, (c + 1) * LANES)
        for r in range(ts // rc):
            acc = b_ref[:, cs]
            for rem in range(8):
                group = [(k, first + k - rem) for k in range(taps) if (first + k) % 8 == rem]
                if not group:
                    continue
                lo = min(a for _, a in group)
                hi = max(a for _, a in group)
                shifted[0:hi - lo + rc, :] = buf[r * rc + rem + lo:r * rc + rem + hi + rc, cs]
                for k, a in group:
                    acc = acc + w_ref[k:k + 1, cs] * shifted[a - lo:a - lo + rc, :]
            ybuf[r * rc:(r + 1) * rc, cs] = acc
    y = ybuf[...]
    mu = jnp.mean(y, axis=-1, keepdims=True)
    yc = y - mu
    var = jnp.mean(yc * yc, axis=-1, keepdims=True)
    yn = yc * lax.rsqrt(var + EPS) * lg_ref[...] + lb_ref[...]
    o_ref[...] = (yn * _sigmoid(yn)).astype(o_ref.dtype)


def _conv_branch(z, cw, dw_w, dw_b, ln_g, ln_b):
    b, s, _ = z.shape
    taps = dw_w.shape[0]
    halo = 32
    assert taps - 1 <= halo
    ts = _pick(s, 256)
    rc = _pick(ts, 128)
    hb = ts // halo
    kern = functools.partial(_conv_kernel, taps=taps, ts=ts, halo=halo, rc=rc)
    cur = lambda col: pl.BlockSpec((None, ts, cw), lambda bi, i: (bi, i, col))
    prev = lambda col: pl.BlockSpec((None, halo, cw), lambda bi, i: (bi, jnp.maximum(i * hb - 1, 0), col))
    vec = pl.BlockSpec((1, cw), lambda bi, i: (0, 0))
    return pl.pallas_call(
        kern,
        grid=(b, s // ts),
        in_specs=[cur(0), cur(1), prev(0), prev(1),
                  pl.BlockSpec((taps, cw), lambda bi, i: (0, 0)), vec, vec, vec],
        out_specs=pl.BlockSpec((None, ts, cw), lambda bi, i: (bi, i, 0)),
        out_shape=jax.ShapeDtypeStruct((b, s, cw), BF16),
        scratch_shapes=[pltpu.VMEM((halo + ts, cw), F32), pltpu.VMEM((ts, cw), F32),
                        pltpu.VMEM((rc + halo, LANES), F32)],
        compiler_params=_params("parallel", "parallel"),
        name="conv_branch",
    )(z, z, z, z, dw_w, dw_b.reshape(1, cw), ln_g.reshape(1, cw), ln_b.reshape(1, cw))


def _shift_rows(v, d, fill):
    if d % 8 == 0:
        return jnp.concatenate([jnp.full((d, v.shape[1]), fill, v.dtype), v[:v.shape[0] - d]], axis=0)
    rows = lax.broadcasted_iota(jnp.int32, v.shape, 0)
    return jnp.where(rows >= d, pltpu.roll(v, d, axis=0), fill)


def _scan_linear(a, bv):
    d = 1
    while d < a.shape[0]:
        bv = a * _shift_rows(bv, d, 0.0) + bv
        a = a * _shift_rows(a, d, 1.0)
        d *= 2
    return a, bv


def _scan_sum(v):
    d = 1
    while d < v.shape[0]:
        v = v + _shift_rows(v, d, 0.0)
        d *= 2
    return v


def _cum_kernel(zf_ref, fb_ref, o_ref, car, *, ts, rs):
    @pl.when(pl.program_id(1) == 0)
    def _():
        car[...] = jnp.zeros_like(car)

    h = car[...]
    for r in range(ts // rs):
        x = zf_ref[r * rs:(r + 1) * rs, :] + fb_ref[...]
        log_f = jnp.minimum(x, 0.0) - jnp.log1p(jnp.exp(-jnp.abs(x)))
        v = _scan_sum(log_f) + h
        o_ref[r * rs:(r + 1) * rs, :] = v
        h = v[rs - 1:rs, :]
    car[...] = h


def _forget_cumsum(zf, f_bias_pad):
    b, s, _ = zf.shape
    ts = _pick(s, 512)
    rs = _pick(ts, 128)
    return pl.pallas_call(
        functools.partial(_cum_kernel, ts=ts, rs=rs),
        grid=(b, s // ts),
        in_specs=[pl.BlockSpec((None, ts, LANES), lambda bi, i: (bi, i, 0)),
                  pl.BlockSpec((1, LANES), lambda bi, i: (0, 0))],
        out_specs=pl.BlockSpec((None, ts, LANES), lambda bi, i: (bi, i, 0)),
        out_shape=jax.ShapeDtypeStruct((b, s, LANES), F32),
        scratch_shapes=[pltpu.VMEM((1, LANES), F32)],
        compiler_params=_params("parallel", "arbitrary"),
        name="forget_cumsum",
    )(zf, f_bias_pad)


def _gelu_tanh(x):
    return 0.5 * x * (1.0 + jnp.tanh(0.7978845608028654 * (x + 0.044715 * (x * x * x))))


def _lru_kernel(u_ref, uh_ref, gate_ref, cw_ref, cb_ref, wr_ref, br_ref, wi_ref, bi_ref, lam_ref, og_ref,
                o_ref, ubuf, abuf, bbuf, car, *, taps, ts, halo, rs):
    lw = u_ref.shape[-1]

    @pl.when(pl.program_id(1) == 0)
    def _():
        car[...] = jnp.zeros_like(car)

    ubuf[halo:halo + ts, :] = u_ref[...]
    ubuf[0:halo, :] = jnp.where(pl.program_id(1) > 0, uh_ref[...], 0.0)
    xc = cb_ref[...]
    for k in range(taps):
        off = halo - (taps - 1) + k
        xc = xc + cw_ref[k:k + 1, :] * ubuf[off:off + ts, :]
    xb = xc.astype(BF16)
    r = _sigmoid(jnp.dot(xb, wr_ref[...], preferred_element_type=F32) + br_ref[...])
    ig = _sigmoid(jnp.dot(xb, wi_ref[...], preferred_element_type=F32) + bi_ref[...])
    nl = -lam_ref[...]
    softplus = jnp.maximum(nl, 0.0) + jnp.log1p(jnp.exp(-jnp.abs(nl)))
    log_a = (-LRU_C) * r * softplus
    th = jnp.tanh(log_a)
    abuf[...] = jnp.exp(log_a)
    bbuf[...] = jnp.sqrt(-2.0 * th / (1.0 - th)) * (ig * xc)
    for c in range(lw // LANES):
        cs = slice(c * LANES, (c + 1) * LANES)
        h = car[:, cs]
        for q in range(ts // rs):
            rows = slice(q * rs, (q + 1) * rs)
            a_cum, b_cum = _scan_linear(abuf[rows, cs], bbuf[rows, cs])
            hv = b_cum + a_cum * h
            bbuf[rows, cs] = hv
            h = hv[rs - 1:rs, :]
        car[:, cs] = h
    y = bbuf[...] * _gelu_tanh(gate_ref[...])
    y = y * lax.rsqrt(jnp.mean(y * y, axis=-1, keepdims=True) + EPS) * og_ref[...]
    o_ref[...] = y.astype(o_ref.dtype)


def _lru_branch(z, lw, col_x, conv_w, conv_b, wr_bd, b_r, wi_bd, b_i, lam, out_g):
    b, s, _ = z.shape
    taps = conv_w.shape[0]
    halo = 8
    assert taps - 1 <= halo
    ts = _pick(s, 256)
    rs = _pick(ts, 128)
    hb = ts // halo
    kern = functools.partial(_lru_kernel, taps=taps, ts=ts, halo=halo, rs=rs)
    vec = pl.BlockSpec((1, lw), lambda bi, i: (0, 0))
    mat = pl.BlockSpec((lw, lw), lambda bi, i: (0, 0))
    return pl.pallas_call(
        kern,
        grid=(b, s // ts),
        in_specs=[
            pl.BlockSpec((None, ts, lw), lambda bi, i: (bi, i, col_x)),
            pl.BlockSpec((None, halo, lw), lambda bi, i: (bi, jnp.maximum(i * hb - 1, 0), col_x)),
            pl.BlockSpec((None, ts, lw), lambda bi, i: (bi, i, col_x + 1)),
            pl.BlockSpec((taps, lw), lambda bi, i: (0, 0)), vec, mat, vec, mat, vec, vec, vec,
        ],
        out_specs=pl.BlockSpec((None, ts, lw), lambda bi, i: (bi, i, 0)),
        out_shape=jax.ShapeDtypeStruct((b, s, lw), BF16),
        scratch_shapes=[pltpu.VMEM((halo + ts, lw), F32), pltpu.VMEM((ts, lw), F32),
                        pltpu.VMEM((ts, lw), F32), pltpu.VMEM((1, lw), F32)],
        compiler_params=_params("parallel", "arbitrary"),
        name="lru_branch",
    )(z, z, z, conv_w, conv_b.reshape(1, lw), wr_bd, b_r.reshape(1, lw), wi_bd, b_i.reshape(1, lw),
      lam.reshape(1, lw), out_g.reshape(1, lw))


def _split3(t):
    hi = t.astype(BF16).astype(F32)
    r = t - hi
    mid = r.astype(BF16).astype(F32)
    return hi, mid, r - mid


def _attn_kernel(q_ref, k_ref, v_ref, cum_ref, o_ref, kaug, vaug, qaug, s_a, s_b, p_a, p_b, al_a, al_b, m_ref, acc_ref,
                 *, tq, tk, scale):
    h = pl.program_id(1)
    i = pl.program_id(2)
    s_len, hd = k_ref.shape
    inv_scale = 1.0 / scale
    c_exp = scale * 1.4426950408889634
    s_buf, p_buf, al_buf = (s_a, s_b), (p_a, p_b), (al_a, al_b)

    def head_cum(start, size):
        lane = lax.broadcasted_iota(jnp.int32, (size, LANES), 1)
        rows = pl.ds(pl.multiple_of(start, size), size)
        return jnp.sum(jnp.where(lane == h, cum_ref[rows, :], 0.0), axis=-1, keepdims=True) * inv_scale

    def key_rows(j):
        return pl.ds(pl.multiple_of(j * tk, tk), tk)

    @pl.when(i == 0)
    def _():
        lane = lax.broadcasted_iota(jnp.int32, (tk, LANES), 1)

        def body(t, carry):
            rows = key_rows(t)
            hi, mid, lo = _split3(-head_cum(t * tk, tk))
            aug = jnp.where(lane < 3, 1.0, jnp.where(lane == 3, hi, jnp.where(lane == 4, mid,
                            jnp.where(lane == 5, lo, 0.0))))
            kaug[t, 0:hd, :] = k_ref[rows, :].T.astype(BF16)
            kaug[t, hd:2 * hd, :] = aug.T.astype(BF16)
            vaug[rows, 0:hd] = v_ref[rows, :].astype(BF16)
            vaug[rows, hd:2 * hd] = jnp.where(lane == 0, 1.0, 0.0).astype(BF16)
            return carry

        lax.fori_loop(0, s_len // tk, body, 0)

    lane_q = lax.broadcasted_iota(jnp.int32, (tq, LANES), 1)
    hi, mid, lo = _split3(head_cum(i * tq, tq))
    qa = jnp.where(lane_q == 0, hi, jnp.where(lane_q == 1, mid, jnp.where(lane_q == 2, lo,
                   jnp.where(lane_q < 6, 1.0, 0.0))))
    qaug[:, 0:hd] = q_ref[...].astype(BF16)
    qaug[:, hd:2 * hd] = qa.astype(BF16)

    def scores_into(dst, j, masked=False):
        s = jnp.dot(qaug[...], kaug[j], preferred_element_type=F32)
        if masked:
            row = lax.broadcasted_iota(jnp.int32, s.shape, 0)
            col = lax.broadcasted_iota(jnp.int32, s.shape, 1)
            s = jnp.where(col <= row, s, -jnp.inf)
        dst[...] = s

    def softmax_into(src, dst_p, dst_alpha):
        s = src[...]
        m_old = m_ref[...]
        m_new = jnp.maximum(m_old, jnp.max(s, axis=-1, keepdims=True))
        dst_alpha[...] = jnp.exp2((m_old - m_new) * c_exp)
        m_ref[...] = m_new
        dst_p[...] = jnp.exp2((s - jnp.concatenate([m_new] * (tk // LANES), axis=1)) * c_exp).astype(BF16)

    def accumulate(src_alpha, src_p, j):
        alpha = jnp.concatenate([src_alpha[...]] * (acc_ref.shape[1] // LANES), axis=1)
        acc_ref[...] = alpha * acc_ref[...] + jnp.dot(src_p[...], vaug[key_rows(j), :], preferred_element_type=F32)

    def blk(t):
        return jnp.where(t == 0, i, jnp.maximum(t - 1, 0))

    def tick(t, par):
        scores_into(s_buf[1 - par], t)
        softmax_into(s_buf[par], p_buf[par], al_buf[par])
        accumulate(al_buf[1 - par], p_buf[1 - par], blk(t - 1))

    scores_into(s_buf[0], i, masked=True)
    m_ref[...] = jnp.full(m_ref.shape, -jnp.inf, F32)
    acc_ref[...] = jnp.zeros(acc_ref.shape, F32)
    p_buf[1][...] = jnp.zeros(p_buf[1].shape, BF16)
    al_buf[1][...] = jnp.ones(al_buf[1].shape, F32)

    def body(u, carry):
        tick(2 * u, 0)
        tick(2 * u + 1, 1)
        return carry

    lax.fori_loop(0, i // 2, body, 0)
    odd = i % 2 == 1

    @pl.when(odd)
    def _():
        tick(i - 1, 0)

    def drain(par):
        softmax_into(s_buf[par], p_buf[par], al_buf[par])
        accumulate(al_buf[1 - par], p_buf[1 - par], blk(i - 1))
        accumulate(al_buf[par], p_buf[par], blk(i))

    @pl.when(odd)
    def _():
        drain(1)

    @pl.when(jnp.logical_not(odd))
    def _():
        drain(0)

    o_ref[...] = acc_ref[:, 0:hd] / acc_ref[:, hd:hd + 1]


def _attention(z, cum, heads, hd, col_q, col_k, col_v):
    b, s, _ = z.shape
    tq = _pick(s, 512)
    tk = tq
    kern = functools.partial(_attn_kernel, tq=tq, tk=tk, scale=hd ** -0.5)
    return pl.pallas_call(
        kern,
        grid=(b, heads, s // tq),
        in_specs=[
            pl.BlockSpec((None, tq, hd), lambda bi, h, i: (bi, i, col_q + h)),
            pl.BlockSpec((None, s, hd), lambda bi, h, i: (bi, 0, col_k + h)),
            pl.BlockSpec((None, s, hd), lambda bi, h, i: (bi, 0, col_v + h)),
            pl.BlockSpec((None, s, LANES), lambda bi, h, i: (bi, 0, 0)),
        ],
        out_specs=pl.BlockSpec((None, tq, hd), lambda bi, h, i: (bi, i, h)),
        out_shape=jax.ShapeDtypeStruct((b, s, heads * hd), F32),
        scratch_shapes=[pltpu.VMEM((s // tk, 2 * hd, tk), BF16), pltpu.VMEM((s, 2 * hd), BF16),
                        pltpu.VMEM((tq, 2 * hd), BF16),
                        pltpu.VMEM((tq, tk), F32), pltpu.VMEM((tq, tk), F32),
                        pltpu.VMEM((tq, tk), BF16), pltpu.VMEM((tq, tk), BF16),
                        pltpu.VMEM((tq, LANES), F32), pltpu.VMEM((tq, LANES), F32), pltpu.VMEM((tq, LANES), F32),
                        pltpu.VMEM((tq, 2 * hd), F32)],
        compiler_params=_params("arbitrary", "arbitrary", "arbitrary"),
        name="fox_attention",
    )(z, z, z, cum)


def _outproj_kernel(yc_ref, o_ref, yl_ref, fg_ref, w1_ref, w2_ref, w3_ref, x_ref, g1_ref, out_ref, yf_scr):
    @pl.when(pl.program_id(2) == 0)
    def _():
        o = o_ref[...]
        yf = o * lax.rsqrt(jnp.mean(o * o, axis=-1, keepdims=True) + EPS) * fg_ref[...]
        yf_scr[...] = yf.astype(BF16)

    y = jnp.dot(yc_ref[...], w1_ref[...], preferred_element_type=F32)
    y = y + jnp.dot(yf_scr[...], w2_ref[...], preferred_element_type=F32)
    y = y + jnp.dot(yl_ref[...], w3_ref[...], preferred_element_type=F32)
    out_ref[...] = x_ref[...] + g1_ref[...] * y


def _outproj(y_conv, o_fox, y_lru, fox_g, w1, w2, w3, x, g1):
    b, s, d = x.shape
    cw, fw, lw = y_conv.shape[-1], o_fox.shape[-1], y_lru.shape[-1]
    tm = _pick(s, 512)
    tn = _pick(d, 1024)
    act = lambda w: pl.BlockSpec((None, tm, w), lambda bi, i, j: (bi, i, 0))
    wgt = lambda w: pl.BlockSpec((w, tn), lambda bi, i, j: (0, j))
    return pl.pallas_call(
        _outproj_kernel,
        grid=(b, s // tm, d // tn),
        in_specs=[act(cw), act(fw), act(lw), pl.BlockSpec((1, fw), lambda bi, i, j: (0, 0)),
                  wgt(cw), wgt(fw), wgt(lw),
                  pl.BlockSpec((None, tm, tn), lambda bi, i, j: (bi, i, j)),
                  pl.BlockSpec((None, 1, tn), lambda bi, i, j: (bi, 0, j))],
        out_specs=pl.BlockSpec((None, tm, tn), lambda bi, i, j: (bi, i, j)),
        out_shape=jax.ShapeDtypeStruct((b, s, d), F32),
        scratch_shapes=[pltpu.VMEM((tm, fw), BF16)],
        compiler_params=_params("parallel", "parallel", "arbitrary"),
        name="outproj",
    )(y_conv, o_fox, y_lru, fox_g.reshape(1, fw), w1, w2, w3, x, g1)


def _bf16_bits(v):
    return lax.bitcast_convert_type(v.astype(BF16).astype(F32), jnp.uint32)


def _pack_halves(v):
    half = v.shape[1] // 2
    word = _bf16_bits(v[:, half:]) | (_bf16_bits(v[:, :half]) >> 16)
    return lax.bitcast_convert_type(word, jnp.int32)


def _unpack_halves(w):
    word = lax.bitcast_convert_type(w, jnp.uint32)
    lo = lax.bitcast_convert_type(word << 16, F32)
    hi = lax.bitcast_convert_type(word & jnp.uint32(0xFFFF0000), F32)
    return lo, hi


def _router_kernel(x_ref, g_ref, sc_ref, sh_ref, whi_ref, wlo_ref, rb_ref, h2_ref, ri_ref, rg_ref, cnt_ref, car,
                   *, n_groups, per_group):
    @pl.when(jnp.logical_and(pl.program_id(0) == 0, pl.program_id(1) == 0))
    def _():
        car[...] = jnp.zeros_like(car)

    h2 = _modulated_norm(x_ref[...], g_ref[...], sc_ref[...], sh_ref[...])
    h2_ref[...] = _pack_halves(h2)
    hi = h2.astype(BF16)
    lo = (h2 - hi.astype(F32)).astype(BF16)
    logits = (jnp.dot(hi, whi_ref[...], preferred_element_type=F32)
              + jnp.dot(lo, whi_ref[...], preferred_element_type=F32)
              + jnp.dot(hi, wlo_ref[...], preferred_element_type=F32)) + rb_ref[...]
    lane_i = lax.broadcasted_iota(jnp.int32, logits.shape, 1)
    lane = lane_i.astype(F32)
    big = float(LANES)

    def first_lane(hit):
        return jnp.min(jnp.where(hit, lane, big), axis=-1, keepdims=True)

    gmask = lane_i < n_groups
    lg1 = jnp.where(gmask, logits, -jnp.inf)
    m1 = jnp.max(lg1, axis=-1, keepdims=True)
    e1 = jnp.exp(lg1 - m1)
    p1 = e1 / jnp.sum(e1, axis=-1, keepdims=True)
    p_grp = jnp.max(p1, axis=-1, keepdims=True)
    grp = first_lane((p1 == p_grp) & gmask)
    lo_lane = n_groups + grp * per_group
    emask = (lane >= lo_lane) & (lane < lo_lane + per_group)
    lg2 = jnp.where(emask, logits, -jnp.inf)
    m2 = jnp.max(lg2, axis=-1, keepdims=True)
    e2 = jnp.exp(lg2 - m2)
    p2 = e2 / jnp.sum(e2, axis=-1, keepdims=True)
    v1 = jnp.max(jnp.where(emask, p2, -1.0), axis=-1, keepdims=True)
    i1 = first_lane((p2 == v1) & emask)
    rest = emask & (lane != i1)
    v2 = jnp.max(jnp.where(rest, p2, -1.0), axis=-1, keepdims=True)
    i2 = first_lane((p2 == v2) & rest)
    denom = v1 + v2
    gate1 = p_grp * (v1 / denom)
    gate2 = p_grp * (v2 / denom)
    tm = logits.shape[0]
    picked = jnp.where((lane == i1) | (lane == i2), 1.0, 0.0)
    tri = jnp.where(lax.broadcasted_iota(jnp.int32, (tm, tm), 0) > lax.broadcasted_iota(jnp.int32, (tm, tm), 1),
                    1.0, 0.0).astype(BF16)
    before = jnp.dot(tri, picked.astype(BF16), preferred_element_type=F32) + car[...]
    rank1 = jnp.sum(jnp.where(lane == i1, before, 0.0), axis=-1, keepdims=True)
    rank2 = jnp.sum(jnp.where(lane == i2, before, 0.0), axis=-1, keepdims=True)
    total = car[...] + jnp.sum(picked, axis=0, keepdims=True)
    car[...] = total
    cnt_ref[...] = total
    info = jnp.where(lane_i == 0, i1 - n_groups, jnp.where(lane_i == 1, i2 - n_groups,
                     jnp.where(lane_i == 2, rank1, jnp.where(lane_i == 3, rank2, 0.0))))
    ri_ref[...] = info.astype(jnp.int32)
    rg_ref[...] = jnp.where(lane_i == 0, gate1, jnp.where(lane_i == 1, gate2, 0.0))


def _router(x, g, sc, sh, w_hi, w_lo, r_bias, n_groups, per_group):
    b, s, d = x.shape
    tm = _pick(s, 512)
    kern = functools.partial(_router_kernel, n_groups=n_groups, per_group=per_group)
    row = lambda w: pl.BlockSpec((None, tm, w), lambda bi, i: (bi, i, 0))
    mod = pl.BlockSpec((None, 1, d), lambda bi, i: (bi, 0, 0))
    wsp = pl.BlockSpec((d, LANES), lambda bi, i: (0, 0))
    return pl.pallas_call(
        kern,
        grid=(b, s // tm),
        in_specs=[row(d), pl.BlockSpec((1, d), lambda bi, i: (0, 0)), mod, mod, wsp, wsp,
                  pl.BlockSpec((1, LANES), lambda bi, i: (0, 0))],
        out_specs=[row(d // 2), row(LANES), row(LANES), pl.BlockSpec((1, LANES), lambda bi, i: (0, 0))],
        out_shape=[jax.ShapeDtypeStruct((b, s, d // 2), jnp.int32), jax.ShapeDtypeStruct((b, s, LANES), jnp.int32),
                   jax.ShapeDtypeStruct((b, s, LANES), F32), jax.ShapeDtypeStruct((1, LANES), F32)],
        scratch_shapes=[pltpu.VMEM((1, LANES), F32)],
        compiler_params=_params("arbitrary", "arbitrary"),
        name="router",
    )(x, g, sc, sh, w_hi, w_lo, r_bias)


def _moe_kernel(bexp_ref, rtok_ref, wslot_ref, nexte_ref, h2_hbm, wg_hbm, wu_hbm, wd_hbm, y_ref,
                *scratch, rows, layer):
    xbufs = scratch[:EXPERT_GATHER_DEPTH]
    sem, wg_bf, wu_bf, wd_bf, wg_raw, wu_raw, wd_raw, wsem = scratch[EXPERT_GATHER_DEPTH:]
    b = pl.program_id(0)
    nb = pl.num_programs(0)
    depth = EXPERT_GATHER_DEPTH

    def row_copy(tok, r, slot):
        return pltpu.make_async_copy(h2_hbm.at[pl.ds(tok, 1)], xbufs[slot].at[pl.ds(r, 1)], sem.at[slot])

    def wait_block(slot):
        pltpu.make_async_copy(h2_hbm.at[pl.ds(0, rows)], xbufs[slot], sem.at[slot]).wait()

    @pl.when(b == 0)
    def _():
        for ahead in range(depth - 1):
            def body(r, carry, ahead=ahead):
                row_copy(rtok_ref[ahead * rows + r], r, ahead).start()
                return carry

            lax.fori_loop(0, rows, body, 0)

    def weight_copies(e, s):
        return [pltpu.make_async_copy(src.at[layer, e], dst.at[s], wsem.at[s])
                for src, dst in ((wg_hbm, wg_raw), (wu_hbm, wu_raw), (wd_hbm, wd_raw))]

    @pl.when(b == 0)
    def _():
        for copy in weight_copies(bexp_ref[0], 0):
            copy.start()

    changed = jnp.logical_or(b == 0, bexp_ref[b] != bexp_ref[jnp.maximum(b - 1, 0)])

    @pl.when(changed)
    def _():
        s = wslot_ref[b]
        for copy in weight_copies(bexp_ref[b], s):
            copy.wait()
        for static_s in range(2):
            @pl.when(s == static_s)
            def _():
                wg_bf[...] = wg_raw[static_s].astype(BF16)
                wu_bf[...] = wu_raw[static_s].astype(BF16)
                wd_bf[...] = wd_raw[static_s].astype(BF16)

        nxt = nexte_ref[b]

        @pl.when(nxt >= 0)
        def _():
            for copy in weight_copies(nxt, 1 - s):
                copy.start()

    def block(slot):
        wait_block(slot)
        ahead_slot = (slot + depth - 1) % depth
        base = jnp.minimum(b + depth - 1, nb - 1) * rows
        for r in range(rows):
            row_copy(rtok_ref[base + r], r, ahead_slot).start(priority=r % 2)
        lo, hi = _unpack_halves(xbufs[slot][...])
        xb = jnp.concatenate([lo.astype(BF16), hi.astype(BF16)], axis=1)
        gt = jnp.dot(xb, wg_bf[...], preferred_element_type=F32)
        up = jnp.dot(xb, wu_bf[...], preferred_element_type=F32)
        hb = (gt * _sigmoid(gt) * up).astype(BF16)
        y_ref[...] = _pack_halves(jnp.dot(hb, wd_bf[...], preferred_element_type=F32))

        @pl.when(b == nb - 1)
        def _():
            for other in range(1, depth):
                wait_block((slot + other) % depth)

    for static_slot in range(depth):
        @pl.when(b % depth == static_slot)
        def _(static_slot=static_slot):
            block(static_slot)


def _moe_experts(block_exp, row_tok, h2_rows, w_gate, w_up, w_down, layer):
    _, _, d, f = w_gate.shape
    rows = MOE_ROWS
    n_blocks = block_exp.shape[0]
    changed = jnp.concatenate([jnp.ones((1,), jnp.int32), (block_exp[1:] != block_exp[:-1]).astype(jnp.int32)])
    w_slot = (jnp.cumsum(changed) - 1) % 2
    n_le = jnp.sum((block_exp[None, :] <= block_exp[:, None]).astype(jnp.int32), axis=1)
    next_exp = jnp.where(n_le < n_blocks, block_exp[jnp.minimum(n_le, n_blocks - 1)], -1)
    any_spec = pl.BlockSpec(memory_space=pl.ANY)
    grid_spec = pltpu.PrefetchScalarGridSpec(
        num_scalar_prefetch=4,
        grid=(n_blocks,),
        in_specs=[any_spec, any_spec, any_spec, any_spec],
        out_specs=pl.BlockSpec((rows, d // 2), lambda bi, *_: (bi, 0)),
        scratch_shapes=[pltpu.VMEM((rows, d // 2), jnp.int32)] * EXPERT_GATHER_DEPTH + [
                        pltpu.SemaphoreType.DMA((EXPERT_GATHER_DEPTH,)),
                        pltpu.VMEM((d, f), BF16), pltpu.VMEM((d, f), BF16), pltpu.VMEM((f, d), BF16),
                        pltpu.VMEM((2, d, f), F32), pltpu.VMEM((2, d, f), F32), pltpu.VMEM((2, f, d), F32),
                        pltpu.SemaphoreType.DMA((2,))],
    )
    return pl.pallas_call(
        functools.partial(_moe_kernel, rows=rows, layer=layer),
        grid_spec=grid_spec,
        out_shape=jax.ShapeDtypeStruct((n_blocks * rows, d // 2), jnp.int32),
        compiler_params=_params("arbitrary"),
        name="moe_experts",
    )(block_exp, row_tok, w_slot.astype(jnp.int32), next_exp.astype(jnp.int32), h2_rows, w_gate, w_up, w_down)


def _combine_kernel(dest_ref, y_hbm, x_ref, rg_ref, g2_ref, fg_ref, out_ref, *scratch, tm, final_norm):
    ybufs, sem = scratch[:GATHER_DEPTH], scratch[GATHER_DEPTH]
    i = pl.program_id(0)
    n = pl.num_programs(0)
    depth = GATHER_DEPTH

    def start(row, r, k, slot):
        copy = pltpu.make_async_copy(y_hbm.at[pl.ds(row, 1)], ybufs[slot].at[k, pl.ds(r, 1)], sem.at[slot])
        copy.start(priority=k % 2)

    def wait_tile(slot):
        for k in range(TOP_K):
            pltpu.make_async_copy(y_hbm.at[pl.ds(0, tm)], ybufs[slot].at[k], sem.at[slot]).wait()

    @pl.when(i == 0)
    def _():
        for ahead in range(depth - 1):
            def body(r, carry, ahead=ahead):
                for k in range(TOP_K):
                    start(dest_ref[(ahead * tm + r) * TOP_K + k], r, k, ahead)
                return carry

            lax.fori_loop(0, tm, body, 0)

    def tile(slot):
        wait_tile(slot)
        ahead_slot = (slot + depth - 1) % depth
        base = jnp.minimum(i + depth - 1, n - 1) * (tm * TOP_K)
        for r in range(tm):
            for k in range(TOP_K):
                start(dest_ref[base + r * TOP_K + k], r, k, ahead_slot)
        gates = rg_ref[...]
        gate = [jnp.broadcast_to(gates[:, k:k + 1], (tm, LANES)) for k in range(TOP_K)]
        half = out_ref.shape[1] // 2
        for j in range(half // LANES):
            words = slice(j * LANES, (j + 1) * LANES)
            moe_lo = moe_hi = None
            for k in range(TOP_K):
                lo, hi = _unpack_halves(ybufs[slot][k, :, words])
                moe_lo = lo * gate[k] if moe_lo is None else moe_lo + lo * gate[k]
                moe_hi = hi * gate[k] if moe_hi is None else moe_hi + hi * gate[k]
            for cols, moe in ((words, moe_lo), (slice(half + j * LANES, half + (j + 1) * LANES), moe_hi)):
                out_ref[:, cols] = x_ref[:, cols] + g2_ref[:, cols] * moe
        if final_norm:
            v = out_ref[...]
            out_ref[...] = v * lax.rsqrt(jnp.mean(v * v, axis=-1, keepdims=True) + EPS) * fg_ref[...]

        @pl.when(i == n - 1)
        def _():
            for other in range(1, depth):
                wait_tile((slot + other) % depth)

    for static_slot in range(depth):
        @pl.when(i % depth == static_slot)
        def _(static_slot=static_slot):
            tile(static_slot)


def _combine(dest, y, x_flat, rg_flat, g2, final_g, seq, final_norm):
    t, d = x_flat.shape
    tm = _pick(seq, 128)
    grid_spec = pltpu.PrefetchScalarGridSpec(
        num_scalar_prefetch=1,
        grid=(t // tm,),
        in_specs=[
            pl.BlockSpec(memory_space=pl.ANY),
            pl.BlockSpec((tm, d), lambda i, de: (i, 0)),
            pl.BlockSpec((tm, LANES), lambda i, de: (i, 0)),
            pl.BlockSpec((None, 1, d), lambda i, de: ((i * tm) // seq, 0, 0)),
            pl.BlockSpec((1, d), lambda i, de: (0, 0)),
        ],
        out_specs=pl.BlockSpec((tm, d), lambda i, de: (i, 0)),
        scratch_shapes=[pltpu.VMEM((TOP_K, tm, d // 2), jnp.int32)] * GATHER_DEPTH + [
                        pltpu.SemaphoreType.DMA((GATHER_DEPTH,))],
    )
    return pl.pallas_call(
        functools.partial(_combine_kernel, tm=tm, final_norm=final_norm),
        grid_spec=grid_spec,
        out_shape=jax.ShapeDtypeStruct((t, d), F32),
        compiler_params=_params("arbitrary"),
        name="moe_combine",
    )(dest, y, x_flat, rg_flat, g2, final_g.reshape(1, d))


def _routing_tables(eid, rank, counts, rows):
    n_assign = eid.shape[0]
    n_experts = counts.shape[0]
    padded = (counts + rows - 1) // rows * rows
    pends = jnp.cumsum(padded)
    pstarts = pends - padded
    onehot = eid[:, None] == jnp.arange(n_experts, dtype=jnp.int32)[None, :]
    dest = (jnp.sum(jnp.where(onehot, pstarts[None, :], 0), axis=1) + rank).astype(jnp.int32)
    n_blocks = -(-n_assign // rows) + n_experts
    row_tok = jnp.zeros((n_blocks * rows,), jnp.int32).at[dest].set(jnp.arange(n_assign, dtype=jnp.int32) // TOP_K)
    first_row = jnp.arange(n_blocks, dtype=jnp.int32) * rows
    block_exp = jnp.minimum(jnp.sum((pends[None, :] <= first_row[:, None]).astype(jnp.int32), axis=1), n_experts - 1)
    return block_exp.astype(jnp.int32), row_tok, dest


def _block_diag(w):
    nb, bw, _ = w.shape
    eye = jnp.eye(nb, dtype=w.dtype)
    return (eye[:, None, :, None] * w[:, :, None, :]).reshape(nb * bw, nb * bw)


def kernel(x, c, w_ada, b_ada, ln1_g, w_in, conv_dw_w, conv_dw_b, conv_ln_g, conv_ln_b, fox_f_bias, fox_out_g, lru_conv_w, lru_conv_b, lru_w_r, lru_b_r, lru_w_i, lru_b_i, lru_lambda, lru_out_g, w_out, ln2_g, w_router_group, b_router_group, w_router_expert, b_router_expert, w_gate, w_up, w_down, final_g):
    b, s, d = x.shape
    depth = w_ada.shape[0]
    cw = conv_dw_b.shape[-1]
    fw = fox_out_g.shape[-1]
    lw = lru_lambda.shape[-1]
    heads = fox_f_bias.shape[-1]
    hd = fw // heads
    n_groups, per_group = b_router_expert.shape[1:]
    n_experts = n_groups * per_group
    off_f = 2 * cw + 3 * fw
    assert cw % LANES == 0 and lw == cw and fw % cw == 0 and hd == LANES and heads <= LANES
    assert n_groups + n_experts <= LANES

    c_pad = jnp.zeros((8, d), F32).at[:b].set(c)
    mod_all = _ada_mod(c_pad, w_ada, b_ada)

    for l in range(depth):
        mod = mod_all[l, :b].reshape(b, 1, 6 * d)
        sh1, sc1, g1, sh2, sc2, g2 = [mod[..., k * d:(k + 1) * d] for k in range(6)]

        w_l = w_in[l]
        w_main = jnp.concatenate([w_l[:, :off_f], w_l[:, off_f + heads:]], axis=1).astype(BF16)
        w_f = jnp.zeros((d, LANES), BF16).at[:, :heads].set(w_l[:, off_f:off_f + heads].astype(BF16))
        z, zf = _inproj(x, ln1_g[l].reshape(1, d), sc1, sh1, w_main, w_f)

        y_conv = _conv_branch(z, cw, conv_dw_w[l], conv_dw_b[l], conv_ln_g[l], conv_ln_b[l])

        f_bias = jnp.zeros((1, LANES), F32).at[0, :heads].set(fox_f_bias[l])
        cum = _forget_cumsum(zf, f_bias)
        o_fox = _attention(z, cum, heads, hd, 2 * cw // hd, (2 * cw + fw) // hd, (2 * cw + 2 * fw) // hd)

        y_lru = _lru_branch(z, lw, off_f // lw, lru_conv_w[l], lru_conv_b[l],
                            _block_diag(lru_w_r[l]).astype(BF16), lru_b_r[l],
                            _block_diag(lru_w_i[l]).astype(BF16), lru_b_i[l], lru_lambda[l], lru_out_g[l])

        wo = w_out[l].astype(BF16)
        x = _outproj(y_conv, o_fox, y_lru, fox_out_g[l], wo[:cw], wo[cw:cw + fw], wo[cw + fw:], x, g1)

        w_r = jnp.concatenate(
            [w_router_group[l], w_router_expert[l].transpose(1, 0, 2).reshape(d, n_experts)], axis=1)
        w_r = jnp.zeros((d, LANES), F32).at[:, :n_groups + n_experts].set(w_r)
        w_hi = w_r.astype(BF16)
        w_lo = (w_r - w_hi.astype(F32)).astype(BF16)
        r_bias = jnp.zeros((1, LANES), F32).at[0, :n_groups + n_experts].set(
            jnp.concatenate([b_router_group[l], b_router_expert[l].reshape(-1)]))
        h2, ri, rg, cnt = _router(x, ln2_g[l].reshape(1, d), sc2, sh2, w_hi, w_lo, r_bias, n_groups, per_group)

        eid = ri[:, :, :TOP_K].reshape(-1)
        rank = ri[:, :, TOP_K:2 * TOP_K].reshape(-1)
        counts = cnt[0, n_groups:n_groups + n_experts].astype(jnp.int32)
        block_exp, row_tok, dest = _routing_tables(eid, rank, counts, MOE_ROWS)
        y = _moe_experts(block_exp, row_tok, h2.reshape(b * s, d // 2), w_gate, w_up, w_down, l)
        x = _combine(dest, y, x.reshape(b * s, d), rg.reshape(b * s, LANES), g2, final_g, s,
                     final_norm=(l == depth - 1)).reshape(b, s, d)

    return x
```

```python
import functools

import jax
import jax.numpy as jnp
from jax import lax
from jax.experimental import pallas as pl
from jax.experimental.pallas import tpu as pltpu

EPS = 1e-6
LRU_C = 8.0
TOP_K = 2
LANES = 128
MOE_ROWS = 256
GATHER_DEPTH = 5
EXPERT_GATHER_DEPTH = 4
VMEM_LIMIT = 56 * 1024 * 1024

F32 = jnp.float32
BF16 = jnp.bfloat16


def _pick(n, pref):
    if n <= pref:
        return n
    for t in range(pref, 7, -1):
        if n % t == 0 and t % 8 == 0:
            return t
    raise ValueError((n, pref))


def _params(*sem):
    return pltpu.CompilerParams(dimension_semantics=sem, vmem_limit_bytes=VMEM_LIMIT)


def _sigmoid(x):
    return jax.nn.sigmoid(x)


def _ada_kernel(c_ref, w_ref, b_ref, o_ref):
    c = c_ref[...]
    cond = (c * _sigmoid(c)).astype(BF16)
    o_ref[...] = jnp.dot(cond, w_ref[...].astype(BF16), preferred_element_type=F32) + b_ref[...]


def _ada_mod(c_pad, w_ada, b_ada):
    depth, d, n = w_ada.shape
    rows = c_pad.shape[0]
    tn = _pick(n, 1024)
    return pl.pallas_call(
        _ada_kernel,
        grid=(depth, n // tn),
        in_specs=[
            pl.BlockSpec((rows, d), lambda l, j: (0, 0)),
            pl.BlockSpec((None, d, tn), lambda l, j: (l, 0, j)),
            pl.BlockSpec((None, 1, tn), lambda l, j: (l, 0, j)),
        ],
        out_specs=pl.BlockSpec((None, rows, tn), lambda l, j: (l, 0, j)),
        out_shape=jax.ShapeDtypeStruct((depth, rows, n), F32),
        compiler_params=_params("parallel", "parallel"),
        name="ada_mod",
    )(c_pad, w_ada, b_ada.reshape(depth, 1, n))


def _modulated_norm(x, g, sc, sh):
    y = x * lax.rsqrt(jnp.mean(x * x, axis=-1, keepdims=True) + EPS) * g
    return y * (1.0 + sc) + sh


def _inproj_kernel(x_ref, g_ref, sc_ref, sh_ref, w_ref, wf_ref, z_ref, zf_ref, h_scr):
    @pl.when(pl.program_id(2) == 0)
    def _():
        hb = _modulated_norm(x_ref[...], g_ref[...], sc_ref[...], sh_ref[...]).astype(BF16)
        h_scr[...] = hb
        zf_ref[...] = jnp.dot(hb, wf_ref[...], preferred_element_type=F32)

    z_ref[...] = jnp.dot(h_scr[...], w_ref[...], preferred_element_type=F32)


def _inproj(x, g, sc, sh, w_main, w_f):
    b, s, d = x.shape
    n = w_main.shape[1]
    tm = _pick(s, 1024)
    tn = _pick(n, 1280)
    return pl.pallas_call(
        _inproj_kernel,
        grid=(b, s // tm, n // tn),
        in_specs=[
            pl.BlockSpec((None, tm, d), lambda bi, i, j: (bi, i, 0)),
            pl.BlockSpec((1, d), lambda bi, i, j: (0, 0)),
            pl.BlockSpec((None, 1, d), lambda bi, i, j: (bi, 0, 0)),
            pl.BlockSpec((None, 1, d), lambda bi, i, j: (bi, 0, 0)),
            pl.BlockSpec((d, tn), lambda bi, i, j: (0, j)),
            pl.BlockSpec((d, LANES), lambda bi, i, j: (0, 0)),
        ],
        out_specs=[
            pl.BlockSpec((None, tm, tn), lambda bi, i, j: (bi, i, j)),
            pl.BlockSpec((None, tm, LANES), lambda bi, i, j: (bi, i, 0)),
        ],
        out_shape=[jax.ShapeDtypeStruct((b, s, n), F32), jax.ShapeDtypeStruct((b, s, LANES), F32)],
        scratch_shapes=[pltpu.VMEM((tm, d), BF16)],
        compiler_params=_params("parallel", "parallel", "arbitrary"),
        name="inproj",
    )(x, g, sc, sh, w_main, w_f)


def _conv_kernel(a_ref, g_ref, ah_ref, gh_ref, w_ref, b_ref, lg_ref, lb_ref, o_ref, buf, ybuf, shifted,
                 *, taps, ts, halo, rc):
    cw = a_ref.shape[-1]
    buf[halo:halo + ts, :] = a_ref[...] * _sigmoid(g_ref[...])
    prev = ah_ref[...] * _sigmoid(gh_ref[...])
    buf[0:halo, :] = jnp.where(pl.program_id(1) > 0, prev, 0.0)
    first = halo - (taps - 1)
    for c in range(cw // LANES):
        cs = slice(c * LANES, (c + 1) * LANES)
        for r in range(ts // rc):
            acc = b_ref[:, cs]
            for rem in range(8):
                group = [(k, first + k - rem) for k in range(taps) if (first + k) % 8 == rem]
                if not group:
                    continue
                lo = min(a for _, a in group)
                hi = max(a for _, a in group)
                shifted[0:hi - lo + rc, :] = buf[r * rc + rem + lo:r * rc + rem + hi + rc, cs]
                for k, a in group:
                    acc = acc + w_ref[k:k + 1, cs] * shifted[a - lo:a - lo + rc, :]
            ybuf[r * rc:(r + 1) * rc, cs] = acc
    y = ybuf[...]
    mu = jnp.mean(y, axis=-1, keepdims=True)
    yc = y - mu
    var = jnp.mean(yc * yc, axis=-1, keepdims=True)
    yn = yc * lax.rsqrt(var + EPS) * lg_ref[...] + lb_ref[...]
    o_ref[...] = (yn * _sigmoid(yn)).astype(o_ref.dtype)


def _conv_branch(z, cw, dw_w, dw_b, ln_g, ln_b):
    b, s, _ = z.shape
    taps = dw_w.shape[0]
    halo = 32
    assert taps - 1 <= halo
    ts = _pick(s, 256)
    rc = _pick(ts, 128)
    hb = ts // halo
    kern = functools.partial(_conv_kernel, taps=taps, ts=ts, halo=halo, rc=rc)
    cur = lambda col: pl.BlockSpec((None, ts, cw), lambda bi, i: (bi, i, col))
    prev = lambda col: pl.BlockSpec((None, halo, cw), lambda bi, i: (bi, jnp.maximum(i * hb - 1, 0), col))
    vec = pl.BlockSpec((1, cw), lambda bi, i: (0, 0))
    return pl.pallas_call(
        kern,
        grid=(b, s // ts),
        in_specs=[cur(0), cur(1), prev(0), prev(1),
                  pl.BlockSpec((taps, cw), lambda bi, i: (0, 0)), vec, vec, vec],
        out_specs=pl.BlockSpec((None, ts, cw), lambda bi, i: (bi, i, 0)),
        out_shape=jax.ShapeDtypeStruct((b, s, cw), BF16),
        scratch_shapes=[pltpu.VMEM((halo + ts, cw), F32), pltpu.VMEM((ts, cw), F32),
                        pltpu.VMEM((rc + halo, LANES), F32)],
        compiler_params=_params("parallel", "parallel"),
        name="conv_branch",
    )(z, z, z, z, dw_w, dw_b.reshape(1, cw), ln_g.reshape(1, cw), ln_b.reshape(1, cw))


def _shift_rows(v, d, fill):
    if d % 8 == 0:
        return jnp.concatenate([jnp.full((d, v.shape[1]), fill, v.dtype), v[:v.shape[0] - d]], axis=0)
    rows = lax.broadcasted_iota(jnp.int32, v.shape, 0)
    return jnp.where(rows >= d, pltpu.roll(v, d, axis=0), fill)


def _scan_linear(a, bv):
    d = 1
    while d < a.shape[0]:
        bv = a * _shift_rows(bv, d, 0.0) + bv
        a = a * _shift_rows(a, d, 1.0)
        d *= 2
    return a, bv


def _scan_sum(v):
    d = 1
    while d < v.shape[0]:
        v = v + _shift_rows(v, d, 0.0)
        d *= 2
    return v


def _cum_kernel(zf_ref, fb_ref, o_ref, car, *, ts, rs):
    @pl.when(pl.program_id(1) == 0)
    def _():
        car[...] = jnp.zeros_like(car)

    h = car[...]
    for r in range(ts // rs):
        x = zf_ref[r * rs:(r + 1) * rs, :] + fb_ref[...]
        log_f = jnp.minimum(x, 0.0) - jnp.log1p(jnp.exp(-jnp.abs(x)))
        v = _scan_sum(log_f) + h
        o_ref[r * rs:(r + 1) * rs, :] = v
        h = v[rs - 1:rs, :]
    car[...] = h


def _forget_cumsum(zf, f_bias_pad):
    b, s, _ = zf.shape
    ts = _pick(s, 512)
    rs = _pick(ts, 128)
    return pl.pallas_call(
        functools.partial(_cum_kernel, ts=ts, rs=rs),
        grid=(b, s // ts),
        in_specs=[pl.BlockSpec((None, ts, LANES), lambda bi, i: (bi, i, 0)),
                  pl.BlockSpec((1, LANES), lambda bi, i: (0, 0))],
        out_specs=pl.BlockSpec((None, ts, LANES), lambda bi, i: (bi, i, 0)),
        out_shape=jax.ShapeDtypeStruct((b, s, LANES), F32),
        scratch_shapes=[pltpu.VMEM((1, LANES), F32)],
        compiler_params=_params("parallel", "arbitrary"),
        name="forget_cumsum",
    )(zf, f_bias_pad)


def _gelu_tanh(x):
    return 0.5 * x * (1.0 + jnp.tanh(0.7978845608028654 * (x + 0.044715 * (x * x * x))))


def _lru_kernel(u_ref, uh_ref, gate_ref, cw_ref, cb_ref, wr_ref, br_ref, wi_ref, bi_ref, lam_ref, og_ref,
                o_ref, ubuf, abuf, bbuf, car, *, taps, ts, halo, rs):
    lw = u_ref.shape[-1]

    @pl.when(pl.program_id(1) == 0)
    def _():
        car[...] = jnp.zeros_like(car)

    ubuf[halo:halo + ts, :] = u_ref[...]
    ubuf[0:halo, :] = jnp.where(pl.program_id(1) > 0, uh_ref[...], 0.0)
    xc = cb_ref[...]
    for k in range(taps):
        off = halo - (taps - 1) + k
        xc = xc + cw_ref[k:k + 1, :] * ubuf[off:off + ts, :]
    xb = xc.astype(BF16)
    r = _sigmoid(jnp.dot(xb, wr_ref[...], preferred_element_type=F32) + br_ref[...])
    ig = _sigmoid(jnp.dot(xb, wi_ref[...], preferred_element_type=F32) + bi_ref[...])
    nl = -lam_ref[...]
    softplus = jnp.maximum(nl, 0.0) + jnp.log1p(jnp.exp(-jnp.abs(nl)))
    log_a = (-LRU_C) * r * softplus
    th = jnp.tanh(log_a)
    abuf[...] = jnp.exp(log_a)
    bbuf[...] = jnp.sqrt(-2.0 * th / (1.0 - th)) * (ig * xc)
    for c in range(lw // LANES):
        cs = slice(c * LANES, (c + 1) * LANES)
        h = car[:, cs]
        for q in range(ts // rs):
            rows = slice(q * rs, (q + 1) * rs)
            a_cum, b_cum = _scan_linear(abuf[rows, cs], bbuf[rows, cs])
            hv = b_cum + a_cum * h
            bbuf[rows, cs] = hv
            h = hv[rs - 1:rs, :]
        car[:, cs] = h
    y = bbuf[...] * _gelu_tanh(gate_ref[...])
    y = y * lax.rsqrt(jnp.mean(y * y, axis=-1, keepdims=True) + EPS) * og_ref[...]
    o_ref[...] = y.astype(o_ref.dtype)


def _lru_branch(z, lw, col_x, conv_w, conv_b, wr_bd, b_r, wi_bd, b_i, lam, out_g):
    b, s, _ = z.shape
    taps = conv_w.shape[0]
    halo = 8
    assert taps - 1 <= halo
    ts = _pick(s, 256)
    rs = _pick(ts, 128)
    hb = ts // halo
    kern = functools.partial(_lru_kernel, taps=taps, ts=ts, halo=halo, rs=rs)
    vec = pl.BlockSpec((1, lw), lambda bi, i: (0, 0))
    mat = pl.BlockSpec((lw, lw), lambda bi, i: (0, 0))
    return pl.pallas_call(
        kern,
        grid=(b, s // ts),
        in_specs=[
            pl.BlockSpec((None, ts, lw), lambda bi, i: (bi, i, col_x)),
            pl.BlockSpec((None, halo, lw), lambda bi, i: (bi, jnp.maximum(i * hb - 1, 0), col_x)),
            pl.BlockSpec((None, ts, lw), lambda bi, i: (bi, i, col_x + 1)),
            pl.BlockSpec((taps, lw), lambda bi, i: (0, 0)), vec, mat, vec, mat, vec, vec, vec,
        ],
        out_specs=pl.BlockSpec((None, ts, lw), lambda bi, i: (bi, i, 0)),
        out_shape=jax.ShapeDtypeStruct((b, s, lw), BF16),
        scratch_shapes=[pltpu.VMEM((halo + ts, lw), F32), pltpu.VMEM((ts, lw), F32),
                        pltpu.VMEM((ts, lw), F32), pltpu.VMEM((1, lw), F32)],
        compiler_params=_params("parallel", "arbitrary"),
        name="lru_branch",
    )(z, z, z, conv_w, conv_b.reshape(1, lw), wr_bd, b_r.reshape(1, lw), wi_bd, b_i.reshape(1, lw),
      lam.reshape(1, lw), out_g.reshape(1, lw))


def _split3(t):
    hi = t.astype(BF16).astype(F32)
    r = t - hi
    mid = r.astype(BF16).astype(F32)
    return hi, mid, r - mid


def _attn_kernel(q_ref, k_ref, v_ref, cum_ref, o_ref, kaug, vaug, qaug, s_a, s_b, p_a, p_b, al_a, al_b, m_ref, acc_ref,
                 *, tq, tk, scale):
    h = pl.program_id(1)
    i = pl.program_id(2)
    s_len, hd = k_ref.shape
    inv_scale = 1.0 / scale
    c_exp = scale * 1.4426950408889634
    s_buf, p_buf, al_buf = (s_a, s_b), (p_a, p_b), (al_a, al_b)

    def head_cum(start, size):
        lane = lax.broadcasted_iota(jnp.int32, (size, LANES), 1)
        rows = pl.ds(pl.multiple_of(start, size), size)
        return jnp.sum(jnp.where(lane == h, cum_ref[rows, :], 0.0), axis=-1, keepdims=True) * inv_scale

    def key_rows(j):
        return pl.ds(pl.multiple_of(j * tk, tk), tk)

    @pl.when(i == 0)
    def _():
        lane = lax.broadcasted_iota(jnp.int32, (tk, LANES), 1)

        def body(t, carry):
            rows = key_rows(t)
            hi, mid, lo = _split3(-head_cum(t * tk, tk))
            aug = jnp.where(lane < 3, 1.0, jnp.where(lane == 3, hi, jnp.where(lane == 4, mid,
                            jnp.where(lane == 5, lo, 0.0))))
            kaug[t, 0:hd, :] = k_ref[rows, :].T.astype(BF16)
            kaug[t, hd:2 * hd, :] = aug.T.astype(BF16)
            vaug[rows, 0:hd] = v_ref[rows, :].astype(BF16)
            vaug[rows, hd:2 * hd] = jnp.where(lane == 0, 1.0, 0.0).astype(BF16)
            return carry

        lax.fori_loop(0, s_len // tk, body, 0)

    lane_q = lax.broadcasted_iota(jnp.int32, (tq, LANES), 1)
    hi, mid, lo = _split3(head_cum(i * tq, tq))
    qa = jnp.where(lane_q == 0, hi, jnp.where(lane_q == 1, mid, jnp.where(lane_q == 2, lo,
                   jnp.where(lane_q < 6, 1.0, 0.0))))
    qaug[:, 0:hd] = q_ref[...].astype(BF16)
    qaug[:, hd:2 * hd] = qa.astype(BF16)

    def scores_into(dst, j, masked=False):
        s = jnp.dot(qaug[...], kaug[j], preferred_element_type=F32)
        if masked:
            row = lax.broadcasted_iota(jnp.int32, s.shape, 0)
            col = lax.broadcasted_iota(jnp.int32, s.shape, 1)
            s = jnp.where(col <= row, s, -jnp.inf)
        dst[...] = s

    def softmax_into(src, dst_p, dst_alpha):
        s = src[...]
        m_old = m_ref[...]
        m_new = jnp.maximum(m_old, jnp.max(s, axis=-1, keepdims=True))
        dst_alpha[...] = jnp.exp2((m_old - m_new) * c_exp)
        m_ref[...] = m_new
        dst_p[...] = jnp.exp2((s - jnp.concatenate([m_new] * (tk // LANES), axis=1)) * c_exp).astype(BF16)

    def accumulate(src_alpha, src_p, j):
        alpha = jnp.concatenate([src_alpha[...]] * (acc_ref.shape[1] // LANES), axis=1)
        acc_ref[...] = alpha * acc_ref[...] + jnp.dot(src_p[...], vaug[key_rows(j), :], preferred_element_type=F32)

    def blk(t):
        return jnp.where(t == 0, i, jnp.maximum(t - 1, 0))

    def tick(t, par):
        scores_into(s_buf[1 - par], t)
        softmax_into(s_buf[par], p_buf[par], al_buf[par])
        accumulate(al_buf[1 - par], p_buf[1 - par], blk(t - 1))

    scores_into(s_buf[0], i, masked=True)
    m_ref[...] = jnp.full(m_ref.shape, -jnp.inf, F32)
    acc_ref[...] = jnp.zeros(acc_ref.shape, F32)
    p_buf[1][...] = jnp.zeros(p_buf[1].shape, BF16)
    al_buf[1][...] = jnp.ones(al_buf[1].shape, F32)

    def body(u, carry):
        tick(2 * u, 0)
        tick(2 * u + 1, 1)
        return carry

    lax.fori_loop(0, i // 2, body, 0)
    odd = i % 2 == 1

    @pl.when(odd)
    def _():
        tick(i - 1, 0)

    def drain(par):
        softmax_into(s_buf[par], p_buf[par], al_buf[par])
        accumulate(al_buf[1 - par], p_buf[1 - par], blk(i - 1))
        accumulate(al_buf[par], p_buf[par], blk(i))

    @pl.when(odd)
    def _():
        drain(1)

    @pl.when(jnp.logical_not(odd))
    def _():
        drain(0)

    o_ref[...] = acc_ref[:, 0:hd] / acc_ref[:, hd:hd + 1]


def _attention(z, cum, heads, hd, col_q, col_k, col_v):
    b, s, _ = z.shape
    tq = _pick(s, 512)
    tk = tq
    kern = functools.partial(_attn_kernel, tq=tq, tk=tk, scale=hd ** -0.5)
    return pl.pallas_call(
        kern,
        grid=(b, heads, s // tq),
        in_specs=[
            pl.BlockSpec((None, tq, hd), lambda bi, h, i: (bi, i, col_q + h)),
            pl.BlockSpec((None, s, hd), lambda bi, h, i: (bi, 0, col_k + h)),
            pl.BlockSpec((None, s, hd), lambda bi, h, i: (bi, 0, col_v + h)),
            pl.BlockSpec((None, s, LANES), lambda bi, h, i: (bi, 0, 0)),
        ],
        out_specs=pl.BlockSpec((None, tq, hd), lambda bi, h, i: (bi, i, h)),
        out_shape=jax.ShapeDtypeStruct((b, s, heads * hd), F32),
        scratch_shapes=[pltpu.VMEM((s // tk, 2 * hd, tk), BF16), pltpu.VMEM((s, 2 * hd), BF16),
                        pltpu.VMEM((tq, 2 * hd), BF16),
                        pltpu.VMEM((tq, tk), F32), pltpu.VMEM((tq, tk), F32),
                        pltpu.VMEM((tq, tk), BF16), pltpu.VMEM((tq, tk), BF16),
                        pltpu.VMEM((tq, LANES), F32), pltpu.VMEM((tq, LANES), F32), pltpu.VMEM((tq, LANES), F32),
                        pltpu.VMEM((tq, 2 * hd), F32)],
        compiler_params=_params("arbitrary", "arbitrary", "arbitrary"),
        name="fox_attention",
    )(z, z, z, cum)


def _outproj_kernel(yc_ref, o_ref, yl_ref, fg_ref, w1_ref, w2_ref, w3_ref, x_ref, g1_ref, out_ref, yf_scr):
    @pl.when(pl.program_id(2) == 0)
    def _():
        o = o_ref[...]
        yf = o * lax.rsqrt(jnp.mean(o * o, axis=-1, keepdims=True) + EPS) * fg_ref[...]
        yf_scr[...] = yf.astype(BF16)

    y = jnp.dot(yc_ref[...], w1_ref[...], preferred_element_type=F32)
    y = y + jnp.dot(yf_scr[...], w2_ref[...], preferred_element_type=F32)
    y = y + jnp.dot(yl_ref[...], w3_ref[...], preferred_element_type=F32)
    out_ref[...] = x_ref[...] + g1_ref[...] * y


def _outproj(y_conv, o_fox, y_lru, fox_g, w1, w2, w3, x, g1):
    b, s, d = x.shape
    cw, fw, lw = y_conv.shape[-1], o_fox.shape[-1], y_lru.shape[-1]
    tm = _pick(s, 512)
    tn = _pick(d, 1024)
    act = lambda w: pl.BlockSpec((None, tm, w), lambda bi, i, j: (bi, i, 0))
    wgt = lambda w: pl.BlockSpec((w, tn), lambda bi, i, j: (0, j))
    return pl.pallas_call(
        _outproj_kernel,
        grid=(b, s // tm, d // tn),
        in_specs=[act(cw), act(fw), act(lw), pl.BlockSpec((1, fw), lambda bi, i, j: (0, 0)),
                  wgt(cw), wgt(fw), wgt(lw),
                  pl.BlockSpec((None, tm, tn), lambda bi, i, j: (bi, i, j)),
                  pl.BlockSpec((None, 1, tn), lambda bi, i, j: (bi, 0, j))],
        out_specs=pl.BlockSpec((None, tm, tn), lambda bi, i, j: (bi, i, j)),
        out_shape=jax.ShapeDtypeStruct((b, s, d), F32),
        scratch_shapes=[pltpu.VMEM((tm, fw), BF16)],
        compiler_params=_params("parallel", "parallel", "arbitrary"),
        name="outproj",
    )(y_conv, o_fox, y_lru, fox_g.reshape(1, fw), w1, w2, w3, x, g1)


def _bf16_bits(v):
    return lax.bitcast_convert_type(v.astype(BF16).astype(F32), jnp.uint32)


def _pack_halves(v):
    half = v.shape[1] // 2
    word = _bf16_bits(v[:, half:]) | (_bf16_bits(v[:, :half]) >> 16)
    return lax.bitcast_convert_type(word, jnp.int32)


def _unpack_halves(w):
    word = lax.bitcast_convert_type(w, jnp.uint32)
    lo = lax.bitcast_convert_type(word << 16, F32)
    hi = lax.bitcast_convert_type(word & jnp.uint32(0xFFFF0000), F32)
    return lo, hi


def _router_kernel(x_ref, g_ref, sc_ref, sh_ref, whi_ref, wlo_ref, rb_ref, h2_ref, ri_ref, rg_ref, cnt_ref, car,
                   *, n_groups, per_group):
    @pl.when(jnp.logical_and(pl.program_id(0) == 0, pl.program_id(1) == 0))
    def _():
        car[...] = jnp.zeros_like(car)

    h2 = _modulated_norm(x_ref[...], g_ref[...], sc_ref[...], sh_ref[...])
    h2_ref[...] = _pack_halves(h2)
    hi = h2.astype(BF16)
    lo = (h2 - hi.astype(F32)).astype(BF16)
    logits = (jnp.dot(hi, whi_ref[...], preferred_element_type=F32)
              + jnp.dot(lo, whi_ref[...], preferred_element_type=F32)
              + jnp.dot(hi, wlo_ref[...], preferred_element_type=F32)) + rb_ref[...]
    lane_i = lax.broadcasted_iota(jnp.int32, logits.shape, 1)
    lane = lane_i.astype(F32)
    big = float(LANES)

    def first_lane(hit):
        return jnp.min(jnp.where(hit, lane, big), axis=-1, keepdims=True)

    gmask = lane_i < n_groups
    lg1 = jnp.where(gmask, logits, -jnp.inf)
    m1 = jnp.max(lg1, axis=-1, keepdims=True)
    e1 = jnp.exp(lg1 - m1)
    p1 = e1 / jnp.sum(e1, axis=-1, keepdims=True)
    p_grp = jnp.max(p1, axis=-1, keepdims=True)
    grp = first_lane((p1 == p_grp) & gmask)
    lo_lane = n_groups + grp * per_group
    emask = (lane >= lo_lane) & (lane < lo_lane + per_group)
    lg2 = jnp.where(emask, logits, -jnp.inf)
    m2 = jnp.max(lg2, axis=-1, keepdims=True)
    e2 = jnp.exp(lg2 - m2)
    p2 = e2 / jnp.sum(e2, axis=-1, keepdims=True)
    v1 = jnp.max(jnp.where(emask, p2, -1.0), axis=-1, keepdims=True)
    i1 = first_lane((p2 == v1) & emask)
    rest = emask & (lane != i1)
    v2 = jnp.max(jnp.where(rest, p2, -1.0), axis=-1, keepdims=True)
    i2 = first_lane((p2 == v2) & rest)
    denom = v1 + v2
    gate1 = p_grp * (v1 / denom)
    gate2 = p_grp * (v2 / denom)
    tm = logits.shape[0]
    picked = jnp.where((lane == i1) | (lane == i2), 1.0, 0.0)
    tri = jnp.where(lax.broadcasted_iota(jnp.int32, (tm, tm), 0) > lax.broadcasted_iota(jnp.int32, (tm, tm), 1),
                    1.0, 0.0).astype(BF16)
    before = jnp.dot(tri, picked.astype(BF16), preferred_element_type=F32) + car[...]
    rank1 = jnp.sum(jnp.where(lane == i1, before, 0.0), axis=-1, keepdims=True)
    rank2 = jnp.sum(jnp.where(lane == i2, before, 0.0), axis=-1, keepdims=True)
    total = car[...] + jnp.sum(picked, axis=0, keepdims=True)
    car[...] = total
    cnt_ref[...] = total
    info = jnp.where(lane_i == 0, i1 - n_groups, jnp.where(lane_i == 1, i2 - n_groups,
                     jnp.where(lane_i == 2, rank1, jnp.where(lane_i == 3, rank2, 0.0))))
    ri_ref[...] = info.astype(jnp.int32)
    rg_ref[...] = jnp.where(lane_i == 0, gate1, jnp.where(lane_i == 1, gate2, 0.0))


def _router(x, g, sc, sh, w_hi, w_lo, r_bias, n_groups, per_group):
    b, s, d = x.shape
    tm = _pick(s, 512)
    kern = functools.partial(_router_kernel, n_groups=n_groups, per_group=per_group)
    row = lambda w: pl.BlockSpec((None, tm, w), lambda bi, i: (bi, i, 0))
    mod = pl.BlockSpec((None, 1, d), lambda bi, i: (bi, 0, 0))
    wsp = pl.BlockSpec((d, LANES), lambda bi, i: (0, 0))
    return pl.pallas_call(
        kern,
        grid=(b, s // tm),
        in_specs=[row(d), pl.BlockSpec((1, d), lambda bi, i: (0, 0)), mod, mod, wsp, wsp,
                  pl.BlockSpec((1, LANES), lambda bi, i: (0, 0))],
        out_specs=[row(d // 2), row(LANES), row(LANES), pl.BlockSpec((1, LANES), lambda bi, i: (0, 0))],
        out_shape=[jax.ShapeDtypeStruct((b, s, d // 2), jnp.int32), jax.ShapeDtypeStruct((b, s, LANES), jnp.int32),
                   jax.ShapeDtypeStruct((b, s, LANES), F32), jax.ShapeDtypeStruct((1, LANES), F32)],
        scratch_shapes=[pltpu.VMEM((1, LANES), F32)],
        compiler_params=_params("arbitrary", "arbitrary"),
        name="router",
    )(x, g, sc, sh, w_hi, w_lo, r_bias)


def _moe_kernel(bexp_ref, rtok_ref, wslot_ref, nexte_ref, h2_hbm, wg_hbm, wu_hbm, wd_hbm, y_ref,
                *scratch, rows, layer):
    xbufs = scratch[:EXPERT_GATHER_DEPTH]
    sem, wg_bf, wu_bf, wd_bf, wg_raw, wu_raw, wd_raw, wsem = scratch[EXPERT_GATHER_DEPTH:]
    b = pl.program_id(0)
    nb = pl.num_programs(0)
    depth = EXPERT_GATHER_DEPTH

    def row_copy(tok, r, slot):
        return pltpu.make_async_copy(h2_hbm.at[pl.ds(tok, 1)], xbufs[slot].at[pl.ds(r, 1)], sem.at[slot])

    def wait_block(slot):
        pltpu.make_async_copy(h2_hbm.at[pl.ds(0, rows)], xbufs[slot], sem.at[slot]).wait()

    @pl.when(b == 0)
    def _():
        for ahead in range(depth - 1):
            def body(r, carry, ahead=ahead):
                row_copy(rtok_ref[ahead * rows + r], r, ahead).start()
                return carry

            lax.fori_loop(0, rows, body, 0)

    def weight_copies(e, s):
        return [pltpu.make_async_copy(src.at[layer, e], dst.at[s], wsem.at[s])
                for src, dst in ((wg_hbm, wg_raw), (wu_hbm, wu_raw), (wd_hbm, wd_raw))]

    @pl.when(b == 0)
    def _():
        for copy in weight_copies(bexp_ref[0], 0):
            copy.start()

    changed = jnp.logical_or(b == 0, bexp_ref[b] != bexp_ref[jnp.maximum(b - 1, 0)])

    @pl.when(changed)
    def _():
        s = wslot_ref[b]
        for copy in weight_copies(bexp_ref[b], s):
            copy.wait()
        for static_s in range(2):
            @pl.when(s == static_s)
            def _():
                wg_bf[...] = wg_raw[static_s].astype(BF16)
                wu_bf[...] = wu_raw[static_s].astype(BF16)
                wd_bf[...] = wd_raw[static_s].astype(BF16)

        nxt = nexte_ref[b]

        @pl.when(nxt >= 0)
        def _():
            for copy in weight_copies(nxt, 1 - s):
                copy.start()

    def block(slot):
        wait_block(slot)
        ahead_slot = (slot + depth - 1) % depth
        base = jnp.minimum(b + depth - 1, nb - 1) * rows
        for r in range(rows):
            row_copy(rtok_ref[base + r], r, ahead_slot).start(priority=r % 2)
        lo, hi = _unpack_halves(xbufs[slot][...])
        xb = jnp.concatenate([lo.astype(BF16), hi.astype(BF16)], axis=1)
        gt = jnp.dot(xb, wg_bf[...], preferred_element_type=F32)
        up = jnp.dot(xb, wu_bf[...], preferred_element_type=F32)
        hb = (gt * _sigmoid(gt) * up).astype(BF16)
        y_ref[...] = _pack_halves(jnp.dot(hb, wd_bf[...], preferred_element_type=F32))

        @pl.when(b == nb - 1)
        def _():
            for other in range(1, depth):
                wait_block((slot + other) % depth)

    for static_slot in range(depth):
        @pl.when(b % depth == static_slot)
        def _(static_slot=static_slot):
            block(static_slot)


def _moe_experts(block_exp, row_tok, h2_rows, w_gate, w_up, w_down, layer):
    _, _, d, f = w_gate.shape
    rows = MOE_ROWS
    n_blocks = block_exp.shape[0]
    changed = jnp.concatenate([jnp.ones((1,), jnp.int32), (block_exp[1:] != block_exp[:-1]).astype(jnp.int32)])
    w_slot = (jnp.cumsum(changed) - 1) % 2
    n_le = jnp.sum((block_exp[None, :] <= block_exp[:, None]).astype(jnp.int32), axis=1)
    next_exp = jnp.where(n_le < n_blocks, block_exp[jnp.minimum(n_le, n_blocks - 1)], -1)
    any_spec = pl.BlockSpec(memory_space=pl.ANY)
    grid_spec = pltpu.PrefetchScalarGridSpec(
        num_scalar_prefetch=4,
        grid=(n_blocks,),
        in_specs=[any_spec, any_spec, any_spec, any_spec],
        out_specs=pl.BlockSpec((rows, d // 2), lambda bi, *_: (bi, 0)),
        scratch_shapes=[pltpu.VMEM((rows, d // 2), jnp.int32)] * EXPERT_GATHER_DEPTH + [
                        pltpu.SemaphoreType.DMA((EXPERT_GATHER_DEPTH,)),
                        pltpu.VMEM((d, f), BF16), pltpu.VMEM((d, f), BF16), pltpu.VMEM((f, d), BF16),
                        pltpu.VMEM((2, d, f), F32), pltpu.VMEM((2, d, f), F32), pltpu.VMEM((2, f, d), F32),
                        pltpu.SemaphoreType.DMA((2,))],
    )
    return pl.pallas_call(
        functools.partial(_moe_kernel, rows=rows, layer=layer),
        grid_spec=grid_spec,
        out_shape=jax.ShapeDtypeStruct((n_blocks * rows, d // 2), jnp.int32),
        compiler_params=_params("arbitrary"),
        name="moe_experts",
    )(block_exp, row_tok, w_slot.astype(jnp.int32), next_exp.astype(jnp.int32), h2_rows, w_gate, w_up, w_down)


def _combine_kernel(dest_ref, y_hbm, x_ref, rg_ref, g2_ref, fg_ref, out_ref, *scratch, tm, final_norm):
    ybufs, sem = scratch[:GATHER_DEPTH], scratch[GATHER_DEPTH]
    i = pl.program_id(0)
    n = pl.num_programs(0)
    depth = GATHER_DEPTH

    def start(row, r, k, slot):
        copy = pltpu.make_async_copy(y_hbm.at[pl.ds(row, 1)], ybufs[slot].at[k, pl.ds(r, 1)], sem.at[slot])
        copy.start(priority=k % 2)

    def wait_tile(slot):
        for k in range(TOP_K):
            pltpu.make_async_copy(y_hbm.at[pl.ds(0, tm)], ybufs[slot].at[k], sem.at[slot]).wait()

    @pl.when(i == 0)
    def _():
        for ahead in range(depth - 1):
            def body(r, carry, ahead=ahead):
                for k in range(TOP_K):
                    start(dest_ref[(ahead * tm + r) * TOP_K + k], r, k, ahead)
                return carry

            lax.fori_loop(0, tm, body, 0)

    def tile(slot):
        wait_tile(slot)
        ahead_slot = (slot + depth - 1) % depth
        base = jnp.minimum(i + depth - 1, n - 1) * (tm * TOP_K)
        for r in range(tm):
            for k in range(TOP_K):
                start(dest_ref[base + r * TOP_K + k], r, k, ahead_slot)
        gates = rg_ref[...]
        gate = [jnp.broadcast_to(gates[:, k:k + 1], (tm, LANES)) for k in range(TOP_K)]
        half = out_ref.shape[1] // 2
        for j in range(half // LANES):
            words = slice(j * LANES, (j + 1) * LANES)
            moe_lo = moe_hi = None
            for k in range(TOP_K):
                lo, hi = _unpack_halves(ybufs[slot][k, :, words])
                moe_lo = lo * gate[k] if moe_lo is None else moe_lo + lo * gate[k]
                moe_hi = hi * gate[k] if moe_hi is None else moe_hi + hi * gate[k]
            for cols, moe in ((words, moe_lo), (slice(half + j * LANES, half + (j + 1) * LANES), moe_hi)):
                out_ref[:, cols] = x_ref[:, cols] + g2_ref[:, cols] * moe
        if final_norm:
            v = out_ref[...]
            out_ref[...] = v * lax.rsqrt(jnp.mean(v * v, axis=-1, keepdims=True) + EPS) * fg_ref[...]

        @pl.when(i == n - 1)
        def _():
            for other in range(1, depth):
                wait_tile((slot + other) % depth)

    for static_slot in range(depth):
        @pl.when(i % depth == static_slot)
        def _(static_slot=static_slot):
            tile(static_slot)


def _combine(dest, y, x_flat, rg_flat, g2, final_g, seq, final_norm):
    t, d = x_flat.shape
    tm = _pick(seq, 128)
    grid_spec = pltpu.PrefetchScalarGridSpec(
        num_scalar_prefetch=1,
        grid=(t // tm,),
        in_specs=[
            pl.BlockSpec(memory_space=pl.ANY),
            pl.BlockSpec((tm, d), lambda i, de: (i, 0)),
            pl.BlockSpec((tm, LANES), lambda i, de: (i, 0)),
            pl.BlockSpec((None, 1, d), lambda i, de: ((i * tm) // seq, 0, 0)),
            pl.BlockSpec((1, d), lambda i, de: (0, 0)),
        ],
        out_specs=pl.BlockSpec((tm, d), lambda i, de: (i, 0)),
        scratch_shapes=[pltpu.VMEM((TOP_K, tm, d // 2), jnp.int32)] * GATHER_DEPTH + [
                        pltpu.SemaphoreType.DMA((GATHER_DEPTH,))],
    )
    return pl.pallas_call(
        functools.partial(_combine_kernel, tm=tm, final_norm=final_norm),
        grid_spec=grid_spec,
        out_shape=jax.ShapeDtypeStruct((t, d), F32),
        compiler_params=_params("arbitrary"),
        name="moe_combine",
    )(dest, y, x_flat, rg_flat, g2, final_g.reshape(1, d))


def _routing_tables(eid, rank, counts, rows):
    n_assign = eid.shape[0]
    n_experts = counts.shape[0]
    padded = (counts + rows - 1) // rows * rows
    pends = jnp.cumsum(padded)
    pstarts = pends - padded
    onehot = eid[:, None] == jnp.arange(n_experts, dtype=jnp.int32)[None, :]
    dest = (jnp.sum(jnp.where(onehot, pstarts[None, :], 0), axis=1) + rank).astype(jnp.int32)
    n_blocks = -(-n_assign // rows) + n_experts
    row_tok = jnp.zeros((n_blocks * rows,), jnp.int32).at[dest].set(jnp.arange(n_assign, dtype=jnp.int32) // TOP_K)
    first_row = jnp.arange(n_blocks, dtype=jnp.int32) * rows
    block_exp = jnp.minimum(jnp.sum((pends[None, :] <= first_row[:, None]).astype(jnp.int32), axis=1), n_experts - 1)
    return block_exp.astype(jnp.int32), row_tok, dest


def _block_diag(w):
    nb, bw, _ = w.shape
    eye = jnp.eye(nb, dtype=w.dtype)
    return (eye[:, None, :, None] * w[:, :, None, :]).reshape(nb * bw, nb * bw)


def kernel(x, c, w_ada, b_ada, ln1_g, w_in, conv_dw_w, conv_dw_b, conv_ln_g, conv_ln_b, fox_f_bias, fox_out_g, lru_conv_w, lru_conv_b, lru_w_r, lru_b_r, lru_w_i, lru_b_i, lru_lambda, lru_out_g, w_out, ln2_g, w_router_group, b_router_group, w_router_expert, b_router_expert, w_gate, w_up, w_down, final_g):
    b, s, d = x.shape
    depth = w_ada.shape[0]
    cw = conv_dw_b.shape[-1]
    fw = fox_out_g.shape[-1]
    lw = lru_lambda.shape[-1]
    heads = fox_f_bias.shape[-1]
    hd = fw // heads
    n_groups, per_group = b_router_expert.shape[1:]
    n_experts = n_groups * per_group
    off_f = 2 * cw + 3 * fw
    assert cw % LANES == 0 and lw == cw and fw % cw == 0 and hd == LANES and heads <= LANES
    assert n_groups + n_experts <= LANES

    c_pad = jnp.zeros((8, d), F32).at[:b].set(c)
    mod_all = _ada_mod(c_pad, w_ada, b_ada)

    for l in range(depth):
        mod = mod_all[l, :b].reshape(b, 1, 6 * d)
        sh1, sc1, g1, sh2, sc2, g2 = [mod[..., k * d:(k + 1) * d] for k in range(6)]

        w_l = w_in[l]
        w_main = jnp.concatenate([w_l[:, :off_f], w_l[:, off_f + heads:]], axis=1).astype(BF16)
        w_f = jnp.zeros((d, LANES), BF16).at[:, :heads].set(w_l[:, off_f:off_f + heads].astype(BF16))
        z, zf = _inproj(x, ln1_g[l].reshape(1, d), sc1, sh1, w_main, w_f)

        y_conv = _conv_branch(z, cw, conv_dw_w[l], conv_dw_b[l], conv_ln_g[l], conv_ln_b[l])

        f_bias = jnp.zeros((1, LANES), F32).at[0, :heads].set(fox_f_bias[l])
        cum = _forget_cumsum(zf, f_bias)
        o_fox = _attention(z, cum, heads, hd, 2 * cw // hd, (2 * cw + fw) // hd, (2 * cw + 2 * fw) // hd)

        y_lru = _lru_branch(z, lw, off_f // lw, lru_conv_w[l], lru_conv_b[l],
                            _block_diag(lru_w_r[l]).astype(BF16), lru_b_r[l],
                            _block_diag(lru_w_i[l]).astype(BF16), lru_b_i[l], lru_lambda[l], lru_out_g[l])

        wo = w_out[l].astype(BF16)
        x = _outproj(y_conv, o_fox, y_lru, fox_out_g[l], wo[:cw], wo[cw:cw + fw], wo[cw + fw:], x, g1)

        w_r = jnp.concatenate(
            [w_router_group[l], w_router_expert[l].transpose(1, 0, 2).reshape(d, n_experts)], axis=1)
        w_r = jnp.zeros((d, LANES), F32).at[:, :n_groups + n_experts].set(w_r)
        w_hi = w_r.astype(BF16)
        w_lo = (w_r - w_hi.astype(F32)).astype(BF16)
        r_bias = jnp.zeros((1, LANES), F32).at[0, :n_groups + n_experts].set(
            jnp.concatenate([b_router_group[l], b_router_expert[l].reshape(-1)]))
        h2, ri, rg, cnt = _router(x, ln2_g[l].reshape(1, d), sc2, sh2, w_hi, w_lo, r_bias, n_groups, per_group)

        eid = ri[:, :, :TOP_K].reshape(-1)
        rank = ri[:, :, TOP_K:2 * TOP_K].reshape(-1)
        counts = cnt[0, n_groups:n_groups + n_experts].astype(jnp.int32)
        block_exp, row_tok, dest = _routing_tables(eid, rank, counts, MOE_ROWS)
        y = _moe_experts(block_exp, row_tok, h2.reshape(b * s, d // 2), w_gate, w_up, w_down, l)
        x = _combine(dest, y, x.reshape(b * s, d), rg.reshape(b * s, LANES), g2, final_g, s,
                     final_norm=(l == depth - 1)).reshape(b, s, d)

    return x
```

```python
import functools

import jax
import jax.numpy as jnp
from jax import lax
from jax.experimental import pallas as pl
from jax.experimental.pallas import tpu as pltpu

EPS = 1e-6
LRU_C = 8.0
TOP_K = 2
LANES = 128
MOE_ROWS = 128
GATHER_DEPTH = 5
EXPERT_GATHER_DEPTH = 8
VMEM_LIMIT = 56 * 1024 * 1024

F32 = jnp.float32
BF16 = jnp.bfloat16


def _pick(n, pref):
    if n <= pref:
        return n
    for t in range(pref, 7, -1):
        if n % t == 0 and t % 8 == 0:
            return t
    raise ValueError((n, pref))


def _params(*sem):
    return pltpu.CompilerParams(dimension_semantics=sem, vmem_limit_bytes=VMEM_LIMIT)


def _sigmoid(x):
    return jax.nn.sigmoid(x)


def _ada_kernel(c_ref, w_ref, b_ref, o_ref):
    c = c_ref[...]
    cond = (c * _sigmoid(c)).astype(BF16)
    o_ref[...] = jnp.dot(cond, w_ref[...].astype(BF16), preferred_element_type=F32) + b_ref[...]


def _ada_mod(c_pad, w_ada, b_ada):
    depth, d, n = w_ada.shape
    rows = c_pad.shape[0]
    tn = _pick(n, 1024)
    return pl.pallas_call(
        _ada_kernel,
        grid=(depth, n // tn),
        in_specs=[
            pl.BlockSpec((rows, d), lambda l, j: (0, 0)),
            pl.BlockSpec((None, d, tn), lambda l, j: (l, 0, j)),
            pl.BlockSpec((None, 1, tn), lambda l, j: (l, 0, j)),
        ],
        out_specs=pl.BlockSpec((None, rows, tn), lambda l, j: (l, 0, j)),
        out_shape=jax.ShapeDtypeStruct((depth, rows, n), F32),
        compiler_params=_params("parallel", "parallel"),
        name="ada_mod",
    )(c_pad, w_ada, b_ada.reshape(depth, 1, n))


def _modulated_norm(x, g, sc, sh):
    y = x * lax.rsqrt(jnp.mean(x * x, axis=-1, keepdims=True) + EPS) * g
    return y * (1.0 + sc) + sh


def _inproj_kernel(x_ref, g_ref, sc_ref, sh_ref, w_ref, wf_ref, z_ref, zf_ref, h_scr):
    @pl.when(pl.program_id(2) == 0)
    def _():
        hb = _modulated_norm(x_ref[...], g_ref[...], sc_ref[...], sh_ref[...]).astype(BF16)
        h_scr[...] = hb
        zf_ref[...] = jnp.dot(hb, wf_ref[...], preferred_element_type=F32)

    z_ref[...] = jnp.dot(h_scr[...], w_ref[...], preferred_element_type=F32)


def _inproj(x, g, sc, sh, w_main, w_f):
    b, s, d = x.shape
    n = w_main.shape[1]
    tm = _pick(s, 1024)
    tn = _pick(n, 1280)
    return pl.pallas_call(
        _inproj_kernel,
        grid=(b, s // tm, n // tn),
        in_specs=[
            pl.BlockSpec((None, tm, d), lambda bi, i, j: (bi, i, 0)),
            pl.BlockSpec((1, d), lambda bi, i, j: (0, 0)),
            pl.BlockSpec((None, 1, d), lambda bi, i, j: (bi, 0, 0)),
            pl.BlockSpec((None, 1, d), lambda bi, i, j: (bi, 0, 0)),
            pl.BlockSpec((d, tn), lambda bi, i, j: (0, j)),
            pl.BlockSpec((d, LANES), lambda bi, i, j: (0, 0)),
        ],
        out_specs=[
            pl.BlockSpec((None, tm, tn), lambda bi, i, j: (bi, i, j)),
            pl.BlockSpec((None, tm, LANES), lambda bi, i, j: (bi, i, 0)),
        ],
        out_shape=[jax.ShapeDtypeStruct((b, s, n), F32), jax.ShapeDtypeStruct((b, s, LANES), F32)],
        scratch_shapes=[pltpu.VMEM((tm, d), BF16)],
        compiler_params=_params("parallel", "parallel", "arbitrary"),
        name="inproj",
    )(x, g, sc, sh, w_main, w_f)


def _conv_kernel(a_ref, g_ref, ah_ref, gh_ref, w_ref, b_ref, lg_ref, lb_ref, o_ref, buf, ybuf, shifted,
                 *, taps, ts, halo, rc):
    cw = a_ref.shape[-1]
    buf[halo:halo + ts, :] = a_ref[...] * _sigmoid(g_ref[...])
    prev = ah_ref[...] * _sigmoid(gh_ref[...])
    buf[0:halo, :] = jnp.where(pl.program_id(1) > 0, prev, 0.0)
    first = halo - (taps - 1)
    for c in range(cw // LANES):
        cs = slice(c * LANES, (c + 1) * LANES)
        for r in range(ts // rc):
            acc = b_ref[:, cs]
            for rem in range(8):
                group = [(k, first + k - rem) for k in range(taps) if (first + k) % 8 == rem]
                if not group:
                    continue
                lo = min(a for _, a in group)
                hi = max(a for _, a in group)
                shifted[0:hi - lo + rc, :] = buf[r * rc + rem + lo:r * rc + rem + hi + rc, cs]
                for k, a in group:
                    acc = acc + w_ref[k:k + 1, cs] * shifted[a - lo:a - lo + rc, :]
            ybuf[r * rc:(r + 1) * rc, cs] = acc
    y = ybuf[...]
    mu = jnp.mean(y, axis=-1, keepdims=True)
    yc = y - mu
    var = jnp.mean(yc * yc, axis=-1, keepdims=True)
    yn = yc * lax.rsqrt(var + EPS) * lg_ref[...] + lb_ref[...]
    o_ref[...] = (yn * _sigmoid(yn)).astype(o_ref.dtype)


def _conv_branch(z, cw, dw_w, dw_b, ln_g, ln_b):
    b, s, _ = z.shape
    taps = dw_w.shape[0]
    halo = 32
    assert taps - 1 <= halo
    ts = _pick(s, 256)
    rc = _pick(ts, 128)
    hb = ts // halo
    kern = functools.partial(_conv_kernel, taps=taps, ts=ts, halo=halo, rc=rc)
    cur = lambda col: pl.BlockSpec((None, ts, cw), lambda bi, i: (bi, i, col))
    prev = lambda col: pl.BlockSpec((None, halo, cw), lambda bi, i: (bi, jnp.maximum(i * hb - 1, 0), col))
    vec = pl.BlockSpec((1, cw), lambda bi, i: (0, 0))
    return pl.pallas_call(
        kern,
        grid=(b, s // ts),
        in_specs=[cur(0), cur(1), prev(0), prev(1),
                  pl.BlockSpec((taps, cw), lambda bi, i: (0, 0)), vec, vec, vec],
        out_specs=pl.BlockSpec((None, ts, cw), lambda bi, i: (bi, i, 0)),
        out_shape=jax.ShapeDtypeStruct((b, s, cw), BF16),
        scratch_shapes=[pltpu.VMEM((halo + ts, cw), F32), pltpu.VMEM((ts, cw), F32),
                        pltpu.VMEM((rc + halo, LANES), F32)],
        compiler_params=_params("parallel", "parallel"),
        name="conv_branch",
    )(z, z, z, z, dw_w, dw_b.reshape(1, cw), ln_g.reshape(1, cw), ln_b.reshape(1, cw))


def _shift_rows(v, d, fill):
    if d % 8 == 0:
        return jnp.concatenate([jnp.full((d, v.shape[1]), fill, v.dtype), v[:v.shape[0] - d]], axis=0)
    rows = lax.broadcasted_iota(jnp.int32, v.shape, 0)
    return jnp.where(rows >= d, pltpu.roll(v, d, axis=0), fill)


def _scan_linear(a, bv):
    d = 1
    while d < a.shape[0]:
        bv = a * _shift_rows(bv, d, 0.0) + bv
        a = a * _shift_rows(a, d, 1.0)
        d *= 2
    return a, bv


def _scan_sum(v):
    d = 1
    while d < v.shape[0]:
        v = v + _shift_rows(v, d, 0.0)
        d *= 2
    return v


def _cum_kernel(zf_ref, fb_ref, o_ref, car, *, ts, rs):
    @pl.when(pl.program_id(1) == 0)
    def _():
        car[...] = jnp.zeros_like(car)

    h = car[...]
    for r in range(ts // rs):
        x = zf_ref[r * rs:(r + 1) * rs, :] + fb_ref[...]
        log_f = jnp.minimum(x, 0.0) - jnp.log1p(jnp.exp(-jnp.abs(x)))
        v = _scan_sum(log_f) + h
        o_ref[r * rs:(r + 1) * rs, :] = v
        h = v[rs - 1:rs, :]
    car[...] = h


def _forget_cumsum(zf, f_bias_pad):
    b, s, _ = zf.shape
    ts = _pick(s, 512)
    rs = _pick(ts, 128)
    return pl.pallas_call(
        functools.partial(_cum_kernel, ts=ts, rs=rs),
        grid=(b, s // ts),
        in_specs=[pl.BlockSpec((None, ts, LANES), lambda bi, i: (bi, i, 0)),
                  pl.BlockSpec((1, LANES), lambda bi, i: (0, 0))],
        out_specs=pl.BlockSpec((None, ts, LANES), lambda bi, i: (bi, i, 0)),
        out_shape=jax.ShapeDtypeStruct((b, s, LANES), F32),
        scratch_shapes=[pltpu.VMEM((1, LANES), F32)],
        compiler_params=_params("parallel", "arbitrary"),
        name="forget_cumsum",
    )(zf, f_bias_pad)


def _gelu_tanh(x):
    return 0.5 * x * (1.0 + jnp.tanh(0.7978845608028654 * (x + 0.044715 * (x * x * x))))


def _lru_kernel(u_ref, uh_ref, gate_ref, cw_ref, cb_ref, wr_ref, br_ref, wi_ref, bi_ref, lam_ref, og_ref,
                o_ref, ubuf, abuf, bbuf, car, *, taps, ts, halo, rs):
    lw = u_ref.shape[-1]

    @pl.when(pl.program_id(1) == 0)
    def _():
        car[...] = jnp.zeros_like(car)

    ubuf[halo:halo + ts, :] = u_ref[...]
    ubuf[0:halo, :] = jnp.where(pl.program_id(1) > 0, uh_ref[...], 0.0)
    xc = cb_ref[...]
    for k in range(taps):
        off = halo - (taps - 1) + k
        xc = xc + cw_ref[k:k + 1, :] * ubuf[off:off + ts, :]
    xb = xc.astype(BF16)
    r = _sigmoid(jnp.dot(xb, wr_ref[...], preferred_element_type=F32) + br_ref[...])
    ig = _sigmoid(jnp.dot(xb, wi_ref[...], preferred_element_type=F32) + bi_ref[...])
    nl = -lam_ref[...]
    softplus = jnp.maximum(nl, 0.0) + jnp.log1p(jnp.exp(-jnp.abs(nl)))
    log_a = (-LRU_C) * r * softplus
    th = jnp.tanh(log_a)
    abuf[...] = jnp.exp(log_a)
    bbuf[...] = jnp.sqrt(-2.0 * th / (1.0 - th)) * (ig * xc)
    for c in range(lw // LANES):
        cs = slice(c * LANES, (c + 1) * LANES)
        h = car[:, cs]
        for q in range(ts // rs):
            rows = slice(q * rs, (q + 1) * rs)
            a_cum, b_cum = _scan_linear(abuf[rows, cs], bbuf[rows, cs])
            hv = b_cum + a_cum * h
            bbuf[rows, cs] = hv
            h = hv[rs - 1:rs, :]
        car[:, cs] = h
    y = bbuf[...] * _gelu_tanh(gate_ref[...])
    y = y * lax.rsqrt(jnp.mean(y * y, axis=-1, keepdims=True) + EPS) * og_ref[...]
    o_ref[...] = y.astype(o_ref.dtype)


def _lru_branch(z, lw, col_x, conv_w, conv_b, wr_bd, b_r, wi_bd, b_i, lam, out_g):
    b, s, _ = z.shape
    taps = conv_w.shape[0]
    halo = 8
    assert taps - 1 <= halo
    ts = _pick(s, 256)
    rs = _pick(ts, 128)
    hb = ts // halo
    kern = functools.partial(_lru_kernel, taps=taps, ts=ts, halo=halo, rs=rs)
    vec = pl.BlockSpec((1, lw), lambda bi, i: (0, 0))
    mat = pl.BlockSpec((lw, lw), lambda bi, i: (0, 0))
    return pl.pallas_call(
        kern,
        grid=(b, s // ts),
        in_specs=[
            pl.BlockSpec((None, ts, lw), lambda bi, i: (bi, i, col_x)),
            pl.BlockSpec((None, halo, lw), lambda bi, i: (bi, jnp.maximum(i * hb - 1, 0), col_x)),
            pl.BlockSpec((None, ts, lw), lambda bi, i: (bi, i, col_x + 1)),
            pl.BlockSpec((taps, lw), lambda bi, i: (0, 0)), vec, mat, vec, mat, vec, vec, vec,
        ],
        out_specs=pl.BlockSpec((None, ts, lw), lambda bi, i: (bi, i, 0)),
        out_shape=jax.ShapeDtypeStruct((b, s, lw), BF16),
        scratch_shapes=[pltpu.VMEM((halo + ts, lw), F32), pltpu.VMEM((ts, lw), F32),
                        pltpu.VMEM((ts, lw), F32), pltpu.VMEM((1, lw), F32)],
        compiler_params=_params("parallel", "arbitrary"),
        name="lru_branch",
    )(z, z, z, conv_w, conv_b.reshape(1, lw), wr_bd, b_r.reshape(1, lw), wi_bd, b_i.reshape(1, lw),
      lam.reshape(1, lw), out_g.reshape(1, lw))


def _split3(t):
    hi = t.astype(BF16).astype(F32)
    r = t - hi
    mid = r.astype(BF16).astype(F32)
    return hi, mid, r - mid


def _attn_kernel(q_ref, k_ref, v_ref, cum_ref, o_ref, kaug, vaug, qaug, s_a, s_b, p_a, p_b, al_a, al_b, m_ref, acc_ref,
                 *, tq, tk, scale):
    h = pl.program_id(1)
    i = pl.program_id(2)
    s_len, hd = k_ref.shape
    inv_scale = 1.0 / scale
    c_exp = scale * 1.4426950408889634
    s_buf, p_buf, al_buf = (s_a, s_b), (p_a, p_b), (al_a, al_b)

    def head_cum(start, size):
        lane = lax.broadcasted_iota(jnp.int32, (size, LANES), 1)
        rows = pl.ds(pl.multiple_of(start, size), size)
        return jnp.sum(jnp.where(lane == h, cum_ref[rows, :], 0.0), axis=-1, keepdims=True) * inv_scale

    def key_rows(j):
        return pl.ds(pl.multiple_of(j * tk, tk), tk)

    @pl.when(i == 0)
    def _():
        lane = lax.broadcasted_iota(jnp.int32, (tk, LANES), 1)

        def body(t, carry):
            rows = key_rows(t)
            hi, mid, lo = _split3(-head_cum(t * tk, tk))
            aug = jnp.where(lane < 3, 1.0, jnp.where(lane == 3, hi, jnp.where(lane == 4, mid,
                            jnp.where(lane == 5, lo, 0.0))))
            kaug[t, 0:hd, :] = k_ref[rows, :].T.astype(BF16)
            kaug[t, hd:2 * hd, :] = aug.T.astype(BF16)
            vaug[rows, 0:hd] = v_ref[rows, :].astype(BF16)
            vaug[rows, hd:2 * hd] = jnp.where(lane == 0, 1.0, 0.0).astype(BF16)
            return carry

        lax.fori_loop(0, s_len // tk, body, 0)

    lane_q = lax.broadcasted_iota(jnp.int32, (tq, LANES), 1)
    hi, mid, lo = _split3(head_cum(i * tq, tq))
    qa = jnp.where(lane_q == 0, hi, jnp.where(lane_q == 1, mid, jnp.where(lane_q == 2, lo,
                   jnp.where(lane_q < 6, 1.0, 0.0))))
    qaug[:, 0:hd] = q_ref[...].astype(BF16)
    qaug[:, hd:2 * hd] = qa.astype(BF16)

    def scores_into(dst, j, masked=False):
        s = jnp.dot(qaug[...], kaug[j], preferred_element_type=F32)
        if masked:
            row = lax.broadcasted_iota(jnp.int32, s.shape, 0)
            col = lax.broadcasted_iota(jnp.int32, s.shape, 1)
            s = jnp.where(col <= row, s, -jnp.inf)
        dst[...] = s

    def softmax_into(src, dst_p, dst_alpha):
        s = src[...]
        m_old = m_ref[...]
        m_new = jnp.maximum(m_old, jnp.max(s, axis=-1, keepdims=True))
        dst_alpha[...] = jnp.exp2((m_old - m_new) * c_exp)
        m_ref[...] = m_new
        dst_p[...] = jnp.exp2((s - jnp.concatenate([m_new] * (tk // LANES), axis=1)) * c_exp).astype(BF16)

    def accumulate(src_alpha, src_p, j):
        alpha = jnp.concatenate([src_alpha[...]] * (acc_ref.shape[1] // LANES), axis=1)
        acc_ref[...] = alpha * acc_ref[...] + jnp.dot(src_p[...], vaug[key_rows(j), :], preferred_element_type=F32)

    def blk(t):
        return jnp.where(t == 0, i, jnp.maximum(t - 1, 0))

    def tick(t, par):
        scores_into(s_buf[1 - par], t)
        softmax_into(s_buf[par], p_buf[par], al_buf[par])
        accumulate(al_buf[1 - par], p_buf[1 - par], blk(t - 1))

    scores_into(s_buf[0], i, masked=True)
    m_ref[...] = jnp.full(m_ref.shape, -jnp.inf, F32)
    acc_ref[...] = jnp.zeros(acc_ref.shape, F32)
    p_buf[1][...] = jnp.zeros(p_buf[1].shape, BF16)
    al_buf[1][...] = jnp.ones(al_buf[1].shape, F32)

    def body(u, carry):
        tick(2 * u, 0)
        tick(2 * u + 1, 1)
        return carry

    lax.fori_loop(0, i // 2, body, 0)
    odd = i % 2 == 1

    @pl.when(odd)
    def _():
        tick(i - 1, 0)

    def drain(par):
        softmax_into(s_buf[par], p_buf[par], al_buf[par])
        accumulate(al_buf[1 - par], p_buf[1 - par], blk(i - 1))
        accumulate(al_buf[par], p_buf[par], blk(i))

    @pl.when(odd)
    def _():
        drain(1)

    @pl.when(jnp.logical_not(odd))
    def _():
        drain(0)

    o_ref[...] = acc_ref[:, 0:hd] / acc_ref[:, hd:hd + 1]


def _attention(z, cum, heads, hd, col_q, col_k, col_v):
    b, s, _ = z.shape
    tq = _pick(s, 512)
    tk = tq
    kern = functools.partial(_attn_kernel, tq=tq, tk=tk, scale=hd ** -0.5)
    return pl.pallas_call(
        kern,
        grid=(b, heads, s // tq),
        in_specs=[
            pl.BlockSpec((None, tq, hd), lambda bi, h, i: (bi, i, col_q + h)),
            pl.BlockSpec((None, s, hd), lambda bi, h, i: (bi, 0, col_k + h)),
            pl.BlockSpec((None, s, hd), lambda bi, h, i: (bi, 0, col_v + h)),
            pl.BlockSpec((None, s, LANES), lambda bi, h, i: (bi, 0, 0)),
        ],
        out_specs=pl.BlockSpec((None, tq, hd), lambda bi, h, i: (bi, i, h)),
        out_shape=jax.ShapeDtypeStruct((b, s, heads * hd), F32),
        scratch_shapes=[pltpu.VMEM((s // tk, 2 * hd, tk), BF16), pltpu.VMEM((s, 2 * hd), BF16),
                        pltpu.VMEM((tq, 2 * hd), BF16),
                        pltpu.VMEM((tq, tk), F32), pltpu.VMEM((tq, tk), F32),
                        pltpu.VMEM((tq, tk), BF16), pltpu.VMEM((tq, tk), BF16),
                        pltpu.VMEM((tq, LANES), F32), pltpu.VMEM((tq, LANES), F32), pltpu.VMEM((tq, LANES), F32),
                        pltpu.VMEM((tq, 2 * hd), F32)],
        compiler_params=_params("arbitrary", "arbitrary", "arbitrary"),
        name="fox_attention",
    )(z, z, z, cum)


def _outproj_kernel(yc_ref, o_ref, yl_ref, fg_ref, w1_ref, w2_ref, w3_ref, x_ref, g1_ref, out_ref, yf_scr):
    @pl.when(pl.program_id(2) == 0)
    def _():
        o = o_ref[...]
        yf = o * lax.rsqrt(jnp.mean(o * o, axis=-1, keepdims=True) + EPS) * fg_ref[...]
        yf_scr[...] = yf.astype(BF16)

    y = jnp.dot(yc_ref[...], w1_ref[...], preferred_element_type=F32)
    y = y + jnp.dot(yf_scr[...], w2_ref[...], preferred_element_type=F32)
    y = y + jnp.dot(yl_ref[...], w3_ref[...], preferred_element_type=F32)
    out_ref[...] = x_ref[...] + g1_ref[...] * y


def _outproj(y_conv, o_fox, y_lru, fox_g, w1, w2, w3, x, g1):
    b, s, d = x.shape
    cw, fw, lw = y_conv.shape[-1], o_fox.shape[-1], y_lru.shape[-1]
    tm = _pick(s, 512)
    tn = _pick(d, 1024)
    act = lambda w: pl.BlockSpec((None, tm, w), lambda bi, i, j: (bi, i, 0))
    wgt = lambda w: pl.BlockSpec((w, tn), lambda bi, i, j: (0, j))
    return pl.pallas_call(
        _outproj_kernel,
        grid=(b, s // tm, d // tn),
        in_specs=[act(cw), act(fw), act(lw), pl.BlockSpec((1, fw), lambda bi, i, j: (0, 0)),
                  wgt(cw), wgt(fw), wgt(lw),
                  pl.BlockSpec((None, tm, tn), lambda bi, i, j: (bi, i, j)),
                  pl.BlockSpec((None, 1, tn), lambda bi, i, j: (bi, 0, j))],
        out_specs=pl.BlockSpec((None, tm, tn), lambda bi, i, j: (bi, i, j)),
        out_shape=jax.ShapeDtypeStruct((b, s, d), F32),
        scratch_shapes=[pltpu.VMEM((tm, fw), BF16)],
        compiler_params=_params("parallel", "parallel", "arbitrary"),
        name="outproj",
    )(y_conv, o_fox, y_lru, fox_g.reshape(1, fw), w1, w2, w3, x, g1)


def _bf16_bits(v):
    return lax.bitcast_convert_type(v.astype(BF16).astype(F32), jnp.uint32)


def _pack_halves(v):
    half = v.shape[1] // 2
    word = _bf16_bits(v[:, half:]) | (_bf16_bits(v[:, :half]) >> 16)
    return lax.bitcast_convert_type(word, jnp.int32)


def _unpack_halves(w):
    word = lax.bitcast_convert_type(w, jnp.uint32)
    lo = lax.bitcast_convert_type(word << 16, F32)
    hi = lax.bitcast_convert_type(word & jnp.uint32(0xFFFF0000), F32)
    return lo, hi


def _router_kernel(x_ref, g_ref, sc_ref, sh_ref, whi_ref, wlo_ref, rb_ref, h2_ref, ri_ref, rg_ref, cnt_ref, car,
                   *, n_groups, per_group):
    @pl.when(jnp.logical_and(pl.program_id(0) == 0, pl.program_id(1) == 0))
    def _():
        car[...] = jnp.zeros_like(car)

    h2 = _modulated_norm(x_ref[...], g_ref[...], sc_ref[...], sh_ref[...])
    h2_ref[...] = _pack_halves(h2)
    hi = h2.astype(BF16)
    lo = (h2 - hi.astype(F32)).astype(BF16)
    logits = (jnp.dot(hi, whi_ref[...], preferred_element_type=F32)
              + jnp.dot(lo, whi_ref[...], preferred_element_type=F32)
              + jnp.dot(hi, wlo_ref[...], preferred_element_type=F32)) + rb_ref[...]
    lane_i = lax.broadcasted_iota(jnp.int32, logits.shape, 1)
    lane = lane_i.astype(F32)
    big = float(LANES)

    def first_lane(hit):
        return jnp.min(jnp.where(hit, lane, big), axis=-1, keepdims=True)

    gmask = lane_i < n_groups
    lg1 = jnp.where(gmask, logits, -jnp.inf)
    m1 = jnp.max(lg1, axis=-1, keepdims=True)
    e1 = jnp.exp(lg1 - m1)
    p1 = e1 / jnp.sum(e1, axis=-1, keepdims=True)
    p_grp = jnp.max(p1, axis=-1, keepdims=True)
    grp = first_lane((p1 == p_grp) & gmask)
    lo_lane = n_groups + grp * per_group
    emask = (lane >= lo_lane) & (lane < lo_lane + per_group)
    lg2 = jnp.where(emask, logits, -jnp.inf)
    m2 = jnp.max(lg2, axis=-1, keepdims=True)
    e2 = jnp.exp(lg2 - m2)
    p2 = e2 / jnp.sum(e2, axis=-1, keepdims=True)
    v1 = jnp.max(jnp.where(emask, p2, -1.0), axis=-1, keepdims=True)
    i1 = first_lane((p2 == v1) & emask)
    rest = emask & (lane != i1)
    v2 = jnp.max(jnp.where(rest, p2, -1.0), axis=-1, keepdims=True)
    i2 = first_lane((p2 == v2) & rest)
    denom = v1 + v2
    gate1 = p_grp * (v1 / denom)
    gate2 = p_grp * (v2 / denom)
    tm = logits.shape[0]
    picked = jnp.where((lane == i1) | (lane == i2), 1.0, 0.0)
    tri = jnp.where(lax.broadcasted_iota(jnp.int32, (tm, tm), 0) > lax.broadcasted_iota(jnp.int32, (tm, tm), 1),
                    1.0, 0.0).astype(BF16)
    before = jnp.dot(tri, picked.astype(BF16), preferred_element_type=F32) + car[...]
    rank1 = jnp.sum(jnp.where(lane == i1, before, 0.0), axis=-1, keepdims=True)
    rank2 = jnp.sum(jnp.where(lane == i2, before, 0.0), axis=-1, keepdims=True)
    total = car[...] + jnp.sum(picked, axis=0, keepdims=True)
    car[...] = total
    cnt_ref[...] = total
    info = jnp.where(lane_i == 0, i1 - n_groups, jnp.where(lane_i == 1, i2 - n_groups,
                     jnp.where(lane_i == 2, rank1, jnp.where(lane_i == 3, rank2, 0.0))))
    ri_ref[...] = info.astype(jnp.int32)
    rg_ref[...] = jnp.where(lane_i == 0, gate1, jnp.where(lane_i == 1, gate2, 0.0))


def _router(x, g, sc, sh, w_hi, w_lo, r_bias, n_groups, per_group):
    b, s, d = x.shape
    tm = _pick(s, 512)
    kern = functools.partial(_router_kernel, n_groups=n_groups, per_group=per_group)
    row = lambda w: pl.BlockSpec((None, tm, w), lambda bi, i: (bi, i, 0))
    mod = pl.BlockSpec((None, 1, d), lambda bi, i: (bi, 0, 0))
    wsp = pl.BlockSpec((d, LANES), lambda bi, i: (0, 0))
    return pl.pallas_call(
        kern,
        grid=(b, s // tm),
        in_specs=[row(d), pl.BlockSpec((1, d), lambda bi, i: (0, 0)), mod, mod, wsp, wsp,
                  pl.BlockSpec((1, LANES), lambda bi, i: (0, 0))],
        out_specs=[row(d // 2), row(LANES), row(LANES), pl.BlockSpec((1, LANES), lambda bi, i: (0, 0))],
        out_shape=[jax.ShapeDtypeStruct((b, s, d // 2), jnp.int32), jax.ShapeDtypeStruct((b, s, LANES), jnp.int32),
                   jax.ShapeDtypeStruct((b, s, LANES), F32), jax.ShapeDtypeStruct((1, LANES), F32)],
        scratch_shapes=[pltpu.VMEM((1, LANES), F32)],
        compiler_params=_params("arbitrary", "arbitrary"),
        name="router",
    )(x, g, sc, sh, w_hi, w_lo, r_bias)


def _moe_kernel(bexp_ref, rtok_ref, wslot_ref, nexte_ref, h2_hbm, wg_hbm, wu_hbm, wd_hbm, y_ref,
                *scratch, rows, layer):
    xbufs = scratch[:EXPERT_GATHER_DEPTH]
    sem, wg_bf, wu_bf, wd_bf, wg_raw, wu_raw, wd_raw, wsem = scratch[EXPERT_GATHER_DEPTH:]
    b = pl.program_id(0)
    nb = pl.num_programs(0)
    depth = EXPERT_GATHER_DEPTH

    def row_copy(tok, r, slot):
        return pltpu.make_async_copy(h2_hbm.at[pl.ds(tok, 1)], xbufs[slot].at[pl.ds(r, 1)], sem.at[slot])

    def wait_block(slot):
        pltpu.make_async_copy(h2_hbm.at[pl.ds(0, rows)], xbufs[slot], sem.at[slot]).wait()

    @pl.when(b == 0)
    def _():
        for ahead in range(depth - 1):
            def body(r, carry, ahead=ahead):
                row_copy(rtok_ref[ahead * rows + r], r, ahead).start()
                return carry

            lax.fori_loop(0, rows, body, 0)

    def weight_copies(e, s):
        return [pltpu.make_async_copy(src.at[layer, e], dst.at[s], wsem.at[s])
                for src, dst in ((wg_hbm, wg_raw), (wu_hbm, wu_raw), (wd_hbm, wd_raw))]

    @pl.when(b == 0)
    def _():
        for copy in weight_copies(bexp_ref[0], 0):
            copy.start()

    changed = jnp.logical_or(b == 0, bexp_ref[b] != bexp_ref[jnp.maximum(b - 1, 0)])

    @pl.when(changed)
    def _():
        s = wslot_ref[b]
        for copy in weight_copies(bexp_ref[b], s):
            copy.wait()
        for static_s in range(2):
            @pl.when(s == static_s)
            def _():
                wg_bf[...] = wg_raw[static_s].astype(BF16)
                wu_bf[...] = wu_raw[static_s].astype(BF16)
                wd_bf[...] = wd_raw[static_s].astype(BF16)

        nxt = nexte_ref[b]

        @pl.when(nxt >= 0)
        def _():
            for copy in weight_copies(nxt, 1 - s):
                copy.start()

    def block(slot):
        wait_block(slot)
        ahead_slot = (slot + depth - 1) % depth
        base = jnp.minimum(b + depth - 1, nb - 1) * rows
        for r in range(rows):
            row_copy(rtok_ref[base + r], r, ahead_slot).start(priority=r % 2)
        lo, hi = _unpack_halves(xbufs[slot][...])
        xb = jnp.concatenate([lo.astype(BF16), hi.astype(BF16)], axis=1)
        gt = jnp.dot(xb, wg_bf[...], preferred_element_type=F32)
        up = jnp.dot(xb, wu_bf[...], preferred_element_type=F32)
        hb = (gt * _sigmoid(gt) * up).astype(BF16)
        y_ref[...] = _pack_halves(jnp.dot(hb, wd_bf[...], preferred_element_type=F32))

        @pl.when(b == nb - 1)
        def _():
            for other in range(1, depth):
                wait_block((slot + other) % depth)

    for static_slot in range(depth):
        @pl.when(b % depth == static_slot)
        def _(static_slot=static_slot):
            block(static_slot)


def _moe_experts(block_exp, row_tok, h2_rows, w_gate, w_up, w_down, layer):
    _, _, d, f = w_gate.shape
    rows = MOE_ROWS
    n_blocks = block_exp.shape[0]
    changed = jnp.concatenate([jnp.ones((1,), jnp.int32), (block_exp[1:] != block_exp[:-1]).astype(jnp.int32)])
    w_slot = (jnp.cumsum(changed) - 1) % 2
    n_le = jnp.sum((block_exp[None, :] <= block_exp[:, None]).astype(jnp.int32), axis=1)
    next_exp = jnp.where(n_le < n_blocks, block_exp[jnp.minimum(n_le, n_blocks - 1)], -1)
    any_spec = pl.BlockSpec(memory_space=pl.ANY)
    grid_spec = pltpu.PrefetchScalarGridSpec(
        num_scalar_prefetch=4,
        grid=(n_blocks,),
        in_specs=[any_spec, any_spec, any_spec, any_spec],
        out_specs=pl.BlockSpec((rows, d // 2), lambda bi, *_: (bi, 0)),
        scratch_shapes=[pltpu.VMEM((rows, d // 2), jnp.int32)] * EXPERT_GATHER_DEPTH + [
                        pltpu.SemaphoreType.DMA((EXPERT_GATHER_DEPTH,)),
                        pltpu.VMEM((d, f), BF16), pltpu.VMEM((d, f), BF16), pltpu.VMEM((f, d), BF16),
                        pltpu.VMEM((2, d, f), F32), pltpu.VMEM((2, d, f), F32), pltpu.VMEM((2, f, d), F32),
                        pltpu.SemaphoreType.DMA((2,))],
    )
    return pl.pallas_call(
        functools.partial(_moe_kernel, rows=rows, layer=layer),
        grid_spec=grid_spec,
        out_shape=jax.ShapeDtypeStruct((n_blocks * rows, d // 2), jnp.int32),
        compiler_params=_params("arbitrary"),
        name="moe_experts",
    )(block_exp, row_tok, w_slot.astype(jnp.int32), next_exp.astype(jnp.int32), h2_rows, w_gate, w_up, w_down)


def _combine_kernel(dest_ref, y_hbm, x_ref, rg_ref, g2_ref, fg_ref, out_ref, *scratch, tm, final_norm):
    ybufs, sem = scratch[:GATHER_DEPTH], scratch[GATHER_DEPTH]
    i = pl.program_id(0)
    n = pl.num_programs(0)
    depth = GATHER_DEPTH

    def start(row, r, k, slot):
        copy = pltpu.make_async_copy(y_hbm.at[pl.ds(row, 1)], ybufs[slot].at[k, pl.ds(r, 1)], sem.at[slot])
        copy.start(priority=k % 2)

    def wait_tile(slot):
        for k in range(TOP_K):
            pltpu.make_async_copy(y_hbm.at[pl.ds(0, tm)], ybufs[slot].at[k], sem.at[slot]).wait()

    @pl.when(i == 0)
    def _():
        for ahead in range(depth - 1):
            def body(r, carry, ahead=ahead):
                for k in range(TOP_K):
                    start(dest_ref[(ahead * tm + r) * TOP_K + k], r, k, ahead)
                return carry

            lax.fori_loop(0, tm, body, 0)

    def tile(slot):
        wait_tile(slot)
        ahead_slot = (slot + depth - 1) % depth
        base = jnp.minimum(i + depth - 1, n - 1) * (tm * TOP_K)
        for r in range(tm):
            for k in range(TOP_K):
                start(dest_ref[base + r * TOP_K + k], r, k, ahead_slot)
        gates = rg_ref[...]
        gate = [jnp.broadcast_to(gates[:, k:k + 1], (tm, LANES)) for k in range(TOP_K)]
        half = out_ref.shape[1] // 2
        for j in range(half // LANES):
            words = slice(j * LANES, (j + 1) * LANES)
            moe_lo = moe_hi = None
            for k in range(TOP_K):
                lo, hi = _unpack_halves(ybufs[slot][k, :, words])
                moe_lo = lo * gate[k] if moe_lo is None else moe_lo + lo * gate[k]
                moe_hi = hi * gate[k] if moe_hi is None else moe_hi + hi * gate[k]
            for cols, moe in ((words, moe_lo), (slice(half + j * LANES, half + (j + 1) * LANES), moe_hi)):
                out_ref[:, cols] = x_ref[:, cols] + g2_ref[:, cols] * moe
        if final_norm:
            v = out_ref[...]
            out_ref[...] = v * lax.rsqrt(jnp.mean(v * v, axis=-1, keepdims=True) + EPS) * fg_ref[...]

        @pl.when(i == n - 1)
        def _():
            for other in range(1, depth):
                wait_tile((slot + other) % depth)

    for static_slot in range(depth):
        @pl.when(i % depth == static_slot)
        def _(static_slot=static_slot):
            tile(static_slot)


def _combine(dest, y, x_flat, rg_flat, g2, final_g, seq, final_norm):
    t, d = x_flat.shape
    tm = _pick(seq, 128)
    grid_spec = pltpu.PrefetchScalarGridSpec(
        num_scalar_prefetch=1,
        grid=(t // tm,),
        in_specs=[
            pl.BlockSpec(memory_space=pl.ANY),
            pl.BlockSpec((tm, d), lambda i, de: (i, 0)),
            pl.BlockSpec((tm, LANES), lambda i, de: (i, 0)),
            pl.BlockSpec((None, 1, d), lambda i, de: ((i * tm) // seq, 0, 0)),
            pl.BlockSpec((1, d), lambda i, de: (0, 0)),
        ],
        out_specs=pl.BlockSpec((tm, d), lambda i, de: (i, 0)),
        scratch_shapes=[pltpu.VMEM((TOP_K, tm, d // 2), jnp.int32)] * GATHER_DEPTH + [
                        pltpu.SemaphoreType.DMA((GATHER_DEPTH,))],
    )
    return pl.pallas_call(
        functools.partial(_combine_kernel, tm=tm, final_norm=final_norm),
        grid_spec=grid_spec,
        out_shape=jax.ShapeDtypeStruct((t, d), F32),
        compiler_params=_params("arbitrary"),
        name="moe_combine",
    )(dest, y, x_flat, rg_flat, g2, final_g.reshape(1, d))


def _routing_tables(eid, rank, counts, rows):
    n_assign = eid.shape[0]
    n_experts = counts.shape[0]
    padded = (counts + rows - 1) // rows * rows
    pends = jnp.cumsum(padded)
    pstarts = pends - padded
    onehot = eid[:, None] == jnp.arange(n_experts, dtype=jnp.int32)[None, :]
    dest = (jnp.sum(jnp.where(onehot, pstarts[None, :], 0), axis=1) + rank).astype(jnp.int32)
    n_blocks = -(-n_assign // rows) + n_experts
    row_tok = (jnp.arange(n_blocks * rows, dtype=jnp.int32) % (n_assign // TOP_K)).at[dest].set(
        jnp.arange(n_assign, dtype=jnp.int32) // TOP_K)
    first_row = jnp.arange(n_blocks, dtype=jnp.int32) * rows
    block_exp = jnp.minimum(jnp.sum((pends[None, :] <= first_row[:, None]).astype(jnp.int32), axis=1), n_experts - 1)
    return block_exp.astype(jnp.int32), row_tok, dest


def _block_diag(w):
    nb, bw, _ = w.shape
    eye = jnp.eye(nb, dtype=w.dtype)
    return (eye[:, None, :, None] * w[:, :, None, :]).reshape(nb * bw, nb * bw)


def kernel(x, c, w_ada, b_ada, ln1_g, w_in, conv_dw_w, conv_dw_b, conv_ln_g, conv_ln_b, fox_f_bias, fox_out_g, lru_conv_w, lru_conv_b, lru_w_r, lru_b_r, lru_w_i, lru_b_i, lru_lambda, lru_out_g, w_out, ln2_g, w_router_group, b_router_group, w_router_expert, b_router_expert, w_gate, w_up, w_down, final_g):
    b, s, d = x.shape
    depth = w_ada.shape[0]
    cw = conv_dw_b.shape[-1]
    fw = fox_out_g.shape[-1]
    lw = lru_lambda.shape[-1]
    heads = fox_f_bias.shape[-1]
    hd = fw // heads
    n_groups, per_group = b_router_expert.shape[1:]
    n_experts = n_groups * per_group
    off_f = 2 * cw + 3 * fw
    assert cw % LANES == 0 and lw == cw and fw % cw == 0 and hd == LANES and heads <= LANES
    assert n_groups + n_experts <= LANES

    c_pad = jnp.zeros((8, d), F32).at[:b].set(c)
    mod_all = _ada_mod(c_pad, w_ada, b_ada)

    for l in range(depth):
        mod = mod_all[l, :b].reshape(b, 1, 6 * d)
        sh1, sc1, g1, sh2, sc2, g2 = [mod[..., k * d:(k + 1) * d] for k in range(6)]

        w_l = w_in[l]
        w_main = jnp.concatenate([w_l[:, :off_f], w_l[:, off_f + heads:]], axis=1).astype(BF16)
        w_f = jnp.zeros((d, LANES), BF16).at[:, :heads].set(w_l[:, off_f:off_f + heads].astype(BF16))
        z, zf = _inproj(x, ln1_g[l].reshape(1, d), sc1, sh1, w_main, w_f)

        y_conv = _conv_branch(z, cw, conv_dw_w[l], conv_dw_b[l], conv_ln_g[l], conv_ln_b[l])

        f_bias = jnp.zeros((1, LANES), F32).at[0, :heads].set(fox_f_bias[l])
        cum = _forget_cumsum(zf, f_bias)
        o_fox = _attention(z, cum, heads, hd, 2 * cw // hd, (2 * cw + fw) // hd, (2 * cw + 2 * fw) // hd)

        y_lru = _lru_branch(z, lw, off_f // lw, lru_conv_w[l], lru_conv_b[l],
                            _block_diag(lru_w_r[l]).astype(BF16), lru_b_r[l],
                            _block_diag(lru_w_i[l]).astype(BF16), lru_b_i[l], lru_lambda[l], lru_out_g[l])

        wo = w_out[l].astype(BF16)
        x = _outproj(y_conv, o_fox, y_lru, fox_out_g[l], wo[:cw], wo[cw:cw + fw], wo[cw + fw:], x, g1)

        w_r = jnp.concatenate(
            [w_router_group[l], w_router_expert[l].transpose(1, 0, 2).reshape(d, n_experts)], axis=1)
        w_r = jnp.zeros((d, LANES), F32).at[:, :n_groups + n_experts].set(w_r)
        w_hi = w_r.astype(BF16)
        w_lo = (w_r - w_hi.astype(F32)).astype(BF16)
        r_bias = jnp.zeros((1, LANES), F32).at[0, :n_groups + n_experts].set(
            jnp.concatenate([b_router_group[l], b_router_expert[l].reshape(-1)]))
        h2, ri, rg, cnt = _router(x, ln2_g[l].reshape(1, d), sc2, sh2, w_hi, w_lo, r_bias, n_groups, per_group)

        eid = ri[:, :, :TOP_K].reshape(-1)
        rank = ri[:, :, TOP_K:2 * TOP_K].reshape(-1)
        counts = cnt[0, n_groups:n_groups + n_experts].astype(jnp.int32)
        block_exp, row_tok, dest = _routing_tables(eid, rank, counts, MOE_ROWS)
        y = _moe_experts(block_exp, row_tok, h2.reshape(b * s, d // 2), w_gate, w_up, w_down, l)
        x = _combine(dest, y, x.reshape(b * s, d), rg.reshape(b * s, LANES), g2, final_g, s,
                     final_norm=(l == depth - 1)).reshape(b, s, d)

    return x
```
